```python
import math
import jax, jax.numpy as jnp
from jax import lax
import numpy as np

D_MODEL = 1024
BATCH = 2
SEQ = 16384
DEPTH = 2

HEAD_DIM = 64
MIX_WIDTH = D_MODEL
N_HEADS_TOTAL = MIX_WIDTH // HEAD_DIM
N_HEADS_DIFF = N_HEADS_TOTAL // 4
N_HEADS_DIL = (N_HEADS_TOTAL - N_HEADS_DIFF) // 2
N_HEADS_NA = N_HEADS_TOTAL - N_HEADS_DIFF - N_HEADS_DIL
DIFF_QK_DIM = HEAD_DIM // 2
DIL_PATTERNS = ((128, 1), (512, 4), (2048, 16))
GRID_W = 64
NA_KH = 8
NA_KW = 16
D_FF = 2816
CONV_W = 3
ROPE_THETA = 10000.0
Q_BLOCK = 128
EPS = 1e-6
NEG = -1e30

kernel_name = "hybrid_diff_dilated_neighborhood_encoder"


def rms_norm(x, g):
    xf = x.astype(jnp.float32)
    y = xf * lax.rsqrt(jnp.mean(xf * xf, axis=-1, keepdims=True) + EPS)
    return y.astype(x.dtype) * g


def rope(x, pos):
    half = x.shape[-1] // 2
    inv = ROPE_THETA ** (-jnp.arange(half, dtype=jnp.float32) / half)
    ang = pos[..., None] * inv
    cos, sin = jnp.cos(ang), jnp.sin(ang)
    x1 = x[..., :half].astype(jnp.float32)
    x2 = x[..., half:].astype(jnp.float32)
    return jnp.concatenate([x1 * cos - x2 * sin, x2 * cos + x1 * sin], axis=-1).astype(x.dtype)


def diff_attention(q, k, v, diff_lambda, subln_g, layer_idx):
    B, H, S, _, DQK = q.shape
    DH = v.shape[-1]
    lam_init = 0.8 - 0.6 * math.exp(-0.3 * layer_idx)
    lq1, lk1, lq2, lk2 = diff_lambda[0], diff_lambda[1], diff_lambda[2], diff_lambda[3]
    lam = (jnp.exp(jnp.sum(lq1 * lk1).astype(jnp.float32))
           - jnp.exp(jnp.sum(lq2 * lk2).astype(jnp.float32)) + lam_init)
    scale = DQK ** -0.5
    n_blk = S // Q_BLOCK
    qb = q.reshape(B, H, n_blk, Q_BLOCK, 2, DQK).transpose(2, 0, 1, 3, 4, 5)

    def block(qi):
        s = jnp.einsum('bhqmd,bhkmd->bhmqk', qi, k).astype(jnp.float32) * scale
        p = jax.nn.softmax(s, axis=-1)
        a = p[:, :, 0] - lam * p[:, :, 1]
        return jnp.einsum('bhqk,bhkd->bhqd', a.astype(v.dtype), v)

    o = lax.map(block, qb)
    o = o.transpose(1, 2, 0, 3, 4).reshape(B, H, S, DH)
    return rms_norm(o, subln_g) * (1.0 - lam_init)


def banded_window_attention(q, k, v, r):
    lead = q.shape[:-2]
    L, DH = q.shape[-2:]
    n = -(-L // r)
    padl = [(0, 0)] * len(lead)
    qb = jnp.pad(q, padl + [(0, n * r - L), (0, 0)]).reshape(*lead, n, r, DH)

    def key_blocks(t):
        tp = jnp.pad(t, padl + [(r, (n + 1) * r - L), (0, 0)]).reshape(*lead, n + 2, r, DH)
        return jnp.concatenate([tp[..., :-2, :, :], tp[..., 1:-1, :, :], tp[..., 2:, :, :]], axis=-2)

    kb, vb = key_blocks(k), key_blocks(v)
    qi = jnp.arange(r)[:, None]
    kj = jnp.arange(3 * r)[None, :]
    key_pos = jnp.arange(n)[:, None, None] * r - r + kj
    mask = (kj >= qi) & (kj <= qi + 2 * r) & (key_pos >= 0) & (key_pos < L)
    s = jnp.einsum('...nqd,...nkd->...nqk', qb, kb).astype(jnp.float32) * (DH ** -0.5)
    s = jnp.where(mask, s, NEG)
    m = jnp.max(s, axis=-1, keepdims=True)
    p = jnp.exp(s - m)
    den = jnp.sum(p, axis=-1)
    o = jnp.einsum('...nqk,...nkd->...nqd', (p / den[..., None]).astype(v.dtype), vb)
    lse = m[..., 0] + jnp.log(den)
    return o.reshape(*lead, n * r, DH)[..., :L, :], lse.reshape(*lead, n * r)[..., :L]


def dilated_attention(q, k, v):
    B, H, S, DH = q.shape
    outs, lses = [], []
    for window, dil in DIL_PATTERNS:
        r = window // (2 * dil)
        L = S // dil

        def split(t):
            return t.reshape(B, H, L, dil, DH).transpose(0, 1, 3, 2, 4)

        o, lse = banded_window_attention(split(q), split(k), split(v), r)
        outs.append(o.transpose(0, 1, 3, 2, 4).reshape(B, H, S, DH))
        lses.append(lse.transpose(0, 1, 3, 2).reshape(B, H, S))
    w = jax.nn.softmax(jnp.stack(lses, axis=0), axis=0)
    out = jnp.sum(w[..., None] * jnp.stack(outs, axis=0).astype(jnp.float32), axis=0)
    return out.astype(q.dtype)


def neighborhood_attention(q, k, v, rpb):
    B, H, S, DH = q.shape
    rows = S // GRID_W
    kh = min(NA_KH, rows)
    kw = NA_KW
    qg = q.reshape(B, H, rows, GRID_W, DH)
    kg = k.reshape(B, H, rows, GRID_W, DH)
    vg = v.reshape(B, H, rows, GRID_W, DH)
    r = jnp.arange(rows)
    row_start = jnp.clip(r - kh // 2, 0, rows - kh)
    key_rows = row_start[:, None] + jnp.arange(kh)[None, :]
    kr = kg[:, :, key_rows]
    vr = vg[:, :, key_rows]
    col = jnp.arange(GRID_W)
    col_start = jnp.clip(col - kw // 2, 0, GRID_W - kw)
    col_mask = (col[None, :] >= col_start[:, None]) & (col[None, :] < col_start[:, None] + kw)
    dr = key_rows - r[:, None] + (NA_KH - 1)
    dc = jnp.clip(col[None, :] - col[:, None], -(kw - 1), kw - 1) + (NA_KW - 1)
    bias = rpb[:, dr[:, None, :, None], dc[None, :, None, :]]
    s = jnp.einsum('bhrqd,bhrjkd->bhrqjk', qg, kr).astype(jnp.float32) * (DH ** -0.5)
    s = s + bias[None].astype(jnp.float32)
    s = jnp.where(col_mask[:, None, :], s, NEG)
    p = jax.nn.softmax(s.reshape(B, H, rows, GRID_W, kh * GRID_W), axis=-1)
    o = jnp.einsum('bhrqn,bhrnd->bhrqd', p.astype(v.dtype), vr.reshape(B, H, rows, kh * GRID_W, DH))
    return o.reshape(B, H, S, DH)


def hybrid_mixer(h, w_in, diff_lambda, diff_subln, na_rpb, w_out, layer_idx):
    B, S, _ = h.shape
    proj = h @ w_in
    wa, wb, wc = N_HEADS_DIFF * HEAD_DIM, N_HEADS_DIL * HEAD_DIM, N_HEADS_NA * HEAD_DIM
    offs = np.cumsum([wa, wa, wa, wb, wb, wb, wc, wc, wc])[:-1].tolist()
    qa, ka, va, qb, kb, vb, qc, kc, vc = jnp.split(proj, offs, axis=-1)
    pos = jnp.arange(S, dtype=jnp.float32)

    def heads(t, n_h):
        return t.reshape(B, S, n_h, HEAD_DIM).transpose(0, 2, 1, 3)

    def diff_heads(t):
        return t.reshape(B, S, N_HEADS_DIFF, 2, DIFF_QK_DIM).transpose(0, 2, 1, 3, 4)

    def merge(o):
        return o.transpose(0, 2, 1, 3).reshape(B, S, -1)

    o_a = diff_attention(rope(diff_heads(qa), pos[:, None]), rope(diff_heads(ka), pos[:, None]),
                         heads(va, N_HEADS_DIFF), diff_lambda, diff_subln, layer_idx)
    o_b = dilated_attention(rope(heads(qb, N_HEADS_DIL), pos), rope(heads(kb, N_HEADS_DIL), pos),
                            heads(vb, N_HEADS_DIL))
    o_c = neighborhood_attention(heads(qc, N_HEADS_NA), heads(kc, N_HEADS_NA), heads(vc, N_HEADS_NA), na_rpb)
    o = jnp.concatenate([merge(o_a), merge(o_b), merge(o_c)], axis=-1)
    return o @ w_out


def conv_glu_ffn(h, w_up, conv_w, conv_b, w_down):
    S = h.shape[1]
    g, u = jnp.split(h @ w_up, 2, axis=-1)
    pad = CONV_W // 2
    gp = jnp.pad(g, ((0, 0), (pad, pad), (0, 0)))
    gc = conv_b
    for j in range(CONV_W):
        gc = gc + gp[:, j:j + S] * conv_w[j]
    return (jax.nn.silu(gc) * u) @ w_down


def setup_inputs(seed: int = 0) -> dict:
    key = jax.random.key(seed)
    ks = jax.random.split(key, 16)
    D = D_MODEL

    def nrm(k, shape, s):
        return jax.random.normal(k, shape, jnp.float32) * s

    return {
        "x": nrm(ks[0], (BATCH, SEQ, D), 1.0),
        "c": nrm(ks[1], (BATCH, D), 1.0),
        "w_ada": nrm(ks[2], (DEPTH, D, 6 * D), 0.5 * D ** -0.5),
        "b_ada": nrm(ks[3], (DEPTH, 6 * D), 0.02),
        "g_attn": 1.0 + nrm(ks[4], (DEPTH, D), 0.02),
        "w_in": nrm(ks[5], (DEPTH, D, 3 * MIX_WIDTH), D ** -0.5),
        "diff_lambda": nrm(ks[6], (DEPTH, 4, DIFF_QK_DIM), 0.1),
        "diff_subln": 1.0 + nrm(ks[7], (DEPTH, HEAD_DIM), 0.02),
        "na_rpb": nrm(ks[8], (DEPTH, N_HEADS_NA, 2 * NA_KH - 1, 2 * NA_KW - 1), 0.1),
        "w_out": nrm(ks[9], (DEPTH, MIX_WIDTH, D), MIX_WIDTH ** -0.5),
        "g_ffn": 1.0 + nrm(ks[10], (DEPTH, D), 0.02),
        "w_up": nrm(ks[11], (DEPTH, D, 2 * D_FF), D ** -0.5),
        "conv_w": nrm(ks[12], (DEPTH, CONV_W, D_FF), CONV_W ** -0.5),
        "conv_b": nrm(ks[13], (DEPTH, D_FF), 0.02),
        "w_down": nrm(ks[14], (DEPTH, D_FF, D), D_FF ** -0.5),
        "g_final": 1.0 + nrm(ks[15], (D,), 0.02),
    }


def reference(x, c, w_ada, b_ada, g_attn, w_in, diff_lambda, diff_subln, na_rpb, w_out,
              g_ffn, w_up, conv_w, conv_b, w_down, g_final):
    for l in range(DEPTH):
        mod = jax.nn.silu(c) @ w_ada[l] + b_ada[l]
        sh_a, sc_a, gt_a, sh_f, sc_f, gt_f = [m[:, None, :] for m in jnp.split(mod, 6, axis=-1)]
        h = rms_norm(x, g_attn[l]) * (1.0 + sc_a) + sh_a
        x = x + gt_a * hybrid_mixer(h, w_in[l], diff_lambda[l], diff_subln[l], na_rpb[l], w_out[l], l)
        h = rms_norm(x, g_ffn[l]) * (1.0 + sc_f) + sh_f
        x = x + gt_f * conv_glu_ffn(h, w_up[l], conv_w[l], conv_b[l], w_down[l])
    return rms_norm(x, g_final)
```

```python
import functools
import math

import numpy as np
import jax
import jax.numpy as jnp
from jax import lax
from jax.experimental import pallas as pl
from jax.experimental.pallas import tpu as pltpu

F32 = jnp.float32
BF16 = jnp.bfloat16

HEAD_DIM = 64
N_HEADS_DIFF = 4
N_HEADS_DIL = 6
N_HEADS_NA = 6
W_DIFF = N_HEADS_DIFF * HEAD_DIM
W_DIL = N_HEADS_DIL * HEAD_DIM
W_NA = N_HEADS_NA * HEAD_DIM
DIFF_QK_DIM = HEAD_DIM // 2
DIL_PATTERNS = ((128, 1), (512, 4), (2048, 16))
GRID_W = 64
NA_KH = 8
NA_KW = 16
CONV_W = 3
ROPE_THETA = 10000.0
EPS = 1e-6
NEG = -1e30

LANES = 128
PAIR_W = 2 * HEAD_DIM
TILE = 256
FF_CHUNK = 256
HALO = 16
VMEM_LIMIT = 56 * 1024 * 1024


def _cparams(sem):
    return pltpu.CompilerParams(dimension_semantics=sem, vmem_limit_bytes=VMEM_LIMIT)


def _mod_kernel(ct_ref, w_ref, b_ref, o_ref):
    ct = ct_ref[...]
    s = ct * jax.nn.sigmoid(ct)
    w = w_ref[0]
    rows = [jnp.sum(w * s[:, b:b + 1], axis=0, keepdims=True) for b in range(ct.shape[1])]
    o_ref[0] = jnp.concatenate(rows, axis=0) + b_ref[0]


def _modulation(c, w_ada, b_ada):
    depth, d, n = w_ada.shape
    bsz = c.shape[0]
    tn = 768
    return pl.pallas_call(
        _mod_kernel,
        grid=(depth, n // tn),
        in_specs=[
            pl.BlockSpec((d, bsz), lambda l, j: (0, 0)),
            pl.BlockSpec((1, d, tn), lambda l, j: (l, 0, j)),
            pl.BlockSpec((1, 1, tn), lambda l, j: (l, 0, j)),
        ],
        out_specs=pl.BlockSpec((1, bsz, tn), lambda l, j: (l, 0, j)),
        out_shape=jax.ShapeDtypeStruct((depth, bsz, n), F32),
        compiler_params=_cparams(("arbitrary", "arbitrary")),
        name="adaln_mod",
    )(c.T, w_ada, b_ada.reshape(depth, 1, n))


def _norm_mod(x, g, sc, sh):
    ms = jnp.mean(x * x, axis=-1, keepdims=True)
    return (x * lax.rsqrt(ms + EPS)) * g * (1.0 + sc) + sh


def _rope_slab(acc, cos, sin, group):
    half = group // 2
    lane = lax.broadcasted_iota(jnp.int32, (1, LANES), 1) % group
    first = lane < half
    outs = []
    for c in range(acc.shape[1] // LANES):
        xc = acc[:, c * LANES:(c + 1) * LANES]
        swapped = jnp.where(first, pltpu.roll(xc, LANES - half, axis=1), pltpu.roll(xc, half, axis=1))
        outs.append(xc * cos + swapped * sin)
    return jnp.concatenate(outs, axis=1)


def _proj_kernel(x_ref, mod_ref, g_ref, w_ref, ca_ref, sa_ref, cb_ref, sb_ref,
                 qt_ref, k_ref, vt_ref):
    tm = x_ref.shape[1]
    h = _norm_mod(x_ref[0], g_ref[...], mod_ref[0, 1:2, :], mod_ref[0, 0:1, :]).astype(BF16)
    ca, sa, cb, sb = ca_ref[...], sa_ref[...], cb_ref[...], sb_ref[...]
    width = W_DIFF + W_DIL + W_NA
    slabs = ((0, W_DIFF), (W_DIFF, W_DIL), (W_DIFF + W_DIL, W_NA))

    def slab(part, si):
        off, w = slabs[si]
        acc = jnp.dot(h, w_ref[:, part * width + off: part * width + off + w], preferred_element_type=F32)
        if part < 2 and si == 0:
            acc = _rope_slab(acc, ca, sa, DIFF_QK_DIM)
        elif part < 2 and si == 1:
            acc = _rope_slab(acc, cb, sb, HEAD_DIM)
        return acc

    q_scale = (DIFF_QK_DIM ** -0.5, HEAD_DIM ** -0.5, HEAD_DIM ** -0.5)
    for si, (off, w) in enumerate(slabs):
        q = slab(0, si) * q_scale[si]
        qt_ref[0, off:off + w, :] = q.T.astype(BF16)
        k_ref[0, :, off:off + w] = slab(1, si).astype(BF16)
        v = slab(2, si)
        for t in range(tm // TILE):
            vt_ref[0, t, off:off + w, :] = v[t * TILE:(t + 1) * TILE, :].T.astype(BF16)


def _rope_tables(s, group):
    half = group // 2
    inv = ROPE_THETA ** (-jnp.arange(half, dtype=F32) / half)
    ang = jnp.arange(s, dtype=F32)[:, None] * inv
    cos, sin = jnp.cos(ang), jnp.sin(ang)
    reps = LANES // group
    return (jnp.tile(jnp.concatenate([cos, cos], axis=1), (1, reps)),
            jnp.tile(jnp.concatenate([-sin, sin], axis=1), (1, reps)))


def _projection(x, mod_l, g, w_perm, tables, tm):
    bsz, s, d = x.shape
    width = W_DIFF + W_DIL + W_NA
    tab_spec = pl.BlockSpec((tm, LANES), lambda b, i: (i, 0))
    return pl.pallas_call(
        _proj_kernel,
        grid=(bsz, s // tm),
        in_specs=[
            pl.BlockSpec((1, tm, d), lambda b, i: (b, i, 0)),
            pl.BlockSpec((1, 6, d), lambda b, i: (b, 0, 0)),
            pl.BlockSpec((1, d), lambda b, i: (0, 0)),
            pl.BlockSpec((d, 3 * width), lambda b, i: (0, 0)),
            tab_spec, tab_spec, tab_spec, tab_spec,
        ],
        out_specs=[
            pl.BlockSpec((1, width, tm), lambda b, i: (b, 0, i)),
            pl.BlockSpec((1, tm, width), lambda b, i: (b, i, 0)),
            pl.BlockSpec((1, tm // TILE, width, TILE), lambda b, i: (b, i, 0, 0)),
        ],
        out_shape=[
            jax.ShapeDtypeStruct((bsz, width, s), BF16),
            jax.ShapeDtypeStruct((bsz, s, width), BF16),
            jax.ShapeDtypeStruct((bsz, s // TILE, width, TILE), BF16),
        ],
        compiler_params=_cparams(("arbitrary", "arbitrary")),
        name="norm_proj_rope",
    )(x, mod_l, g.reshape(1, d), w_perm, *tables)


def _masked_queries(qt, n_split):
    row = lax.broadcasted_iota(jnp.int32, (PAIR_W, 1), 0)
    step = PAIR_W // n_split
    return [jnp.where((row >= j * step) & (row < (j + 1) * step), qt, jnp.zeros_like(qt))
            for j in range(n_split)]


def _online_update(j, s, vt_h, m_s, l_s, acc_s):
    m_old = m_s[j]
    m_new = jnp.maximum(m_old, jnp.max(s, axis=0, keepdims=True))
    alpha = jnp.exp(m_old - m_new)
    e = jnp.exp(s - m_new)
    l_s[j] = alpha * l_s[j] + jnp.sum(e, axis=0, keepdims=True)
    pv = jnp.dot(vt_h, e.astype(BF16), preferred_element_type=F32)
    acc_s[j] = acc_s[j] * alpha + pv
    m_s[j] = m_new


def _init_state(m_s, l_s, acc_s):
    m_s[...] = jnp.full(m_s.shape, NEG, F32)
    l_s[...] = jnp.zeros(l_s.shape, F32)
    acc_s[...] = jnp.zeros(acc_s.shape, F32)


def _diff_kernel(lam_init, dl_ref, g_ref, qt_ref, k_ref, vt_ref, o_ref, m_s, l_s, acc_s):
    n_kt = vt_ref.shape[1]
    qs = _masked_queries(qt_ref[0], 4)
    _init_state(m_s, l_s, acc_s)

    def body(kt, carry):
        kb = k_ref[0, pl.ds(pl.multiple_of(kt * TILE, TILE), TILE), :]
        vtb = vt_ref[0, kt]
        for j in range(4):
            hd = j // 2
            s = jnp.dot(kb, qs[j], preferred_element_type=F32)
            _online_update(j, s, vtb[hd * HEAD_DIM:(hd + 1) * HEAD_DIM, :], m_s, l_s, acc_s)
        return carry

    lax.fori_loop(0, n_kt, body, 0)

    dl = dl_ref[...]
    lam = (jnp.exp(jnp.sum(dl[0:1] * dl[1:2], axis=1, keepdims=True))
           - jnp.exp(jnp.sum(dl[2:3] * dl[3:4], axis=1, keepdims=True)) + lam_init)
    g = g_ref[...]
    outs = []
    for hd in range(2):
        o = acc_s[2 * hd] / l_s[2 * hd] - lam * (acc_s[2 * hd + 1] / l_s[2 * hd + 1])
        ms = jnp.mean(o * o, axis=0, keepdims=True)
        outs.append((o * lax.rsqrt(ms + EPS)) * g * (1.0 - lam_init))
    o_ref[0] = jnp.concatenate(outs, axis=0).T.astype(BF16)


def _diff_attention(qt, k, vt, diff_lambda, subln, layer_idx, tq):
    bsz, _, s = qt.shape
    n_pairs = N_HEADS_DIFF // 2
    lam_init = 0.8 - 0.6 * math.exp(-0.3 * layer_idx)
    return pl.pallas_call(
        functools.partial(_diff_kernel, lam_init),
        grid=(bsz, n_pairs, s // tq),
        in_specs=[
            pl.BlockSpec(diff_lambda.shape, lambda b, p, i: (0, 0)),
            pl.BlockSpec((HEAD_DIM, 1), lambda b, p, i: (0, 0)),
            pl.BlockSpec((1, PAIR_W, tq), lambda b, p, i: (b, p, i)),
            pl.BlockSpec((1, s, PAIR_W), lambda b, p, i: (b, 0, p)),
            pl.BlockSpec((1, s // TILE, PAIR_W, TILE), lambda b, p, i: (b, 0, p, 0)),
        ],
        out_specs=pl.BlockSpec((1, tq, PAIR_W), lambda b, p, i: (b, i, p)),
        out_shape=jax.ShapeDtypeStruct((bsz, s, W_DIFF), BF16),
        scratch_shapes=[
            pltpu.VMEM((4, 1, tq), F32),
            pltpu.VMEM((4, 1, tq), F32),
            pltpu.VMEM((4, HEAD_DIM, tq), F32),
        ],
        compiler_params=_cparams(("arbitrary", "arbitrary", "arbitrary")),
        name="diff_attention",
    )(diff_lambda, subln.reshape(HEAD_DIM, 1), qt, k, vt)


def _band_kernel(offsets, per_head_bias, qt_ref, k_ref, vt_ref, bias_ref, o_ref, m_s, l_s, acc_s):
    i = pl.program_id(2)
    n_blk = vt_ref.shape[1]
    qs = _masked_queries(qt_ref[0], 2)
    _init_state(m_s, l_s, acc_s)

    for oi, off in enumerate(offsets):
        kbi = i + off

        @pl.when((kbi >= 0) & (kbi < n_blk))
        def _():
            kb = k_ref[0, pl.ds(pl.multiple_of(kbi * TILE, TILE), TILE), :]
            vtb = vt_ref[0, kbi]
            for hd in range(2):
                if per_head_bias:
                    bias = bias_ref[0, hd, oi * TILE:(oi + 1) * TILE, :]
                else:
                    bias = bias_ref[oi]
                s = jnp.dot(kb, qs[hd], preferred_element_type=F32) + bias
                _online_update(hd, s, vtb[hd * HEAD_DIM:(hd + 1) * HEAD_DIM, :], m_s, l_s, acc_s)

    outs = [acc_s[hd] / l_s[hd] for hd in range(2)]
    o_ref[0] = jnp.concatenate(outs, axis=0).T.astype(BF16)


def _band_attention(qt, k, vt, bias, bias_spec, offsets, per_head_bias, pair0, n_pairs, name):
    bsz, _, s = qt.shape
    return pl.pallas_call(
        functools.partial(_band_kernel, offsets, per_head_bias),
        grid=(bsz, n_pairs, s // TILE),
        in_specs=[
            pl.BlockSpec((1, PAIR_W, TILE), lambda b, p, i: (b, pair0 + p, i)),
            pl.BlockSpec((1, s, PAIR_W), lambda b, p, i: (b, 0, pair0 + p)),
            pl.BlockSpec((1, s // TILE, PAIR_W, TILE), lambda b, p, i: (b, 0, pair0 + p, 0)),
            bias_spec,
        ],
        out_specs=pl.BlockSpec((1, TILE, PAIR_W), lambda b, p, i: (b, i, p)),
        out_shape=jax.ShapeDtypeStruct((bsz, s, n_pairs * PAIR_W), BF16),
        scratch_shapes=[
            pltpu.VMEM((2, 1, TILE), F32),
            pltpu.VMEM((2, 1, TILE), F32),
            pltpu.VMEM((2, HEAD_DIM, TILE), F32),
        ],
        compiler_params=_cparams(("arbitrary", "arbitrary", "arbitrary")),
        name=name,
    )(qt, k, vt, bias)


def _dilated_bias():
    reach = max(w // 2 for w, _ in DIL_PATTERNS)
    n_off = -(-reach // TILE)
    offsets = tuple(range(-n_off, n_off + 1))
    kj = np.arange(TILE)[:, None]
    qi = np.arange(TILE)[None, :]
    tabs = []
    for off in offsets:
        delta = off * TILE + kj - qi
        cnt = np.zeros((TILE, TILE), np.int32)
        for window, dil in DIL_PATTERNS:
            cnt += ((delta % dil == 0) & (np.abs(delta) <= window // 2)).astype(np.int32)
        tabs.append(np.where(cnt > 0, np.log(np.maximum(cnt, 1).astype(np.float64)), NEG))
    return offsets, jnp.asarray(np.stack(tabs), F32)


def _na_bias(rpb, rows):
    kh = min(NA_KH, rows)
    q_rows = TILE // GRID_W
    kj = np.arange(3 * TILE)[:, None]
    qi = np.arange(TILE)[None, :]
    out = []
    for r0 in (0, 2 * q_rows, rows - q_rows):
        r = r0 + qi // GRID_W
        qc = qi % GRID_W
        kr = r0 - q_rows + kj // GRID_W
        kc = kj % GRID_W
        rs = np.clip(r - kh // 2, 0, rows - kh)
        cs = np.clip(qc - NA_KW // 2, 0, GRID_W - NA_KW)
        valid = (kr >= rs) & (kr < rs + kh) & (kc >= cs) & (kc < cs + NA_KW)
        dr = np.clip(kr - r + (NA_KH - 1), 0, 2 * NA_KH - 2)
        dc = np.clip(kc - qc, -(NA_KW - 1), NA_KW - 1) + (NA_KW - 1)
        out.append(jnp.where(valid[None], rpb[:, dr, dc], NEG))
    return jnp.stack(out).astype(F32)


def _outproj_kernel(x_ref, mod_ref, oa_ref, ob_ref, oc_ref, w_ref, y_ref):
    mix = jnp.dot(oa_ref[0], w_ref[0:W_DIFF, :], preferred_element_type=F32)
    mix += jnp.dot(ob_ref[0], w_ref[W_DIFF:W_DIFF + W_DIL, :], preferred_element_type=F32)
    mix += jnp.dot(oc_ref[0], w_ref[W_DIFF + W_DIL:, :], preferred_element_type=F32)
    y_ref[0] = x_ref[0] + mod_ref[0, 2:3, :] * mix


def _out_projection(x, mod_l, o_a, o_b, o_c, w_out, tm):
    bsz, s, d = x.shape
    row = lambda w: pl.BlockSpec((1, tm, w), lambda b, i: (b, i, 0))
    return pl.pallas_call(
        _outproj_kernel,
        grid=(bsz, s // tm),
        in_specs=[
            row(d),
            pl.BlockSpec((1, 6, d), lambda b, i: (b, 0, 0)),
            row(W_DIFF), row(W_DIL), row(W_NA),
            pl.BlockSpec(w_out.shape, lambda b, i: (0, 0)),
        ],
        out_specs=row(d),
        out_shape=jax.ShapeDtypeStruct((bsz, s, d), F32),
        compiler_params=_cparams(("arbitrary", "arbitrary")),
        name="out_proj_residual",
    )(x, mod_l, o_a, o_b, o_c, w_out)


def _ffn_up_kernel(x_ref, xp_ref, xn_ref, mod_ref, g_ref, wg_ref, wu_ref, cw_ref, cb_ref,
                   a_ref, h_s, g_s):
    i = pl.program_id(1)
    tm = x_ref.shape[1]
    g, sc, sh = g_ref[...], mod_ref[0, 4:5, :], mod_ref[0, 3:4, :]
    keep_prev = (i > 0).astype(F32)
    keep_next = (i < pl.num_programs(1) - 1).astype(F32)
    h_s[0:HALO, :] = (_norm_mod(xp_ref[0], g, sc, sh) * keep_prev).astype(BF16)
    h_s[HALO:HALO + tm, :] = _norm_mod(x_ref[0], g, sc, sh).astype(BF16)
    h_s[HALO + tm:, :] = (_norm_mod(xn_ref[0], g, sc, sh) * keep_next).astype(BF16)
    for c in range(wg_ref.shape[0]):
        g_s[...] = jnp.dot(h_s[...], wg_ref[c], preferred_element_type=F32)
        cw = cw_ref[c]
        gc = cb_ref[c]
        for t in range(CONV_W):
            gc = gc + g_s[pl.ds(HALO - CONV_W // 2 + t, tm), :] * cw[t:t + 1, :]
        u = jnp.dot(h_s[HALO:HALO + tm, :], wu_ref[c], preferred_element_type=F32)
        a_ref[0, :, c * FF_CHUNK:(c + 1) * FF_CHUNK] = (gc * jax.nn.sigmoid(gc) * u).astype(BF16)


def _ffn_up(x, mod_l, g, wg, wu, cw, cb, tm):
    bsz, s, d = x.shape
    n_ch = wg.shape[0]
    hb = tm // HALO
    n_hb = s // HALO
    return pl.pallas_call(
        _ffn_up_kernel,
        grid=(bsz, s // tm),
        in_specs=[
            pl.BlockSpec((1, tm, d), lambda b, i: (b, i, 0)),
            pl.BlockSpec((1, HALO, d), lambda b, i: (b, jnp.maximum(i * hb - 1, 0), 0)),
            pl.BlockSpec((1, HALO, d), lambda b, i: (b, jnp.minimum((i + 1) * hb, n_hb - 1), 0)),
            pl.BlockSpec((1, 6, d), lambda b, i: (b, 0, 0)),
            pl.BlockSpec((1, d), lambda b, i: (0, 0)),
            pl.BlockSpec(wg.shape, lambda b, i: (0, 0, 0)),
            pl.BlockSpec(wu.shape, lambda b, i: (0, 0, 0)),
            pl.BlockSpec(cw.shape, lambda b, i: (0, 0, 0)),
            pl.BlockSpec(cb.shape, lambda b, i: (0, 0, 0)),
        ],
        out_specs=pl.BlockSpec((1, tm, n_ch * FF_CHUNK), lambda b, i: (b, i, 0)),
        out_shape=jax.ShapeDtypeStruct((bsz, s, n_ch * FF_CHUNK), BF16),
        scratch_shapes=[
            pltpu.VMEM((tm + 2 * HALO, d), BF16),
            pltpu.VMEM((tm + 2 * HALO, FF_CHUNK), F32),
        ],
        compiler_params=_cparams(("arbitrary", "arbitrary")),
        name="ffn_up_conv_glu",
    )(x, x, x, mod_l, g.reshape(1, d), wg, wu, cw, cb)


def _ffn_down_kernel(final, x_ref, mod_ref, a_ref, w_ref, gf_ref, y_ref):
    y = x_ref[0] + mod_ref[0, 5:6, :] * jnp.dot(a_ref[0], w_ref[...], preferred_element_type=F32)
    if final:
        ms = jnp.mean(y * y, axis=-1, keepdims=True)
        y = (y * lax.rsqrt(ms + EPS)) * gf_ref[...]
    y_ref[0] = y


def _ffn_down(x, mod_l, a, w_down, g_final, final, tm):
    bsz, s, d = x.shape
    return pl.pallas_call(
        functools.partial(_ffn_down_kernel, final),
        grid=(bsz, s // tm),
        in_specs=[
            pl.BlockSpec((1, tm, d), lambda b, i: (b, i, 0)),
            pl.BlockSpec((1, 6, d), lambda b, i: (b, 0, 0)),
            pl.BlockSpec((1, tm, a.shape[2]), lambda b, i: (b, i, 0)),
            pl.BlockSpec(w_down.shape, lambda b, i: (0, 0)),
            pl.BlockSpec((1, d), lambda b, i: (0, 0)),
        ],
        out_specs=pl.BlockSpec((1, tm, d), lambda b, i: (b, i, 0)),
        out_shape=jax.ShapeDtypeStruct((bsz, s, d), F32),
        compiler_params=_cparams(("arbitrary", "arbitrary")),
        name="ffn_down_residual",
    )(x, mod_l, a, w_down, g_final.reshape(1, d))


def kernel(x, c, w_ada, b_ada, g_attn, w_in, diff_lambda, diff_subln, na_rpb, w_out, g_ffn, w_up,
           conv_w, conv_b, w_down, g_final):
    bsz, s, d = x.shape
    depth = w_ada.shape[0]
    d_ff = w_down.shape[1]
    n_ch = d_ff // FF_CHUNK
    tm = 512
    assert s % tm == 0 and tm % TILE == 0 and d_ff % FF_CHUNK == 0 and TILE % GRID_W == 0

    mod = _modulation(c, w_ada, b_ada).reshape(depth, bsz, 6, d)
    tables = _rope_tables(s, DIFF_QK_DIM) + _rope_tables(s, HEAD_DIM)
    dil_offsets, dil_bias = _dilated_bias()
    rows = s // GRID_W
    n_qblk = s // TILE

    wa, wb, wc = W_DIFF, W_DIL, W_NA
    offs = np.cumsum([0, wa, wa, wa, wb, wb, wb, wc, wc, wc])
    order = (0, 3, 6, 1, 4, 7, 2, 5, 8)

    for l in range(depth):
        w_perm = jnp.concatenate([w_in[l][:, offs[j]:offs[j + 1]] for j in order], axis=1).astype(BF16)
        qt, k, vt = _projection(x, mod[l], g_attn[l], w_perm, tables, tm)

        o_a = _diff_attention(qt, k, vt, diff_lambda[l], diff_subln[l], l, 2 * TILE)
        o_b = _band_attention(
            qt, k, vt, dil_bias,
            pl.BlockSpec(dil_bias.shape, lambda b, p, i: (0, 0, 0)),
            dil_offsets, False, N_HEADS_DIFF // 2, N_HEADS_DIL // 2, "dilated_attention")
        na_bias = _na_bias(na_rpb[l], rows).reshape(3, N_HEADS_NA // 2, 2, 3 * TILE, TILE)
        o_c = _band_attention(
            qt, k, vt, na_bias.reshape(3 * (N_HEADS_NA // 2), 2, 3 * TILE, TILE),
            pl.BlockSpec((1, 2, 3 * TILE, TILE),
                         lambda b, p, i: (jnp.where(i == 0, 0, jnp.where(i == n_qblk - 1, 2, 1)) * (N_HEADS_NA // 2) + p,
                                          0, 0, 0)),
            (-1, 0, 1), True, (N_HEADS_DIFF + N_HEADS_DIL) // 2, N_HEADS_NA // 2, "neighbourhood_attention")

        x = _out_projection(x, mod[l], o_a, o_b, o_c, w_out[l].astype(BF16), tm)

        wg = w_up[l][:, :d_ff].reshape(d, n_ch, FF_CHUNK).transpose(1, 0, 2).astype(BF16)
        wu = w_up[l][:, d_ff:].reshape(d, n_ch, FF_CHUNK).transpose(1, 0, 2).astype(BF16)
        cw = conv_w[l].reshape(CONV_W, n_ch, FF_CHUNK).transpose(1, 0, 2)
        cb = conv_b[l].reshape(n_ch, 1, FF_CHUNK)
        a = _ffn_up(x, mod[l], g_ffn[l], wg, wu, cw, cb, tm)
        x = _ffn_down(x, mod[l], a, w_down[l].astype(BF16), g_final, l == depth - 1, tm)
    return x
```

```python
import functools
import math

import numpy as np
import jax
import jax.numpy as jnp
from jax import lax
from jax.experimental import pallas as pl
from jax.experimental.pallas import tpu as pltpu

F32 = jnp.float32
BF16 = jnp.bfloat16

HEAD_DIM = 64
N_HEADS_DIFF = 4
N_HEADS_DIL = 6
N_HEADS_NA = 6
W_DIFF = N_HEADS_DIFF * HEAD_DIM
W_DIL = N_HEADS_DIL * HEAD_DIM
W_NA = N_HEADS_NA * HEAD_DIM
DIFF_QK_DIM = HEAD_DIM // 2
DIL_PATTERNS = ((128, 1), (512, 4), (2048, 16))
GRID_W = 64
NA_KH = 8
NA_KW = 16
CONV_W = 3
ROPE_THETA = 10000.0
EPS = 1e-6
NEG = -1e30

LANES = 128
PAIR_W = 2 * HEAD_DIM
TILE = 256
FF_CHUNK = 256
HALO = 16
VMEM_LIMIT = 56 * 1024 * 1024
LOG2E = math.log2(math.e)
SUM_ROWS = 16


def _cparams(sem):
    return pltpu.CompilerParams(dimension_semantics=sem, vmem_limit_bytes=VMEM_LIMIT)


def _mod_kernel(ct_ref, w_ref, b_ref, o_ref):
    ct = ct_ref[...]
    s = ct * jax.nn.sigmoid(ct)
    w = w_ref[0]
    rows = [jnp.sum(w * s[:, b:b + 1], axis=0, keepdims=True) for b in range(ct.shape[1])]
    o_ref[0] = jnp.concatenate(rows, axis=0) + b_ref[0]


def _modulation(c, w_ada, b_ada):
    depth, d, n = w_ada.shape
    bsz = c.shape[0]
    tn = 768
    return pl.pallas_call(
        _mod_kernel,
        grid=(depth, n // tn),
        in_specs=[
            pl.BlockSpec((d, bsz), lambda l, j: (0, 0)),
            pl.BlockSpec((1, d, tn), lambda l, j: (l, 0, j)),
            pl.BlockSpec((1, 1, tn), lambda l, j: (l, 0, j)),
        ],
        out_specs=pl.BlockSpec((1, bsz, tn), lambda l, j: (l, 0, j)),
        out_shape=jax.ShapeDtypeStruct((depth, bsz, n), F32),
        compiler_params=_cparams(("arbitrary", "arbitrary")),
        name="adaln_mod",
    )(c.T, w_ada, b_ada.reshape(depth, 1, n))


def _norm_mod(x, g, sc, sh):
    ms = jnp.mean(x * x, axis=-1, keepdims=True)
    return (x * lax.rsqrt(ms + EPS)) * g * (1.0 + sc) + sh


def _rope_slab(acc, cos, sin, group):
    half = group // 2
    lane = lax.broadcasted_iota(jnp.int32, (1, LANES), 1) % group
    first = lane < half
    outs = []
    for c in range(acc.shape[1] // LANES):
        xc = acc[:, c * LANES:(c + 1) * LANES]
        swapped = jnp.where(first, pltpu.roll(xc, LANES - half, axis=1), pltpu.roll(xc, half, axis=1))
        outs.append(xc * cos + swapped * sin)
    return jnp.concatenate(outs, axis=1)


def _proj_kernel(x_ref, mod_ref, g_ref, w_ref, ca_ref, sa_ref, cb_ref, sb_ref,
                 qt_ref, k_ref, vt_ref):
    tm = x_ref.shape[1]
    h = _norm_mod(x_ref[0], g_ref[...], mod_ref[0, 1:2, :], mod_ref[0, 0:1, :]).astype(BF16)
    ca, sa, cb, sb = ca_ref[...], sa_ref[...], cb_ref[...], sb_ref[...]
    width = W_DIFF + W_DIL + W_NA
    slabs = ((0, W_DIFF), (W_DIFF, W_DIL), (W_DIFF + W_DIL, W_NA))

    def slab(part, si):
        off, w = slabs[si]
        acc = jnp.dot(h, w_ref[:, part * width + off: part * width + off + w], preferred_element_type=F32)
        if part < 2 and si == 0:
            acc = _rope_slab(acc, ca, sa, DIFF_QK_DIM)
        elif part < 2 and si == 1:
            acc = _rope_slab(acc, cb, sb, HEAD_DIM)
        return acc

    q_scale = (LOG2E * DIFF_QK_DIM ** -0.5, LOG2E * HEAD_DIM ** -0.5, LOG2E * HEAD_DIM ** -0.5)
    for si, (off, w) in enumerate(slabs):
        q = slab(0, si) * q_scale[si]
        qt_ref[0, off:off + w, :] = q.T.astype(BF16)
        k_ref[0, :, off:off + w] = slab(1, si).astype(BF16)
        v = slab(2, si)
        for t in range(tm // TILE):
            vt_ref[0, t, off:off + w, :] = v[t * TILE:(t + 1) * TILE, :].T.astype(BF16)


def _rope_tables(s, group):
    half = group // 2
    inv = ROPE_THETA ** (-jnp.arange(half, dtype=F32) / half)
    ang = jnp.arange(s, dtype=F32)[:, None] * inv
    cos, sin = jnp.cos(ang), jnp.sin(ang)
    reps = LANES // group
    return (jnp.tile(jnp.concatenate([cos, cos], axis=1), (1, reps)),
            jnp.tile(jnp.concatenate([-sin, sin], axis=1), (1, reps)))


def _projection(x, mod_l, g, w_perm, tables, tm):
    bsz, s, d = x.shape
    width = W_DIFF + W_DIL + W_NA
    tab_spec = pl.BlockSpec((tm, LANES), lambda b, i: (i, 0))
    return pl.pallas_call(
        _proj_kernel,
        grid=(bsz, s // tm),
        in_specs=[
            pl.BlockSpec((1, tm, d), lambda b, i: (b, i, 0)),
            pl.BlockSpec((1, 6, d), lambda b, i: (b, 0, 0)),
            pl.BlockSpec((1, d), lambda b, i: (0, 0)),
            pl.BlockSpec((d, 3 * width), lambda b, i: (0, 0)),
            tab_spec, tab_spec, tab_spec, tab_spec,
        ],
        out_specs=[
            pl.BlockSpec((1, width, tm), lambda b, i: (b, 0, i)),
            pl.BlockSpec((1, tm, width), lambda b, i: (b, i, 0)),
            pl.BlockSpec((1, tm // TILE, width, TILE), lambda b, i: (b, i, 0, 0)),
        ],
        out_shape=[
            jax.ShapeDtypeStruct((bsz, width, s), BF16),
            jax.ShapeDtypeStruct((bsz, s, width), BF16),
            jax.ShapeDtypeStruct((bsz, s // TILE, width, TILE), BF16),
        ],
        compiler_params=_cparams(("arbitrary", "arbitrary")),
        name="norm_proj_rope",
    )(x, mod_l, g.reshape(1, d), w_perm, *tables)


def _masked_queries(qt, n_split):
    row = lax.broadcasted_iota(jnp.int32, (PAIR_W, 1), 0)
    step = PAIR_W // n_split
    return [jnp.where((row >= j * step) & (row < (j + 1) * step), qt, jnp.zeros_like(qt))
            for j in range(n_split)]


def _v_with_ones(vtb, hd):
    ones = jnp.ones((SUM_ROWS, vtb.shape[1]), BF16)
    return jnp.concatenate([vtb[hd * HEAD_DIM:(hd + 1) * HEAD_DIM, :], ones], axis=0)


def _diff_kernel(lam_init, dl_ref, g_ref, qt_ref, k_ref, vt_ref, o_ref, q_s, s_s, mb_s, m_s, acc_s):
    tq = qt_ref.shape[2]
    n_kt = vt_ref.shape[1]
    for j, qm in enumerate(_masked_queries(qt_ref[0], 4)):
        q_s[:, j * tq:(j + 1) * tq] = qm
    m_s[...] = jnp.full(m_s.shape, NEG, F32)
    acc_s[...] = jnp.zeros(acc_s.shape, F32)

    def scores(kt, slot):
        kb = k_ref[0, pl.ds(pl.multiple_of(kt * TILE, TILE), TILE), :]
        s = jnp.dot(kb, q_s[...], preferred_element_type=F32)
        s_s[slot] = s
        mb_s[slot] = jnp.max(s, axis=0, keepdims=True)

    def consume(kt, slot):
        m_old = m_s[...]
        m_new = jnp.maximum(m_old, mb_s[slot])
        alpha = jnp.exp2(m_old - m_new)
        m_s[...] = m_new
        e = jnp.exp2(s_s[slot] - m_new).astype(BF16)
        vtb = vt_ref[0, kt]
        for hd in range(2):
            lanes = slice(2 * hd * tq, (2 * hd + 2) * tq)
            pv = jnp.dot(_v_with_ones(vtb, hd), e[:, lanes], preferred_element_type=F32)
            acc_s[hd] = acc_s[hd] * alpha[:, lanes] + pv

    scores(0, 0)

    def body(tt, carry):
        kt = 2 * tt
        scores(kt + 1, 1)
        consume(kt, 0)
        scores(jnp.minimum(kt + 2, n_kt - 1), 0)
        consume(kt + 1, 1)
        return carry

    lax.fori_loop(0, n_kt // 2, body, 0)

    dl = dl_ref[...]
    lam = (jnp.exp(jnp.sum(dl[0:1] * dl[1:2], axis=1, keepdims=True))
           - jnp.exp(jnp.sum(dl[2:3] * dl[3:4], axis=1, keepdims=True)) + lam_init)
    g = g_ref[...]
    outs = []
    for hd in range(2):
        acc = acc_s[hd]
        p = [acc[0:HEAD_DIM, mp * tq:(mp + 1) * tq] / acc[HEAD_DIM:HEAD_DIM + 1, mp * tq:(mp + 1) * tq]
             for mp in range(2)]
        o = p[0] - lam * p[1]
        ms = jnp.mean(o * o, axis=0, keepdims=True)
        outs.append((o * lax.rsqrt(ms + EPS)) * g * (1.0 - lam_init))
    o_ref[0] = jnp.concatenate(outs, axis=0).T.astype(BF16)


def _diff_attention(qt, k, vt, diff_lambda, subln, layer_idx, tq):
    bsz, _, s = qt.shape
    n_pairs = N_HEADS_DIFF // 2
    lam_init = 0.8 - 0.6 * math.exp(-0.3 * layer_idx)
    return pl.pallas_call(
        functools.partial(_diff_kernel, lam_init),
        grid=(bsz, n_pairs, s // tq),
        in_specs=[
            pl.BlockSpec(diff_lambda.shape, lambda b, p, i: (0, 0)),
            pl.BlockSpec((HEAD_DIM, 1), lambda b, p, i: (0, 0)),
            pl.BlockSpec((1, PAIR_W, tq), lambda b, p, i: (b, p, i)),
            pl.BlockSpec((1, s, PAIR_W), lambda b, p, i: (b, 0, p)),
            pl.BlockSpec((1, s // TILE, PAIR_W, TILE), lambda b, p, i: (b, 0, p, 0)),
        ],
        out_specs=pl.BlockSpec((1, tq, PAIR_W), lambda b, p, i: (b, i, p)),
        out_shape=jax.ShapeDtypeStruct((bsz, s, W_DIFF), BF16),
        scratch_shapes=[
            pltpu.VMEM((PAIR_W, 4 * tq), BF16),
            pltpu.VMEM((2, TILE, 4 * tq), F32),
            pltpu.VMEM((2, 1, 4 * tq), F32),
            pltpu.VMEM((1, 4 * tq), F32),
            pltpu.VMEM((2, HEAD_DIM + SUM_ROWS, 2 * tq), F32),
        ],
        compiler_params=_cparams(("arbitrary", "arbitrary", "arbitrary")),
        name="diff_attention",
    )(diff_lambda, subln.reshape(HEAD_DIM, 1), qt, k, vt)


def _band_kernel(offsets, window_bias, qt_ref, k_ref, vt_ref, bias_ref, o_ref, q_s, s_s):
    i = pl.program_id(2)
    n_blk = vt_ref.shape[1]
    for hd, qm in enumerate(_masked_queries(qt_ref[0], 2)):
        q_s[:, hd * TILE:(hd + 1) * TILE] = qm

    tiles, col_max = [], None
    for oi, off in enumerate(offsets):
        kbi = i + off
        kt = jnp.clip(kbi, 0, n_blk - 1)
        tiles.append(kt)
        if window_bias:
            bias = bias_ref[0, 0, oi * TILE:(oi + 1) * TILE, :]
        else:
            bias = bias_ref[jnp.where((kbi >= 0) & (kbi < n_blk), oi, len(offsets))]
        kb = k_ref[0, pl.ds(pl.multiple_of(kt * TILE, TILE), TILE), :]
        s = jnp.dot(kb, q_s[...], preferred_element_type=F32) + bias
        s_s[oi] = s
        cm = jnp.max(s, axis=0, keepdims=True)
        col_max = cm if col_max is None else jnp.maximum(col_max, cm)

    acc = [None, None]
    for oi in range(len(offsets)):
        e = jnp.exp2(s_s[oi] - col_max).astype(BF16)
        vtb = vt_ref[0, tiles[oi]]
        for hd in range(2):
            pv = jnp.dot(_v_with_ones(vtb, hd), e[:, hd * TILE:(hd + 1) * TILE], preferred_element_type=F32)
            acc[hd] = pv if acc[hd] is None else acc[hd] + pv
    outs = [a[0:HEAD_DIM, :] / a[HEAD_DIM:HEAD_DIM + 1, :] for a in acc]
    o_ref[0] = jnp.concatenate(outs, axis=0).T.astype(BF16)


def _band_attention(qt, k, vt, bias, bias_spec, offsets, window_bias, pair0, n_pairs, name):
    bsz, _, s = qt.shape
    return pl.pallas_call(
        functools.partial(_band_kernel, offsets, window_bias),
        grid=(bsz, n_pairs, s // TILE),
        in_specs=[
            pl.BlockSpec((1, PAIR_W, TILE), lambda b, p, i: (b, pair0 + p, i)),
            pl.BlockSpec((1, s, PAIR_W), lambda b, p, i: (b, 0, pair0 + p)),
            pl.BlockSpec((1, s // TILE, PAIR_W, TILE), lambda b, p, i: (b, 0, pair0 + p, 0)),
            bias_spec,
        ],
        out_specs=pl.BlockSpec((1, TILE, PAIR_W), lambda b, p, i: (b, i, p)),
        out_shape=jax.ShapeDtypeStruct((bsz, s, n_pairs * PAIR_W), BF16),
        scratch_shapes=[
            pltpu.VMEM((PAIR_W, 2 * TILE), BF16),
            pltpu.VMEM((len(offsets), TILE, 2 * TILE), F32),
        ],
        compiler_params=_cparams(("arbitrary", "arbitrary", "arbitrary")),
        name=name,
    )(qt, k, vt, bias)


def _dilated_bias():
    reach = max(w // 2 for w, _ in DIL_PATTERNS)
    n_off = -(-reach // TILE)
    offsets = tuple(range(-n_off, n_off + 1))
    kj = np.arange(TILE)[:, None]
    qi = np.arange(TILE)[None, :]
    tabs = []
    for off in offsets:
        delta = off * TILE + kj - qi
        cnt = np.zeros((TILE, TILE), np.int32)
        for window, dil in DIL_PATTERNS:
            cnt += ((delta % dil == 0) & (np.abs(delta) <= window // 2)).astype(np.int32)
        tabs.append(np.where(cnt > 0, np.log2(np.maximum(cnt, 1).astype(np.float64)), NEG))
    tabs.append(np.full((TILE, TILE), NEG))
    return offsets, jnp.asarray(np.tile(np.stack(tabs), (1, 1, 2)), F32)


def _na_bias_kernel(rows, rpb_ref, o_ref):
    hd = pl.program_id(0)
    kh = min(NA_KH, rows)
    q_rows = TILE // GRID_W
    kc = lax.broadcasted_iota(jnp.int32, (GRID_W, LANES), 0)
    lane = lax.broadcasted_iota(jnp.int32, (GRID_W, LANES), 1)
    qc = lane % GRID_W
    dc = jnp.clip(kc - qc, -(NA_KW - 1), NA_KW - 1) + (NA_KW - 1)
    cs = jnp.clip(qc - NA_KW // 2, 0, GRID_W - NA_KW)
    col_ok = (kc >= cs) & (kc < cs + NA_KW)
    neg = jnp.full((GRID_W, LANES), NEG, F32)
    tiles = []
    for dr in range(2 * NA_KH - 1):
        t = neg
        for j in range(2 * NA_KW - 1):
            t = jnp.where(dc == j, rpb_ref[hd, dr, j] * LOG2E, t)
        tiles.append(jnp.where(col_ok, t, neg))
    for v, r0 in enumerate((0, 2 * q_rows, rows - q_rows)):
        for kr_rel in range(3 * q_rows):
            kr = r0 - q_rows + kr_rel
            for pair in range(q_rows // 2):
                halves = []
                for r in (r0 + 2 * pair, r0 + 2 * pair + 1):
                    rs = min(max(r - kh // 2, 0), rows - kh)
                    halves.append(tiles[kr - r + NA_KH - 1] if rs <= kr < rs + kh else neg)
                blk = jnp.where(lane < GRID_W, halves[0], halves[1])
                o_ref[v, 0, kr_rel * GRID_W:(kr_rel + 1) * GRID_W, pair * LANES:(pair + 1) * LANES] = blk


def _na_bias(rpb, rows):
    n_heads = rpb.shape[0]
    return pl.pallas_call(
        functools.partial(_na_bias_kernel, rows),
        grid=(n_heads,),
        in_specs=[pl.BlockSpec(memory_space=pltpu.SMEM)],
        out_specs=pl.BlockSpec((3, 1, 3 * TILE, TILE), lambda h: (0, h // 2, 0, h % 2)),
        out_shape=jax.ShapeDtypeStruct((3, n_heads // 2, 3 * TILE, 2 * TILE), F32),
        compiler_params=_cparams(("arbitrary",)),
        name="na_bias_table",
    )(rpb)


def _outproj_kernel(x_ref, mod_ref, oa_ref, ob_ref, oc_ref, w_ref, y_ref):
    mix = jnp.dot(oa_ref[0], w_ref[0:W_DIFF, :], preferred_element_type=F32)
    mix += jnp.dot(ob_ref[0], w_ref[W_DIFF:W_DIFF + W_DIL, :], preferred_element_type=F32)
    mix += jnp.dot(oc_ref[0], w_ref[W_DIFF + W_DIL:, :], preferred_element_type=F32)
    y_ref[0] = x_ref[0] + mod_ref[0, 2:3, :] * mix


def _out_projection(x, mod_l, o_a, o_b, o_c, w_out, tm):
    bsz, s, d = x.shape
    row = lambda w: pl.BlockSpec((1, tm, w), lambda b, i: (b, i, 0))
    return pl.pallas_call(
        _outproj_kernel,
        grid=(bsz, s // tm),
        in_specs=[
            row(d),
            pl.BlockSpec((1, 6, d), lambda b, i: (b, 0, 0)),
            row(W_DIFF), row(W_DIL), row(W_NA),
            pl.BlockSpec(w_out.shape, lambda b, i: (0, 0)),
        ],
        out_specs=row(d),
        out_shape=jax.ShapeDtypeStruct((bsz, s, d), F32),
        compiler_params=_cparams(("arbitrary", "arbitrary")),
        name="out_proj_residual",
    )(x, mod_l, o_a, o_b, o_c, w_out)


def _ffn_up_kernel(x_ref, xp_ref, xn_ref, mod_ref, g_ref, wg_ref, wu_ref, cw_ref, cb_ref,
                   a_ref, h_s, g_s):
    i = pl.program_id(1)
    tm = x_ref.shape[1]
    g, sc, sh = g_ref[...], mod_ref[0, 4:5, :], mod_ref[0, 3:4, :]
    keep_prev = (i > 0).astype(F32)
    keep_next = (i < pl.num_programs(1) - 1).astype(F32)
    h_s[0:HALO, :] = (_norm_mod(xp_ref[0], g, sc, sh) * keep_prev).astype(BF16)
    h_s[HALO:HALO + tm, :] = _norm_mod(x_ref[0], g, sc, sh).astype(BF16)
    h_s[HALO + tm:, :] = (_norm_mod(xn_ref[0], g, sc, sh) * keep_next).astype(BF16)
    for c in range(wg_ref.shape[0]):
        g_s[...] = jnp.dot(h_s[...], wg_ref[c], preferred_element_type=F32)
        cw = cw_ref[c]
        gc = cb_ref[c]
        for t in range(CONV_W):
            gc = gc + g_s[pl.ds(HALO - CONV_W // 2 + t, tm), :] * cw[t:t + 1, :]
        u = jnp.dot(h_s[HALO:HALO + tm, :], wu_ref[c], preferred_element_type=F32)
        a_ref[0, :, c * FF_CHUNK:(c + 1) * FF_CHUNK] = (gc * jax.nn.sigmoid(gc) * u).astype(BF16)


def _ffn_up(x, mod_l, g, wg, wu, cw, cb, tm):
    bsz, s, d = x.shape
    n_ch = wg.shape[0]
    hb = tm // HALO
    n_hb = s // HALO
    return pl.pallas_call(
        _ffn_up_kernel,
        grid=(bsz, s // tm),
        in_specs=[
            pl.BlockSpec((1, tm, d), lambda b, i: (b, i, 0)),
            pl.BlockSpec((1, HALO, d), lambda b, i: (b, jnp.maximum(i * hb - 1, 0), 0)),
            pl.BlockSpec((1, HALO, d), lambda b, i: (b, jnp.minimum((i + 1) * hb, n_hb - 1), 0)),
            pl.BlockSpec((1, 6, d), lambda b, i: (b, 0, 0)),
            pl.BlockSpec((1, d), lambda b, i: (0, 0)),
            pl.BlockSpec(wg.shape, lambda b, i: (0, 0, 0)),
            pl.BlockSpec(wu.shape, lambda b, i: (0, 0, 0)),
            pl.BlockSpec(cw.shape, lambda b, i: (0, 0, 0)),
            pl.BlockSpec(cb.shape, lambda b, i: (0, 0, 0)),
        ],
        out_specs=pl.BlockSpec((1, tm, n_ch * FF_CHUNK), lambda b, i: (b, i, 0)),
        out_shape=jax.ShapeDtypeStruct((bsz, s, n_ch * FF_CHUNK), BF16),
        scratch_shapes=[
            pltpu.VMEM((tm + 2 * HALO, d), BF16),
            pltpu.VMEM((tm + 2 * HALO, FF_CHUNK), F32),
        ],
        compiler_params=_cparams(("arbitrary", "arbitrary")),
        name="ffn_up_conv_glu",
    )(x, x, x, mod_l, g.reshape(1, d), wg, wu, cw, cb)


def _ffn_down_kernel(final, x_ref, mod_ref, a_ref, w_ref, gf_ref, y_ref):
    y = x_ref[0] + mod_ref[0, 5:6, :] * jnp.dot(a_ref[0], w_ref[...], preferred_element_type=F32)
    if final:
        ms = jnp.mean(y * y, axis=-1, keepdims=True)
        y = (y * lax.rsqrt(ms + EPS)) * gf_ref[...]
    y_ref[0] = y


def _ffn_down(x, mod_l, a, w_down, g_final, final, tm):
    bsz, s, d = x.shape
    return pl.pallas_call(
        functools.partial(_ffn_down_kernel, final),
        grid=(bsz, s // tm),
        in_specs=[
            pl.BlockSpec((1, tm, d), lambda b, i: (b, i, 0)),
            pl.BlockSpec((1, 6, d), lambda b, i: (b, 0, 0)),
            pl.BlockSpec((1, tm, a.shape[2]), lambda b, i: (b, i, 0)),
            pl.BlockSpec(w_down.shape, lambda b, i: (0, 0)),
            pl.BlockSpec((1, d), lambda b, i: (0, 0)),
        ],
        out_specs=pl.BlockSpec((1, tm, d), lambda b, i: (b, i, 0)),
        out_shape=jax.ShapeDtypeStruct((bsz, s, d), F32),
        compiler_params=_cparams(("arbitrary", "arbitrary")),
        name="ffn_down_residual",
    )(x, mod_l, a, w_down, g_final.reshape(1, d))


def kernel(x, c, w_ada, b_ada, g_attn, w_in, diff_lambda, diff_subln, na_rpb, w_out, g_ffn, w_up,
           conv_w, conv_b, w_down, g_final):
    bsz, s, d = x.shape
    depth = w_ada.shape[0]
    d_ff = w_down.shape[1]
    n_ch = d_ff // FF_CHUNK
    tm = 512
    assert s % tm == 0 and tm % TILE == 0 and d_ff % FF_CHUNK == 0 and TILE % (2 * GRID_W) == 0
    assert (s // TILE) % 2 == 0 and s // GRID_W >= 4 * (TILE // GRID_W)

    mod = _modulation(c, w_ada, b_ada).reshape(depth, bsz, 6, d)
    tables = _rope_tables(s, DIFF_QK_DIM) + _rope_tables(s, HEAD_DIM)
    dil_offsets, dil_bias = _dilated_bias()
    rows = s // GRID_W
    n_qblk = s // TILE

    wa, wb, wc = W_DIFF, W_DIL, W_NA
    offs = np.cumsum([0, wa, wa, wa, wb, wb, wb, wc, wc, wc])
    order = (0, 3, 6, 1, 4, 7, 2, 5, 8)

    for l in range(depth):
        w_perm = jnp.concatenate([w_in[l][:, offs[j]:offs[j + 1]] for j in order], axis=1).astype(BF16)
        qt, k, vt = _projection(x, mod[l], g_attn[l], w_perm, tables, tm)

        o_a = _diff_attention(qt, k, vt, diff_lambda[l], diff_subln[l], l, 2 * TILE)
        o_b = _band_attention(
            qt, k, vt, dil_bias,
            pl.BlockSpec(dil_bias.shape, lambda b, p, i: (0, 0, 0)),
            dil_offsets, False, N_HEADS_DIFF // 2, N_HEADS_DIL // 2, "dilated_attention")
        o_c = _band_attention(
            qt, k, vt, _na_bias(na_rpb[l], rows),
            pl.BlockSpec((1, 1, 3 * TILE, 2 * TILE),
                         lambda b, p, i: (jnp.where(i == 0, 0, jnp.where(i == n_qblk - 1, 2, 1)), p, 0, 0)),
            (-1, 0, 1), True, (N_HEADS_DIFF + N_HEADS_DIL) // 2, N_HEADS_NA // 2, "neighbourhood_attention")

        x = _out_projection(x, mod[l], o_a, o_b, o_c, w_out[l].astype(BF16), tm)

        wg = w_up[l][:, :d_ff].reshape(d, n_ch, FF_CHUNK).transpose(1, 0, 2).astype(BF16)
        wu = w_up[l][:, d_ff:].reshape(d, n_ch, FF_CHUNK).transpose(1, 0, 2).astype(BF16)
        cw = conv_w[l].reshape(CONV_W, n_ch, FF_CHUNK).transpose(1, 0, 2)
        cb = conv_b[l].reshape(n_ch, 1, FF_CHUNK)
        a = _ffn_up(x, mod[l], g_ffn[l], wg, wu, cw, cb, tm)
        x = _ffn_down(x, mod[l], a, w_down[l].astype(BF16), g_final, l == depth - 1, tm)
    return x
```

```python
import functools
import math

import numpy as np
import jax
import jax.numpy as jnp
from jax import lax
from jax.experimental import pallas as pl
from jax.experimental.pallas import tpu as pltpu

F32 = jnp.float32
BF16 = jnp.bfloat16

HEAD_DIM = 64
N_HEADS_DIFF = 4
N_HEADS_DIL = 6
N_HEADS_NA = 6
W_DIFF = N_HEADS_DIFF * HEAD_DIM
W_DIL = N_HEADS_DIL * HEAD_DIM
W_NA = N_HEADS_NA * HEAD_DIM
DIFF_QK_DIM = HEAD_DIM // 2
DIL_PATTERNS = ((128, 1), (512, 4), (2048, 16))
GRID_W = 64
NA_KH = 8
NA_KW = 16
CONV_W = 3
ROPE_THETA = 10000.0
EPS = 1e-6
NEG = -1e30

LANES = 128
PAIR_W = 2 * HEAD_DIM
TILE = 256
FF_CHUNK = 256
HALO = 16
VMEM_LIMIT = 56 * 1024 * 1024
LOG2E = math.log2(math.e)
SUM_ROWS = 16
LAG_MARGIN = 64.0
LAG_UNROLL = 7
QK_AHEAD = 4


def _cparams(sem):
    return pltpu.CompilerParams(dimension_semantics=sem, vmem_limit_bytes=VMEM_LIMIT)


def _mod_kernel(ct_ref, w_ref, b_ref, o_ref):
    ct = ct_ref[...]
    s = ct * jax.nn.sigmoid(ct)
    w = w_ref[0]
    rows = [jnp.sum(w * s[:, b:b + 1], axis=0, keepdims=True) for b in range(ct.shape[1])]
    o_ref[0] = jnp.concatenate(rows, axis=0) + b_ref[0]


def _modulation(c, w_ada, b_ada):
    depth, d, n = w_ada.shape
    bsz = c.shape[0]
    tn = 768
    return pl.pallas_call(
        _mod_kernel,
        grid=(depth, n // tn),
        in_specs=[
            pl.BlockSpec((d, bsz), lambda l, j: (0, 0)),
            pl.BlockSpec((1, d, tn), lambda l, j: (l, 0, j)),
            pl.BlockSpec((1, 1, tn), lambda l, j: (l, 0, j)),
        ],
        out_specs=pl.BlockSpec((1, bsz, tn), lambda l, j: (l, 0, j)),
        out_shape=jax.ShapeDtypeStruct((depth, bsz, n), F32),
        compiler_params=_cparams(("arbitrary", "arbitrary")),
        name="adaln_mod",
    )(c.T, w_ada, b_ada.reshape(depth, 1, n))


def _norm_mod(x, g, sc, sh):
    ms = jnp.mean(x * x, axis=-1, keepdims=True)
    return (x * lax.rsqrt(ms + EPS)) * g * (1.0 + sc) + sh


def _rope_slab(acc, cos, sin, group):
    half = group // 2
    lane = lax.broadcasted_iota(jnp.int32, (1, LANES), 1) % group
    first = lane < half
    outs = []
    for c in range(acc.shape[1] // LANES):
        xc = acc[:, c * LANES:(c + 1) * LANES]
        swapped = jnp.where(first, pltpu.roll(xc, LANES - half, axis=1), pltpu.roll(xc, half, axis=1))
        outs.append(xc * cos + swapped * sin)
    return jnp.concatenate(outs, axis=1)


def _proj_kernel(x_ref, mod_ref, g_ref, w_ref, ca_ref, sa_ref, cb_ref, sb_ref,
                 qt_ref, k_ref, vt_ref, kmax_ref):
    tm = x_ref.shape[1]
    h = _norm_mod(x_ref[0], g_ref[...], mod_ref[0, 1:2, :], mod_ref[0, 0:1, :]).astype(BF16)
    ca, sa, cb, sb = ca_ref[...], sa_ref[...], cb_ref[...], sb_ref[...]
    width = W_DIFF + W_DIL + W_NA
    slabs = ((0, W_DIFF), (W_DIFF, W_DIL), (W_DIFF + W_DIL, W_NA))

    def slab(part, si):
        off, w = slabs[si]
        acc = jnp.dot(h, w_ref[:, part * width + off: part * width + off + w], preferred_element_type=F32)
        if part < 2 and si == 0:
            acc = _rope_slab(acc, ca, sa, DIFF_QK_DIM)
        elif part < 2 and si == 1:
            acc = _rope_slab(acc, cb, sb, HEAD_DIM)
        return acc

    q_scale = (LOG2E * DIFF_QK_DIM ** -0.5, LOG2E * HEAD_DIM ** -0.5, LOG2E * HEAD_DIM ** -0.5)
    for si, (off, w) in enumerate(slabs):
        q = slab(0, si) * q_scale[si]
        qt_ref[0, off:off + w, :] = q.T.astype(BF16)
        kb = slab(1, si).astype(BF16)
        k_ref[0, :, off:off + w] = kb
        if si == 0:
            kmax_ref[0, 0] = jnp.max(jnp.abs(kb.astype(F32)), axis=0, keepdims=True)
        v = slab(2, si)
        for t in range(tm // TILE):
            vt_ref[0, t, off:off + w, :] = v[t * TILE:(t + 1) * TILE, :].T.astype(BF16)


def _rope_tables(s, group):
    half = group // 2
    inv = ROPE_THETA ** (-jnp.arange(half, dtype=F32) / half)
    ang = jnp.arange(s, dtype=F32)[:, None] * inv
    cos, sin = jnp.cos(ang), jnp.sin(ang)
    reps = LANES // group
    return (jnp.tile(jnp.concatenate([cos, cos], axis=1), (1, reps)),
            jnp.tile(jnp.concatenate([-sin, sin], axis=1), (1, reps)))


def _projection(x, mod_l, g, w_perm, tables, tm):
    bsz, s, d = x.shape
    width = W_DIFF + W_DIL + W_NA
    tab_spec = pl.BlockSpec((tm, LANES), lambda b, i: (i, 0))
    return pl.pallas_call(
        _proj_kernel,
        grid=(bsz, s // tm),
        in_specs=[
            pl.BlockSpec((1, tm, d), lambda b, i: (b, i, 0)),
            pl.BlockSpec((1, 6, d), lambda b, i: (b, 0, 0)),
            pl.BlockSpec((1, d), lambda b, i: (0, 0)),
            pl.BlockSpec((d, 3 * width), lambda b, i: (0, 0)),
            tab_spec, tab_spec, tab_spec, tab_spec,
        ],
        out_specs=[
            pl.BlockSpec((1, width, tm), lambda b, i: (b, 0, i)),
            pl.BlockSpec((1, tm, width), lambda b, i: (b, i, 0)),
            pl.BlockSpec((1, tm // TILE, width, TILE), lambda b, i: (b, i, 0, 0)),
            pl.BlockSpec((1, 1, 1, W_DIFF), lambda b, i: (b, i, 0, 0)),
        ],
        out_shape=[
            jax.ShapeDtypeStruct((bsz, width, s), BF16),
            jax.ShapeDtypeStruct((bsz, s, width), BF16),
            jax.ShapeDtypeStruct((bsz, s // TILE, width, TILE), BF16),
            jax.ShapeDtypeStruct((bsz, s // tm, 1, W_DIFF), F32),
        ],
        compiler_params=_cparams(("arbitrary", "arbitrary")),
        name="norm_proj_rope",
    )(x, mod_l, g.reshape(1, d), w_perm, *tables)


def _masked_queries(qt, n_split):
    row = lax.broadcasted_iota(jnp.int32, (PAIR_W, 1), 0)
    step = PAIR_W // n_split
    return [jnp.where((row >= j * step) & (row < (j + 1) * step), qt, jnp.zeros_like(qt))
            for j in range(n_split)]


def _v_with_ones(vtb, hd):
    ones = jnp.ones((SUM_ROWS, vtb.shape[1]), BF16)
    return jnp.concatenate([vtb[hd * HEAD_DIM:(hd + 1) * HEAD_DIM, :], ones], axis=0)


def _diff_kernel(lam_init, dl_ref, g_ref, kmax_ref, qt_ref, k_ref, vt_ref, o_ref, q_s, s_s, m_s, acc_s):
    tq = qt_ref.shape[2]
    n_kt = vt_ref.shape[1]
    for j, qm in enumerate(_masked_queries(qt_ref[0], 4)):
        q_s[:, j * tq:(j + 1) * tq] = qm
    lane_tiles = [slice(nt * TILE, (nt + 1) * TILE) for nt in range(4 * tq // TILE)]

    def k_tile(kt):
        return k_ref[0, pl.ds(pl.multiple_of(kt * TILE, TILE), TILE), :]

    def v_tiles(kt):
        vtb = vt_ref[0, kt]
        return [_v_with_ones(vtb, hd) for hd in range(2)]

    def exact_step(kt, first):
        kb, v_ext = k_tile(kt), v_tiles(kt)
        for ln in lane_tiles:
            s_s[:, ln] = jnp.dot(kb, q_s[:, ln], preferred_element_type=F32)
        for ln in lane_tiles:
            tile_max = jnp.max(s_s[:, ln], axis=0, keepdims=True)
            m_new = tile_max if first else jnp.maximum(m_s[:, ln], tile_max)
            e = jnp.exp2(s_s[:, ln] - m_new).astype(BF16)
            pv = jnp.dot(v_ext[ln.start // (2 * tq)], e, preferred_element_type=F32)
            acc_s[:, ln] = pv if first else acc_s[:, ln] * jnp.exp2(m_s[:, ln] - m_new) + pv
            m_s[:, ln] = m_new

    def lagging_steps(kts):
        def finish(v_ext, ln, s):
            m_used = m_s[:, ln]
            e = jnp.exp2(s - m_used).astype(BF16)
            m_new = jnp.maximum(m_used, jnp.max(s, axis=0, keepdims=True))
            pv = jnp.dot(v_ext[ln.start // (2 * tq)], e, preferred_element_type=F32)
            acc_s[:, ln] = (acc_s[:, ln] + pv) * jnp.exp2(m_used - m_new)
            m_s[:, ln] = m_new

        pending = []
        for kt in kts:
            kb, v_ext = k_tile(kt), v_tiles(kt)
            for ln in lane_tiles:
                pending.append((v_ext, ln, jnp.dot(kb, q_s[:, ln], preferred_element_type=F32)))
                if len(pending) > QK_AHEAD:
                    finish(*pending.pop(0))
        for item in pending:
            finish(*item)

    exact_step(0, True)

    kmax = jnp.max(kmax_ref[0], axis=0) * (1.0 + 2.0 ** -7)
    kmax = jnp.broadcast_to(kmax, (SUM_ROWS, PAIR_W)).astype(BF16)
    bound = jnp.dot(kmax, jnp.abs(q_s[...]), preferred_element_type=F32)[0:1] * (1.0 + 2.0 ** -7)
    lag_ok = jnp.max(bound - m_s[...]) < LAG_MARGIN

    @pl.when(lag_ok)
    def _():
        def body(tt, carry):
            lagging_steps([LAG_UNROLL * tt + 1 + u for u in range(LAG_UNROLL)])
            return carry
        n_trips = (n_kt - 1) // LAG_UNROLL
        lax.fori_loop(0, n_trips, body, 0)
        if n_trips * LAG_UNROLL + 1 < n_kt:
            lagging_steps(list(range(n_trips * LAG_UNROLL + 1, n_kt)))

    @pl.when(jnp.logical_not(lag_ok))
    def _():
        def body(kt, carry):
            exact_step(kt, False)
            return carry
        lax.fori_loop(1, n_kt, body, 0)

    dl = dl_ref[...]
    lam = (jnp.exp(jnp.sum(dl[0:1] * dl[1:2], axis=1, keepdims=True))
           - jnp.exp(jnp.sum(dl[2:3] * dl[3:4], axis=1, keepdims=True)) + lam_init)
    g = g_ref[...]
    outs = []
    for hd in range(2):
        p = [acc_s[0:HEAD_DIM, (2 * hd + mp) * tq:(2 * hd + mp + 1) * tq]
             / acc_s[HEAD_DIM:HEAD_DIM + 1, (2 * hd + mp) * tq:(2 * hd + mp + 1) * tq] for mp in range(2)]
        o = p[0] - lam * p[1]
        ms = jnp.mean(o * o, axis=0, keepdims=True)
        outs.append((o * lax.rsqrt(ms + EPS)) * g * (1.0 - lam_init))
    o_ref[0] = jnp.concatenate(outs, axis=0).T.astype(BF16)


def _diff_attention(qt, k, vt, kmax, diff_lambda, subln, layer_idx, tq):
    bsz, _, s = qt.shape
    n_pairs = N_HEADS_DIFF // 2
    lam_init = 0.8 - 0.6 * math.exp(-0.3 * layer_idx)
    return pl.pallas_call(
        functools.partial(_diff_kernel, lam_init),
        grid=(bsz, n_pairs, s // tq),
        in_specs=[
            pl.BlockSpec(diff_lambda.shape, lambda b, p, i: (0, 0)),
            pl.BlockSpec((HEAD_DIM, 1), lambda b, p, i: (0, 0)),
            pl.BlockSpec((1, kmax.shape[1], 1, PAIR_W), lambda b, p, i: (b, 0, 0, p)),
            pl.BlockSpec((1, PAIR_W, tq), lambda b, p, i: (b, p, i)),
            pl.BlockSpec((1, s, PAIR_W), lambda b, p, i: (b, 0, p)),
            pl.BlockSpec((1, s // TILE, PAIR_W, TILE), lambda b, p, i: (b, 0, p, 0)),
        ],
        out_specs=pl.BlockSpec((1, tq, PAIR_W), lambda b, p, i: (b, i, p)),
        out_shape=jax.ShapeDtypeStruct((bsz, s, W_DIFF), BF16),
        scratch_shapes=[
            pltpu.VMEM((PAIR_W, 4 * tq), BF16),
            pltpu.VMEM((TILE, 4 * tq), F32),
            pltpu.VMEM((1, 4 * tq), F32),
            pltpu.VMEM((HEAD_DIM + SUM_ROWS, 4 * tq), F32),
        ],
        compiler_params=_cparams(("arbitrary", "arbitrary", "arbitrary")),
        name="diff_attention",
    )(diff_lambda, subln.reshape(HEAD_DIM, 1), kmax, qt, k, vt)


def _band_kernel(offsets, window_bias, qt_ref, k_ref, vt_ref, bias_ref, o_ref, q_s, s_s):
    i = pl.program_id(2)
    n_blk = vt_ref.shape[1]
    for hd, qm in enumerate(_masked_queries(qt_ref[0], 2)):
        q_s[:, hd * TILE:(hd + 1) * TILE] = qm

    tiles, col_max = [], None
    for oi, off in enumerate(offsets):
        kbi = i + off
        kt = jnp.clip(kbi, 0, n_blk - 1)
        tiles.append(kt)
        if window_bias:
            bias = bias_ref[0, 0, oi * TILE:(oi + 1) * TILE, :]
        else:
            bias = bias_ref[jnp.where((kbi >= 0) & (kbi < n_blk), oi, len(offsets))]
        kb = k_ref[0, pl.ds(pl.multiple_of(kt * TILE, TILE), TILE), :]
        s = jnp.dot(kb, q_s[...], preferred_element_type=F32) + bias
        s_s[oi] = s
        cm = jnp.max(s, axis=0, keepdims=True)
        col_max = cm if col_max is None else jnp.maximum(col_max, cm)

    acc = [None, None]
    for oi in range(len(offsets)):
        e = jnp.exp2(s_s[oi] - col_max).astype(BF16)
        vtb = vt_ref[0, tiles[oi]]
        for hd in range(2):
            pv = jnp.dot(_v_with_ones(vtb, hd), e[:, hd * TILE:(hd + 1) * TILE], preferred_element_type=F32)
            acc[hd] = pv if acc[hd] is None else acc[hd] + pv
    outs = [a[0:HEAD_DIM, :] / a[HEAD_DIM:HEAD_DIM + 1, :] for a in acc]
    o_ref[0] = jnp.concatenate(outs, axis=0).T.astype(BF16)


def _band_attention(qt, k, vt, bias, bias_spec, offsets, window_bias, pair0, n_pairs, name):
    bsz, _, s = qt.shape
    return pl.pallas_call(
        functools.partial(_band_kernel, offsets, window_bias),
        grid=(bsz, n_pairs, s // TILE),
        in_specs=[
            pl.BlockSpec((1, PAIR_W, TILE), lambda b, p, i: (b, pair0 + p, i)),
            pl.BlockSpec((1, s, PAIR_W), lambda b, p, i: (b, 0, pair0 + p)),
            pl.BlockSpec((1, s // TILE, PAIR_W, TILE), lambda b, p, i: (b, 0, pair0 + p, 0)),
            bias_spec,
        ],
        out_specs=pl.BlockSpec((1, TILE, PAIR_W), lambda b, p, i: (b, i, p)),
        out_shape=jax.ShapeDtypeStruct((bsz, s, n_pairs * PAIR_W), BF16),
        scratch_shapes=[
            pltpu.VMEM((PAIR_W, 2 * TILE), BF16),
            pltpu.VMEM((len(offsets), TILE, 2 * TILE), F32),
        ],
        compiler_params=_cparams(("arbitrary", "arbitrary", "arbitrary")),
        name=name,
    )(qt, k, vt, bias)


def _dilated_bias():
    reach = max(w // 2 for w, _ in DIL_PATTERNS)
    n_off = -(-reach // TILE)
    offsets = tuple(range(-n_off, n_off + 1))
    kj = np.arange(TILE)[:, None]
    qi = np.arange(TILE)[None, :]
    tabs = []
    for off in offsets:
        delta = off * TILE + kj - qi
        cnt = np.zeros((TILE, TILE), np.int32)
        for window, dil in DIL_PATTERNS:
            cnt += ((delta % dil == 0) & (np.abs(delta) <= window // 2)).astype(np.int32)
        tabs.append(np.where(cnt > 0, np.log2(np.maximum(cnt, 1).astype(np.float64)), NEG))
    tabs.append(np.full((TILE, TILE), NEG))
    return offsets, jnp.asarray(np.tile(np.stack(tabs), (1, 1, 2)), F32)


def _na_bias_kernel(rows, rpb_ref, o_ref):
    hd = pl.program_id(0)
    kh = min(NA_KH, rows)
    q_rows = TILE // GRID_W
    kc = lax.broadcasted_iota(jnp.int32, (GRID_W, LANES), 0)
    lane = lax.broadcasted_iota(jnp.int32, (GRID_W, LANES), 1)
    qc = lane % GRID_W
    dc = jnp.clip(kc - qc, -(NA_KW - 1), NA_KW - 1) + (NA_KW - 1)
    cs = jnp.clip(qc - NA_KW // 2, 0, GRID_W - NA_KW)
    col_ok = (kc >= cs) & (kc < cs + NA_KW)
    neg = jnp.full((GRID_W, LANES), NEG, F32)
    tiles = []
    for dr in range(2 * NA_KH - 1):
        t = neg
        for j in range(2 * NA_KW - 1):
            t = jnp.where(dc == j, rpb_ref[hd, dr, j] * LOG2E, t)
        tiles.append(jnp.where(col_ok, t, neg))
    for v, r0 in enumerate((0, 2 * q_rows, rows - q_rows)):
        for kr_rel in range(3 * q_rows):
            kr = r0 - q_rows + kr_rel
            for pair in range(q_rows // 2):
                halves = []
                for r in (r0 + 2 * pair, r0 + 2 * pair + 1):
                    rs = min(max(r - kh // 2, 0), rows - kh)
                    halves.append(tiles[kr - r + NA_KH - 1] if rs <= kr < rs + kh else neg)
                blk = jnp.where(lane < GRID_W, halves[0], halves[1])
                o_ref[v, 0, kr_rel * GRID_W:(kr_rel + 1) * GRID_W, pair * LANES:(pair + 1) * LANES] = blk


def _na_bias(rpb, rows):
    n_heads = rpb.shape[0]
    return pl.pallas_call(
        functools.partial(_na_bias_kernel, rows),
        grid=(n_heads,),
        in_specs=[pl.BlockSpec(memory_space=pltpu.SMEM)],
        out_specs=pl.BlockSpec((3, 1, 3 * TILE, TILE), lambda h: (0, h // 2, 0, h % 2)),
        out_shape=jax.ShapeDtypeStruct((3, n_heads // 2, 3 * TILE, 2 * TILE), F32),
        compiler_params=_cparams(("arbitrary",)),
        name="na_bias_table",
    )(rpb)


def _outproj_kernel(x_ref, mod_ref, oa_ref, ob_ref, oc_ref, w_ref, y_ref):
    mix = jnp.dot(oa_ref[0], w_ref[0:W_DIFF, :], preferred_element_type=F32)
    mix += jnp.dot(ob_ref[0], w_ref[W_DIFF:W_DIFF + W_DIL, :], preferred_element_type=F32)
    mix += jnp.dot(oc_ref[0], w_ref[W_DIFF + W_DIL:, :], preferred_element_type=F32)
    y_ref[0] = x_ref[0] + mod_ref[0, 2:3, :] * mix


def _out_projection(x, mod_l, o_a, o_b, o_c, w_out, tm):
    bsz, s, d = x.shape
    row = lambda w: pl.BlockSpec((1, tm, w), lambda b, i: (b, i, 0))
    return pl.pallas_call(
        _outproj_kernel,
        grid=(bsz, s // tm),
        in_specs=[
            row(d),
            pl.BlockSpec((1, 6, d), lambda b, i: (b, 0, 0)),
            row(W_DIFF), row(W_DIL), row(W_NA),
            pl.BlockSpec(w_out.shape, lambda b, i: (0, 0)),
        ],
        out_specs=row(d),
        out_shape=jax.ShapeDtypeStruct((bsz, s, d), F32),
        compiler_params=_cparams(("arbitrary", "arbitrary")),
        name="out_proj_residual",
    )(x, mod_l, o_a, o_b, o_c, w_out)


def _ffn_up_kernel(x_ref, xp_ref, xn_ref, mod_ref, g_ref, wg_ref, wu_ref, cw_ref, cb_ref,
                   a_ref, h_s, g_s):
    i = pl.program_id(1)
    tm = x_ref.shape[1]
    g, sc, sh = g_ref[...], mod_ref[0, 4:5, :], mod_ref[0, 3:4, :]
    keep_prev = (i > 0).astype(F32)
    keep_next = (i < pl.num_programs(1) - 1).astype(F32)
    h_s[0:HALO, :] = (_norm_mod(xp_ref[0], g, sc, sh) * keep_prev).astype(BF16)
    h_s[HALO:HALO + tm, :] = _norm_mod(x_ref[0], g, sc, sh).astype(BF16)
    h_s[HALO + tm:, :] = (_norm_mod(xn_ref[0], g, sc, sh) * keep_next).astype(BF16)
    for c in range(wg_ref.shape[0]):
        g_s[...] = jnp.dot(h_s[...], wg_ref[c], preferred_element_type=F32)
        cw = cw_ref[c]
        gc = cb_ref[c]
        for t in range(CONV_W):
            gc = gc + g_s[pl.ds(HALO - CONV_W // 2 + t, tm), :] * cw[t:t + 1, :]
        u = jnp.dot(h_s[HALO:HALO + tm, :], wu_ref[c], preferred_element_type=F32)
        a_ref[0, :, c * FF_CHUNK:(c + 1) * FF_CHUNK] = (gc * jax.nn.sigmoid(gc) * u).astype(BF16)


def _ffn_up(x, mod_l, g, wg, wu, cw, cb, tm):
    bsz, s, d = x.shape
    n_ch = wg.shape[0]
    hb = tm // HALO
    n_hb = s // HALO
    return pl.pallas_call(
        _ffn_up_kernel,
        grid=(bsz, s // tm),
        in_specs=[
            pl.BlockSpec((1, tm, d), lambda b, i: (b, i, 0)),
            pl.BlockSpec((1, HALO, d), lambda b, i: (b, jnp.maximum(i * hb - 1, 0), 0)),
            pl.BlockSpec((1, HALO, d), lambda b, i: (b, jnp.minimum((i + 1) * hb, n_hb - 1), 0)),
            pl.BlockSpec((1, 6, d), lambda b, i: (b, 0, 0)),
            pl.BlockSpec((1, d), lambda b, i: (0, 0)),
            pl.BlockSpec(wg.shape, lambda b, i: (0, 0, 0)),
            pl.BlockSpec(wu.shape, lambda b, i: (0, 0, 0)),
            pl.BlockSpec(cw.shape, lambda b, i: (0, 0, 0)),
            pl.BlockSpec(cb.shape, lambda b, i: (0, 0, 0)),
        ],
        out_specs=pl.BlockSpec((1, tm, n_ch * FF_CHUNK), lambda b, i: (b, i, 0)),
        out_shape=jax.ShapeDtypeStruct((bsz, s, n_ch * FF_CHUNK), BF16),
        scratch_shapes=[
            pltpu.VMEM((tm + 2 * HALO, d), BF16),
            pltpu.VMEM((tm + 2 * HALO, FF_CHUNK), F32),
        ],
        compiler_params=_cparams(("arbitrary", "arbitrary")),
        name="ffn_up_conv_glu",
    )(x, x, x, mod_l, g.reshape(1, d), wg, wu, cw, cb)


def _ffn_down_kernel(final, x_ref, mod_ref, a_ref, w_ref, gf_ref, y_ref):
    y = x_ref[0] + mod_ref[0, 5:6, :] * jnp.dot(a_ref[0], w_ref[...], preferred_element_type=F32)
    if final:
        ms = jnp.mean(y * y, axis=-1, keepdims=True)
        y = (y * lax.rsqrt(ms + EPS)) * gf_ref[...]
    y_ref[0] = y


def _ffn_down(x, mod_l, a, w_down, g_final, final, tm):
    bsz, s, d = x.shape
    return pl.pallas_call(
        functools.partial(_ffn_down_kernel, final),
        grid=(bsz, s // tm),
        in_specs=[
            pl.BlockSpec((1, tm, d), lambda b, i: (b, i, 0)),
            pl.BlockSpec((1, 6, d), lambda b, i: (b, 0, 0)),
            pl.BlockSpec((1, tm, a.shape[2]), lambda b, i: (b, i, 0)),
            pl.BlockSpec(w_down.shape, lambda b, i: (0, 0)),
            pl.BlockSpec((1, d), lambda b, i: (0, 0)),
        ],
        out_specs=pl.BlockSpec((1, tm, d), lambda b, i: (b, i, 0)),
        out_shape=jax.ShapeDtypeStruct((bsz, s, d), F32),
        compiler_params=_cparams(("arbitrary", "arbitrary")),
        name="ffn_down_residual",
    )(x, mod_l, a, w_down, g_final.reshape(1, d))


def kernel(x, c, w_ada, b_ada, g_attn, w_in, diff_lambda, diff_subln, na_rpb, w_out, g_ffn, w_up,
           conv_w, conv_b, w_down, g_final):
    bsz, s, d = x.shape
    depth = w_ada.shape[0]
    d_ff = w_down.shape[1]
    n_ch = d_ff // FF_CHUNK
    tm = 512
    assert s % tm == 0 and tm % TILE == 0 and d_ff % FF_CHUNK == 0 and TILE % (2 * GRID_W) == 0
    assert (s // TILE) % 2 == 0 and s // GRID_W >= 4 * (TILE // GRID_W)

    mod = _modulation(c, w_ada, b_ada).reshape(depth, bsz, 6, d)
    tables = _rope_tables(s, DIFF_QK_DIM) + _rope_tables(s, HEAD_DIM)
    dil_offsets, dil_bias = _dilated_bias()
    rows = s // GRID_W
    n_qblk = s // TILE

    wa, wb, wc = W_DIFF, W_DIL, W_NA
    offs = np.cumsum([0, wa, wa, wa, wb, wb, wb, wc, wc, wc])
    order = (0, 3, 6, 1, 4, 7, 2, 5, 8)

    for l in range(depth):
        w_perm = jnp.concatenate([w_in[l][:, offs[j]:offs[j + 1]] for j in order], axis=1).astype(BF16)
        qt, k, vt, kmax = _projection(x, mod[l], g_attn[l], w_perm, tables, tm)

        o_a = _diff_attention(qt, k, vt, kmax, diff_lambda[l], diff_subln[l], l, 2 * TILE)
        o_b = _band_attention(
            qt, k, vt, dil_bias,
            pl.BlockSpec(dil_bias.shape, lambda b, p, i: (0, 0, 0)),
            dil_offsets, False, N_HEADS_DIFF // 2, N_HEADS_DIL // 2, "dilated_attention")
        o_c = _band_attention(
            qt, k, vt, _na_bias(na_rpb[l], rows),
            pl.BlockSpec((1, 1, 3 * TILE, 2 * TILE),
                         lambda b, p, i: (jnp.where(i == 0, 0, jnp.where(i == n_qblk - 1, 2, 1)), p, 0, 0)),
            (-1, 0, 1), True, (N_HEADS_DIFF + N_HEADS_DIL) // 2, N_HEADS_NA // 2, "neighbourhood_attention")

        x = _out_projection(x, mod[l], o_a, o_b, o_c, w_out[l].astype(BF16), tm)

        wg = w_up[l][:, :d_ff].reshape(d, n_ch, FF_CHUNK).transpose(1, 0, 2).astype(BF16)
        wu = w_up[l][:, d_ff:].reshape(d, n_ch, FF_CHUNK).transpose(1, 0, 2).astype(BF16)
        cw = conv_w[l].reshape(CONV_W, n_ch, FF_CHUNK).transpose(1, 0, 2)
        cb = conv_b[l].reshape(n_ch, 1, FF_CHUNK)
        a = _ffn_up(x, mod[l], g_ffn[l], wg, wu, cw, cb, tm)
        x = _ffn_down(x, mod[l], a, w_down[l].astype(BF16), g_final, l == depth - 1, tm)
    return x
```

```python
import functools
import math

import numpy as np
import jax
import jax.numpy as jnp
from jax import lax
from jax.experimental import pallas as pl
from jax.experimental.pallas import tpu as pltpu

F32 = jnp.float32
BF16 = jnp.bfloat16

HEAD_DIM = 64
N_HEADS_DIFF = 4
N_HEADS_DIL = 6
N_HEADS_NA = 6
W_DIFF = N_HEADS_DIFF * HEAD_DIM
W_DIL = N_HEADS_DIL * HEAD_DIM
W_NA = N_HEADS_NA * HEAD_DIM
DIFF_QK_DIM = HEAD_DIM // 2
DIL_PATTERNS = ((128, 1), (512, 4), (2048, 16))
GRID_W = 64
NA_KH = 8
NA_KW = 16
CONV_W = 3
ROPE_THETA = 10000.0
EPS = 1e-6
NEG = -1e30

LANES = 128
PAIR_W = 2 * HEAD_DIM
TILE = 256
FF_CHUNK = 256
HALO = 16
VMEM_LIMIT = 56 * 1024 * 1024
LOG2E = math.log2(math.e)
SUM_ROWS = 16
LAG_MARGIN = 64.0
BAND_Q_TILES = 4
LAG_UNROLL = 7
QK_AHEAD = 4


def _cparams(sem):
    return pltpu.CompilerParams(dimension_semantics=sem, vmem_limit_bytes=VMEM_LIMIT)


def _mod_kernel(ct_ref, w_ref, b_ref, o_ref):
    ct = ct_ref[...]
    s = ct * jax.nn.sigmoid(ct)
    w = w_ref[0]
    rows = [jnp.sum(w * s[:, b:b + 1], axis=0, keepdims=True) for b in range(ct.shape[1])]
    o_ref[0] = jnp.concatenate(rows, axis=0) + b_ref[0]


def _modulation(c, w_ada, b_ada):
    depth, d, n = w_ada.shape
    bsz = c.shape[0]
    tn = 768
    return pl.pallas_call(
        _mod_kernel,
        grid=(depth, n // tn),
        in_specs=[
            pl.BlockSpec((d, bsz), lambda l, j: (0, 0)),
            pl.BlockSpec((1, d, tn), lambda l, j: (l, 0, j)),
            pl.BlockSpec((1, 1, tn), lambda l, j: (l, 0, j)),
        ],
        out_specs=pl.BlockSpec((1, bsz, tn), lambda l, j: (l, 0, j)),
        out_shape=jax.ShapeDtypeStruct((depth, bsz, n), F32),
        compiler_params=_cparams(("arbitrary", "arbitrary")),
        name="adaln_mod",
    )(c.T, w_ada, b_ada.reshape(depth, 1, n))


def _norm_mod(x, g, sc, sh):
    ms = jnp.mean(x * x, axis=-1, keepdims=True)
    return (x * lax.rsqrt(ms + EPS)) * g * (1.0 + sc) + sh


def _rope_slab(acc, cos, sin, group):
    half = group // 2
    lane = lax.broadcasted_iota(jnp.int32, (1, LANES), 1) % group
    first = lane < half
    outs = []
    for c in range(acc.shape[1] // LANES):
        xc = acc[:, c * LANES:(c + 1) * LANES]
        swapped = jnp.where(first, pltpu.roll(xc, LANES - half, axis=1), pltpu.roll(xc, half, axis=1))
        outs.append(xc * cos + swapped * sin)
    return jnp.concatenate(outs, axis=1)


def _proj_kernel(x_ref, mod_ref, g_ref, w_ref, ca_ref, sa_ref, cb_ref, sb_ref,
                 qt_ref, k_ref, vt_ref, kmax_ref):
    tm = x_ref.shape[1]
    h = _norm_mod(x_ref[0], g_ref[...], mod_ref[0, 1:2, :], mod_ref[0, 0:1, :]).astype(BF16)
    ca, sa, cb, sb = ca_ref[...], sa_ref[...], cb_ref[...], sb_ref[...]
    width = W_DIFF + W_DIL + W_NA
    slabs = ((0, W_DIFF), (W_DIFF, W_DIL), (W_DIFF + W_DIL, W_NA))

    def slab(part, si):
        off, w = slabs[si]
        acc = jnp.dot(h, w_ref[:, part * width + off: part * width + off + w], preferred_element_type=F32)
        if part < 2 and si == 0:
            acc = _rope_slab(acc, ca, sa, DIFF_QK_DIM)
        elif part < 2 and si == 1:
            acc = _rope_slab(acc, cb, sb, HEAD_DIM)
        return acc

    q_scale = (LOG2E * DIFF_QK_DIM ** -0.5, LOG2E * HEAD_DIM ** -0.5, LOG2E * HEAD_DIM ** -0.5)
    for si, (off, w) in enumerate(slabs):
        q = slab(0, si) * q_scale[si]
        qt_ref[0, off:off + w, :] = q.T.astype(BF16)
        kb = slab(1, si).astype(BF16)
        k_ref[0, :, off:off + w] = kb
        kmax_ref[0, 0, :, off:off + w] = jnp.max(jnp.abs(kb.astype(F32)), axis=0, keepdims=True)
        v = slab(2, si)
        for t in range(tm // TILE):
            vt_ref[0, t, off:off + w, :] = v[t * TILE:(t + 1) * TILE, :].T.astype(BF16)


def _rope_tables(s, group):
    half = group // 2
    inv = ROPE_THETA ** (-jnp.arange(half, dtype=F32) / half)
    ang = jnp.arange(s, dtype=F32)[:, None] * inv
    cos, sin = jnp.cos(ang), jnp.sin(ang)
    reps = LANES // group
    return (jnp.tile(jnp.concatenate([cos, cos], axis=1), (1, reps)),
            jnp.tile(jnp.concatenate([-sin, sin], axis=1), (1, reps)))


def _projection(x, mod_l, g, w_perm, tables, tm):
    bsz, s, d = x.shape
    width = W_DIFF + W_DIL + W_NA
    tab_spec = pl.BlockSpec((tm, LANES), lambda b, i: (i, 0))
    return pl.pallas_call(
        _proj_kernel,
        grid=(bsz, s // tm),
        in_specs=[
            pl.BlockSpec((1, tm, d), lambda b, i: (b, i, 0)),
            pl.BlockSpec((1, 6, d), lambda b, i: (b, 0, 0)),
            pl.BlockSpec((1, d), lambda b, i: (0, 0)),
            pl.BlockSpec((d, 3 * width), lambda b, i: (0, 0)),
            tab_spec, tab_spec, tab_spec, tab_spec,
        ],
        out_specs=[
            pl.BlockSpec((1, width, tm), lambda b, i: (b, 0, i)),
            pl.BlockSpec((1, tm, width), lambda b, i: (b, i, 0)),
            pl.BlockSpec((1, tm // TILE, width, TILE), lambda b, i: (b, i, 0, 0)),
            pl.BlockSpec((1, 1, 1, width), lambda b, i: (b, i, 0, 0)),
        ],
        out_shape=[
            jax.ShapeDtypeStruct((bsz, width, s), BF16),
            jax.ShapeDtypeStruct((bsz, s, width), BF16),
            jax.ShapeDtypeStruct((bsz, s // TILE, width, TILE), BF16),
            jax.ShapeDtypeStruct((bsz, s // tm, 1, width), F32),
        ],
        compiler_params=_cparams(("arbitrary", "arbitrary")),
        name="norm_proj_rope",
    )(x, mod_l, g.reshape(1, d), w_perm, *tables)


def _masked_queries(qt, n_split):
    row = lax.broadcasted_iota(jnp.int32, (PAIR_W, 1), 0)
    step = PAIR_W // n_split
    return [jnp.where((row >= j * step) & (row < (j + 1) * step), qt, jnp.zeros_like(qt))
            for j in range(n_split)]


def _v_with_ones(vtb, hd):
    ones = jnp.ones((SUM_ROWS, vtb.shape[1]), BF16)
    return jnp.concatenate([vtb[hd * HEAD_DIM:(hd + 1) * HEAD_DIM, :], ones], axis=0)


def _diff_kernel(lam_init, dl_ref, g_ref, kmax_ref, qt_ref, k_ref, vt_ref, o_ref, q_s, s_s, m_s, acc_s):
    tq = qt_ref.shape[2]
    n_kt = vt_ref.shape[1]
    for j, qm in enumerate(_masked_queries(qt_ref[0], 4)):
        q_s[:, j * tq:(j + 1) * tq] = qm
    lane_tiles = [slice(nt * TILE, (nt + 1) * TILE) for nt in range(4 * tq // TILE)]

    def k_tile(kt):
        return k_ref[0, pl.ds(pl.multiple_of(kt * TILE, TILE), TILE), :]

    def v_tiles(kt):
        vtb = vt_ref[0, kt]
        return [_v_with_ones(vtb, hd) for hd in range(2)]

    def exact_step(kt, first):
        kb, v_ext = k_tile(kt), v_tiles(kt)
        for ln in lane_tiles:
            s_s[:, ln] = jnp.dot(kb, q_s[:, ln], preferred_element_type=F32)
        for ln in lane_tiles:
            tile_max = jnp.max(s_s[:, ln], axis=0, keepdims=True)
            m_new = tile_max if first else jnp.maximum(m_s[:, ln], tile_max)
            e = jnp.exp2(s_s[:, ln] - m_new).astype(BF16)
            pv = jnp.dot(v_ext[ln.start // (2 * tq)], e, preferred_element_type=F32)
            acc_s[:, ln] = pv if first else acc_s[:, ln] * jnp.exp2(m_s[:, ln] - m_new) + pv
            m_s[:, ln] = m_new

    def lagging_steps(kts):
        def finish(v_ext, ln, s):
            m_used = m_s[:, ln]
            e = jnp.exp2(s - m_used).astype(BF16)
            m_new = jnp.maximum(m_used, jnp.max(s, axis=0, keepdims=True))
            pv = jnp.dot(v_ext[ln.start // (2 * tq)], e, preferred_element_type=F32)
            acc_s[:, ln] = (acc_s[:, ln] + pv) * jnp.exp2(m_used - m_new)
            m_s[:, ln] = m_new

        pending = []
        for kt in kts:
            kb, v_ext = k_tile(kt), v_tiles(kt)
            for ln in lane_tiles:
                pending.append((v_ext, ln, jnp.dot(kb, q_s[:, ln], preferred_element_type=F32)))
                if len(pending) > QK_AHEAD:
                    finish(*pending.pop(0))
        for item in pending:
            finish(*item)

    exact_step(0, True)

    kmax = jnp.max(kmax_ref[0], axis=0) * (1.0 + 2.0 ** -7)
    kmax = jnp.broadcast_to(kmax, (SUM_ROWS, PAIR_W)).astype(BF16)
    bound = jnp.dot(kmax, jnp.abs(q_s[...]), preferred_element_type=F32)[0:1] * (1.0 + 2.0 ** -7)
    lag_ok = jnp.max(bound - m_s[...]) < LAG_MARGIN

    @pl.when(lag_ok)
    def _():
        def body(tt, carry):
            lagging_steps([LAG_UNROLL * tt + 1 + u for u in range(LAG_UNROLL)])
            return carry
        n_trips = (n_kt - 1) // LAG_UNROLL
        lax.fori_loop(0, n_trips, body, 0)
        if n_trips * LAG_UNROLL + 1 < n_kt:
            lagging_steps(list(range(n_trips * LAG_UNROLL + 1, n_kt)))

    @pl.when(jnp.logical_not(lag_ok))
    def _():
        def body(kt, carry):
            exact_step(kt, False)
            return carry
        lax.fori_loop(1, n_kt, body, 0)

    dl = dl_ref[...]
    lam = (jnp.exp(jnp.sum(dl[0:1] * dl[1:2], axis=1, keepdims=True))
           - jnp.exp(jnp.sum(dl[2:3] * dl[3:4], axis=1, keepdims=True)) + lam_init)
    g = g_ref[...]
    outs = []
    for hd in range(2):
        p = [acc_s[0:HEAD_DIM, (2 * hd + mp) * tq:(2 * hd + mp + 1) * tq]
             / acc_s[HEAD_DIM:HEAD_DIM + 1, (2 * hd + mp) * tq:(2 * hd + mp + 1) * tq] for mp in range(2)]
        o = p[0] - lam * p[1]
        ms = jnp.mean(o * o, axis=0, keepdims=True)
        outs.append((o * lax.rsqrt(ms + EPS)) * g * (1.0 - lam_init))
    o_ref[0] = jnp.concatenate(outs, axis=0).T.astype(BF16)


def _diff_attention(qt, k, vt, kmax, diff_lambda, subln, layer_idx, tq):
    bsz, _, s = qt.shape
    n_pairs = N_HEADS_DIFF // 2
    lam_init = 0.8 - 0.6 * math.exp(-0.3 * layer_idx)
    return pl.pallas_call(
        functools.partial(_diff_kernel, lam_init),
        grid=(bsz, n_pairs, s // tq),
        in_specs=[
            pl.BlockSpec(diff_lambda.shape, lambda b, p, i: (0, 0)),
            pl.BlockSpec((HEAD_DIM, 1), lambda b, p, i: (0, 0)),
            pl.BlockSpec((1, kmax.shape[1], 1, PAIR_W), lambda b, p, i: (b, 0, 0, p)),
            pl.BlockSpec((1, PAIR_W, tq), lambda b, p, i: (b, p, i)),
            pl.BlockSpec((1, s, PAIR_W), lambda b, p, i: (b, 0, p)),
            pl.BlockSpec((1, s // TILE, PAIR_W, TILE), lambda b, p, i: (b, 0, p, 0)),
        ],
        out_specs=pl.BlockSpec((1, tq, PAIR_W), lambda b, p, i: (b, i, p)),
        out_shape=jax.ShapeDtypeStruct((bsz, s, W_DIFF), BF16),
        scratch_shapes=[
            pltpu.VMEM((PAIR_W, 4 * tq), BF16),
            pltpu.VMEM((TILE, 4 * tq), F32),
            pltpu.VMEM((1, 4 * tq), F32),
            pltpu.VMEM((HEAD_DIM + SUM_ROWS, 4 * tq), F32),
        ],
        compiler_params=_cparams(("arbitrary", "arbitrary", "arbitrary")),
        name="diff_attention",
    )(diff_lambda, subln.reshape(HEAD_DIM, 1), kmax, qt, k, vt)


def _band_kernel(offsets, window_bias, bmax_ref, kmax_ref, qt_ref, k_ref, vt_ref, bias_ref, o_ref,
                 q_s, sc_s, s_s, acc_s):
    i = pl.program_id(2)
    n_blk = vt_ref.shape[1]
    n_off = len(offsets)
    centre = offsets.index(0)
    n_qt = qt_ref.shape[2] // TILE
    qt = qt_ref[0]
    for t in range(n_qt):
        for hd, qm in enumerate(_masked_queries(qt[:, t * TILE:(t + 1) * TILE], 2)):
            q_s[:, (2 * t + hd) * TILE:(2 * t + hd + 1) * TILE] = qm
    lane_tiles = [slice(c * TILE, (c + 1) * TILE) for c in range(2 * n_qt)]

    def q_tile(ln):
        return i * n_qt + ln.start // (2 * TILE)

    def key_tile(oi, ln):
        return jnp.clip(q_tile(ln) + offsets[oi], 0, n_blk - 1)

    def scores(oi, ln):
        qi = q_tile(ln)
        head_lanes = slice(ln.start % (2 * TILE), ln.start % (2 * TILE) + TILE)
        if window_bias:
            variant = jnp.where(qi == 0, 0, jnp.where(qi == n_blk - 1, 2, 1))
            bias = bias_ref[variant, 0, oi * TILE:(oi + 1) * TILE, head_lanes]
        else:
            kbi = qi + offsets[oi]
            bias = bias_ref[jnp.where((kbi >= 0) & (kbi < n_blk), oi, n_off), :, head_lanes]
        kb = k_ref[0, pl.ds(pl.multiple_of(key_tile(oi, ln) * TILE, TILE), TILE), :]
        return jnp.dot(kb, q_s[:, ln], preferred_element_type=F32) + bias

    def weighted_values(oi, ln, e):
        vtb = vt_ref[0, key_tile(oi, ln)]
        return jnp.dot(_v_with_ones(vtb, (ln.start // TILE) % 2), e, preferred_element_type=F32)

    for ln in lane_tiles:
        sc_s[:, ln] = scores(centre, ln)
    m_c = jnp.max(sc_s[...], axis=0, keepdims=True)

    kmax = jnp.max(kmax_ref[0], axis=0) * (1.0 + 2.0 ** -7)
    kmax = jnp.broadcast_to(kmax, (SUM_ROWS, PAIR_W)).astype(BF16)
    bound = jnp.dot(kmax, jnp.abs(q_s[...]), preferred_element_type=F32)[0:1] * (1.0 + 2.0 ** -7) + bmax_ref[...]
    shift_ok = jnp.max(bound - m_c) < LAG_MARGIN

    @pl.when(shift_ok)
    def _():
        acc = {ln.start: None for ln in lane_tiles}

        def finish(oi, ln, s):
            pv = weighted_values(oi, ln, jnp.exp2(s - m_c[:, ln]).astype(BF16))
            acc[ln.start] = pv if acc[ln.start] is None else acc[ln.start] + pv

        pending = []
        for oi in range(n_off):
            for ln in lane_tiles:
                pending.append((oi, ln, sc_s[:, ln] if oi == centre else scores(oi, ln)))
                if len(pending) > QK_AHEAD:
                    finish(*pending.pop(0))
        for item in pending:
            finish(*item)
        for ln in lane_tiles:
            acc_s[:, ln] = acc[ln.start]

    @pl.when(jnp.logical_not(shift_ok))
    def _():
        for ln in lane_tiles:
            col_max = m_c[:, ln]
            for oi in range(n_off):
                s_s[oi] = sc_s[:, ln] if oi == centre else scores(oi, ln)
                col_max = jnp.maximum(col_max, jnp.max(s_s[oi], axis=0, keepdims=True))
            acc = None
            for oi in range(n_off):
                pv = weighted_values(oi, ln, jnp.exp2(s_s[oi] - col_max).astype(BF16))
                acc = pv if acc is None else acc + pv
            acc_s[:, ln] = acc

    out = acc_s[0:HEAD_DIM, :] / acc_s[HEAD_DIM:HEAD_DIM + 1, :]
    for t in range(n_qt):
        pair = jnp.concatenate([out[:, lane_tiles[2 * t]], out[:, lane_tiles[2 * t + 1]]], axis=0)
        o_ref[0, t * TILE:(t + 1) * TILE, :] = pair.T.astype(BF16)


def _band_attention(qt, k, vt, kmax, bias, bias_spec, bias_max, offsets, window_bias, pair0, n_pairs, name):
    bsz, _, s = qt.shape
    tq = BAND_Q_TILES * TILE
    return pl.pallas_call(
        functools.partial(_band_kernel, offsets, window_bias),
        grid=(bsz, n_pairs, s // tq),
        in_specs=[
            pl.BlockSpec((1, 1), lambda b, p, i: (0, 0)),
            pl.BlockSpec((1, kmax.shape[1], 1, PAIR_W), lambda b, p, i: (b, 0, 0, pair0 + p)),
            pl.BlockSpec((1, PAIR_W, tq), lambda b, p, i: (b, pair0 + p, i)),
            pl.BlockSpec((1, s, PAIR_W), lambda b, p, i: (b, 0, pair0 + p)),
            pl.BlockSpec((1, s // TILE, PAIR_W, TILE), lambda b, p, i: (b, 0, pair0 + p, 0)),
            bias_spec,
        ],
        out_specs=pl.BlockSpec((1, tq, PAIR_W), lambda b, p, i: (b, i, p)),
        out_shape=jax.ShapeDtypeStruct((bsz, s, n_pairs * PAIR_W), BF16),
        scratch_shapes=[
            pltpu.VMEM((PAIR_W, 2 * tq), BF16),
            pltpu.VMEM((TILE, 2 * tq), F32),
            pltpu.VMEM((len(offsets), TILE, TILE), F32),
            pltpu.VMEM((HEAD_DIM + SUM_ROWS, 2 * tq), F32),
        ],
        compiler_params=_cparams(("arbitrary", "arbitrary", "arbitrary")),
        name=name,
    )(jnp.reshape(bias_max, (1, 1)).astype(F32), kmax, qt, k, vt, bias)


def _dilated_bias():
    reach = max(w // 2 for w, _ in DIL_PATTERNS)
    n_off = -(-reach // TILE)
    offsets = tuple(range(-n_off, n_off + 1))
    kj = np.arange(TILE)[:, None]
    qi = np.arange(TILE)[None, :]
    tabs = []
    for off in offsets:
        delta = off * TILE + kj - qi
        cnt = np.zeros((TILE, TILE), np.int32)
        for window, dil in DIL_PATTERNS:
            cnt += ((delta % dil == 0) & (np.abs(delta) <= window // 2)).astype(np.int32)
        tabs.append(np.where(cnt > 0, np.log2(np.maximum(cnt, 1).astype(np.float64)), NEG))
    tabs.append(np.full((TILE, TILE), NEG))
    return offsets, jnp.asarray(np.tile(np.stack(tabs), (1, 1, 2)), F32)


def _na_bias_kernel(rows, rpb_ref, o_ref):
    hd = pl.program_id(0)
    kh = min(NA_KH, rows)
    q_rows = TILE // GRID_W
    kc = lax.broadcasted_iota(jnp.int32, (GRID_W, LANES), 0)
    lane = lax.broadcasted_iota(jnp.int32, (GRID_W, LANES), 1)
    qc = lane % GRID_W
    dc = jnp.clip(kc - qc, -(NA_KW - 1), NA_KW - 1) + (NA_KW - 1)
    cs = jnp.clip(qc - NA_KW // 2, 0, GRID_W - NA_KW)
    col_ok = (kc >= cs) & (kc < cs + NA_KW)
    neg = jnp.full((GRID_W, LANES), NEG, F32)
    tiles = []
    for dr in range(2 * NA_KH - 1):
        t = neg
        for j in range(2 * NA_KW - 1):
            t = jnp.where(dc == j, rpb_ref[hd, dr, j] * LOG2E, t)
        tiles.append(jnp.where(col_ok, t, neg))
    for v, r0 in enumerate((0, 2 * q_rows, rows - q_rows)):
        for kr_rel in range(3 * q_rows):
            kr = r0 - q_rows + kr_rel
            for pair in range(q_rows // 2):
                halves = []
                for r in (r0 + 2 * pair, r0 + 2 * pair + 1):
                    rs = min(max(r - kh // 2, 0), rows - kh)
                    halves.append(tiles[kr - r + NA_KH - 1] if rs <= kr < rs + kh else neg)
                blk = jnp.where(lane < GRID_W, halves[0], halves[1])
                o_ref[v, 0, kr_rel * GRID_W:(kr_rel + 1) * GRID_W, pair * LANES:(pair + 1) * LANES] = blk


def _na_bias(rpb, rows):
    n_heads = rpb.shape[0]
    return pl.pallas_call(
        functools.partial(_na_bias_kernel, rows),
        grid=(n_heads,),
        in_specs=[pl.BlockSpec(memory_space=pltpu.SMEM)],
        out_specs=pl.BlockSpec((3, 1, 3 * TILE, TILE), lambda h: (0, h // 2, 0, h % 2)),
        out_shape=jax.ShapeDtypeStruct((3, n_heads // 2, 3 * TILE, 2 * TILE), F32),
        compiler_params=_cparams(("arbitrary",)),
        name="na_bias_table",
    )(rpb)


def _outproj_kernel(x_ref, mod_ref, oa_ref, ob_ref, oc_ref, w_ref, y_ref):
    mix = jnp.dot(oa_ref[0], w_ref[0:W_DIFF, :], preferred_element_type=F32)
    mix += jnp.dot(ob_ref[0], w_ref[W_DIFF:W_DIFF + W_DIL, :], preferred_element_type=F32)
    mix += jnp.dot(oc_ref[0], w_ref[W_DIFF + W_DIL:, :], preferred_element_type=F32)
    y_ref[0] = x_ref[0] + mod_ref[0, 2:3, :] * mix


def _out_projection(x, mod_l, o_a, o_b, o_c, w_out, tm):
    bsz, s, d = x.shape
    row = lambda w: pl.BlockSpec((1, tm, w), lambda b, i: (b, i, 0))
    return pl.pallas_call(
        _outproj_kernel,
        grid=(bsz, s // tm),
        in_specs=[
            row(d),
            pl.BlockSpec((1, 6, d), lambda b, i: (b, 0, 0)),
            row(W_DIFF), row(W_DIL), row(W_NA),
            pl.BlockSpec(w_out.shape, lambda b, i: (0, 0)),
        ],
        out_specs=row(d),
        out_shape=jax.ShapeDtypeStruct((bsz, s, d), F32),
        compiler_params=_cparams(("arbitrary", "arbitrary")),
        name="out_proj_residual",
    )(x, mod_l, o_a, o_b, o_c, w_out)


def _ffn_up_kernel(x_ref, xp_ref, xn_ref, mod_ref, g_ref, wg_ref, wu_ref, cw_ref, cb_ref,
                   a_ref, h_s, g_s):
    i = pl.program_id(1)
    tm = x_ref.shape[1]
    g, sc, sh = g_ref[...], mod_ref[0, 4:5, :], mod_ref[0, 3:4, :]
    keep_prev = (i > 0).astype(F32)
    keep_next = (i < pl.num_programs(1) - 1).astype(F32)
    h_s[0:HALO, :] = (_norm_mod(xp_ref[0], g, sc, sh) * keep_prev).astype(BF16)
    h_s[HALO:HALO + tm, :] = _norm_mod(x_ref[0], g, sc, sh).astype(BF16)
    h_s[HALO + tm:, :] = (_norm_mod(xn_ref[0], g, sc, sh) * keep_next).astype(BF16)
    for c in range(wg_ref.shape[0]):
        g_s[...] = jnp.dot(h_s[...], wg_ref[c], preferred_element_type=F32)
        cw = cw_ref[c]
        gc = cb_ref[c]
        for t in range(CONV_W):
            gc = gc + g_s[pl.ds(HALO - CONV_W // 2 + t, tm), :] * cw[t:t + 1, :]
        u = jnp.dot(h_s[HALO:HALO + tm, :], wu_ref[c], preferred_element_type=F32)
        a_ref[0, :, c * FF_CHUNK:(c + 1) * FF_CHUNK] = (gc * jax.nn.sigmoid(gc) * u).astype(BF16)


def _ffn_up(x, mod_l, g, wg, wu, cw, cb, tm):
    bsz, s, d = x.shape
    n_ch = wg.shape[0]
    hb = tm // HALO
    n_hb = s // HALO
    return pl.pallas_call(
        _ffn_up_kernel,
        grid=(bsz, s // tm),
        in_specs=[
            pl.BlockSpec((1, tm, d), lambda b, i: (b, i, 0)),
            pl.BlockSpec((1, HALO, d), lambda b, i: (b, jnp.maximum(i * hb - 1, 0), 0)),
            pl.BlockSpec((1, HALO, d), lambda b, i: (b, jnp.minimum((i + 1) * hb, n_hb - 1), 0)),
            pl.BlockSpec((1, 6, d), lambda b, i: (b, 0, 0)),
            pl.BlockSpec((1, d), lambda b, i: (0, 0)),
            pl.BlockSpec(wg.shape, lambda b, i: (0, 0, 0)),
            pl.BlockSpec(wu.shape, lambda b, i: (0, 0, 0)),
            pl.BlockSpec(cw.shape, lambda b, i: (0, 0, 0)),
            pl.BlockSpec(cb.shape, lambda b, i: (0, 0, 0)),
        ],
        out_specs=pl.BlockSpec((1, tm, n_ch * FF_CHUNK), lambda b, i: (b, i, 0)),
        out_shape=jax.ShapeDtypeStruct((bsz, s, n_ch * FF_CHUNK), BF16),
        scratch_shapes=[
            pltpu.VMEM((tm + 2 * HALO, d), BF16),
            pltpu.VMEM((tm + 2 * HALO, FF_CHUNK), F32),
        ],
        compiler_params=_cparams(("arbitrary", "arbitrary")),
        name="ffn_up_conv_glu",
    )(x, x, x, mod_l, g.reshape(1, d), wg, wu, cw, cb)


def _ffn_down_kernel(final, x_ref, mod_ref, a_ref, w_ref, gf_ref, y_ref):
    y = x_ref[0] + mod_ref[0, 5:6, :] * jnp.dot(a_ref[0], w_ref[...], preferred_element_type=F32)
    if final:
        ms = jnp.mean(y * y, axis=-1, keepdims=True)
        y = (y * lax.rsqrt(ms + EPS)) * gf_ref[...]
    y_ref[0] = y


def _ffn_down(x, mod_l, a, w_down, g_final, final, tm):
    bsz, s, d = x.shape
    return pl.pallas_call(
        functools.partial(_ffn_down_kernel, final),
        grid=(bsz, s // tm),
        in_specs=[
            pl.BlockSpec((1, tm, d), lambda b, i: (b, i, 0)),
            pl.BlockSpec((1, 6, d), lambda b, i: (b, 0, 0)),
            pl.BlockSpec((1, tm, a.shape[2]), lambda b, i: (b, i, 0)),
            pl.BlockSpec(w_down.shape, lambda b, i: (0, 0)),
            pl.BlockSpec((1, d), lambda b, i: (0, 0)),
        ],
        out_specs=pl.BlockSpec((1, tm, d), lambda b, i: (b, i, 0)),
        out_shape=jax.ShapeDtypeStruct((bsz, s, d), F32),
        compiler_params=_cparams(("arbitrary", "arbitrary")),
        name="ffn_down_residual",
    )(x, mod_l, a, w_down, g_final.reshape(1, d))


def kernel(x, c, w_ada, b_ada, g_attn, w_in, diff_lambda, diff_subln, na_rpb, w_out, g_ffn, w_up,
           conv_w, conv_b, w_down, g_final):
    bsz, s, d = x.shape
    depth = w_ada.shape[0]
    d_ff = w_down.shape[1]
    n_ch = d_ff // FF_CHUNK
    tm = 512
    assert s % tm == 0 and tm % TILE == 0 and d_ff % FF_CHUNK == 0 and TILE % (2 * GRID_W) == 0
    assert (s // TILE) % 2 == 0 and s // GRID_W >= 4 * (TILE // GRID_W)

    mod = _modulation(c, w_ada, b_ada).reshape(depth, bsz, 6, d)
    tables = _rope_tables(s, DIFF_QK_DIM) + _rope_tables(s, HEAD_DIM)
    dil_offsets, dil_bias = _dilated_bias()
    rows = s // GRID_W

    wa, wb, wc = W_DIFF, W_DIL, W_NA
    offs = np.cumsum([0, wa, wa, wa, wb, wb, wb, wc, wc, wc])
    order = (0, 3, 6, 1, 4, 7, 2, 5, 8)

    for l in range(depth):
        w_perm = jnp.concatenate([w_in[l][:, offs[j]:offs[j + 1]] for j in order], axis=1).astype(BF16)
        qt, k, vt, kmax = _projection(x, mod[l], g_attn[l], w_perm, tables, tm)

        o_a = _diff_attention(qt, k, vt, kmax, diff_lambda[l], diff_subln[l], l, 2 * TILE)
        o_b = _band_attention(
            qt, k, vt, kmax, dil_bias,
            pl.BlockSpec(dil_bias.shape, lambda b, p, i: (0, 0, 0)),
            jnp.float32(math.log2(len(DIL_PATTERNS))),
            dil_offsets, False, N_HEADS_DIFF // 2, N_HEADS_DIL // 2, "dilated_attention")
        o_c = _band_attention(
            qt, k, vt, kmax, _na_bias(na_rpb[l], rows),
            pl.BlockSpec((3, 1, 3 * TILE, 2 * TILE), lambda b, p, i: (0, p, 0, 0)),
            jnp.max(na_rpb[l]) * LOG2E,
            (-1, 0, 1), True, (N_HEADS_DIFF + N_HEADS_DIL) // 2, N_HEADS_NA // 2, "neighbourhood_attention")

        x = _out_projection(x, mod[l], o_a, o_b, o_c, w_out[l].astype(BF16), tm)

        wg = w_up[l][:, :d_ff].reshape(d, n_ch, FF_CHUNK).transpose(1, 0, 2).astype(BF16)
        wu = w_up[l][:, d_ff:].reshape(d, n_ch, FF_CHUNK).transpose(1, 0, 2).astype(BF16)
        cw = conv_w[l].reshape(CONV_W, n_ch, FF_CHUNK).transpose(1, 0, 2)
        cb = conv_b[l].reshape(n_ch, 1, FF_CHUNK)
        a = _ffn_up(x, mod[l], g_ffn[l], wg, wu, cw, cb, tm)
        x = _ffn_down(x, mod[l], a, w_down[l].astype(BF16), g_final, l == depth - 1, tm)
    return x
```

```python
import functools
import math

import numpy as np
import jax
import jax.numpy as jnp
from jax import lax
from jax.experimental import pallas as pl
from jax.experimental.pallas import tpu as pltpu

F32 = jnp.float32
BF16 = jnp.bfloat16

HEAD_DIM = 64
N_HEADS_DIFF = 4
N_HEADS_DIL = 6
N_HEADS_NA = 6
W_DIFF = N_HEADS_DIFF * HEAD_DIM
W_DIL = N_HEADS_DIL * HEAD_DIM
W_NA = N_HEADS_NA * HEAD_DIM
DIFF_QK_DIM = HEAD_DIM // 2
DIL_PATTERNS = ((128, 1), (512, 4), (2048, 16))
GRID_W = 64
NA_KH = 8
NA_KW = 16
CONV_W = 3
ROPE_THETA = 10000.0
EPS = 1e-6
NEG = -1e30

LANES = 128
PAIR_W = 2 * HEAD_DIM
TILE = 256
FF_CHUNK = 256
HALO = 16
VMEM_LIMIT = 56 * 1024 * 1024
LOG2E = math.log2(math.e)
SUM_ROWS = 16
LAG_MARGIN = 64.0
BAND_Q_TILES = 4
LAG_UNROLL = 7
QK_AHEAD = 4


def _cparams(sem):
    return pltpu.CompilerParams(dimension_semantics=sem, vmem_limit_bytes=VMEM_LIMIT)


def _mod_kernel(ct_ref, w_ref, b_ref, o_ref):
    ct = ct_ref[...]
    s = ct * jax.nn.sigmoid(ct)
    w = w_ref[0]
    rows = [jnp.sum(w * s[:, b:b + 1], axis=0, keepdims=True) for b in range(ct.shape[1])]
    o_ref[0] = jnp.concatenate(rows, axis=0) + b_ref[0]


def _modulation(c, w_ada, b_ada):
    depth, d, n = w_ada.shape
    bsz = c.shape[0]
    tn = 768
    return pl.pallas_call(
        _mod_kernel,
        grid=(depth, n // tn),
        in_specs=[
            pl.BlockSpec((d, bsz), lambda l, j: (0, 0)),
            pl.BlockSpec((1, d, tn), lambda l, j: (l, 0, j)),
            pl.BlockSpec((1, 1, tn), lambda l, j: (l, 0, j)),
        ],
        out_specs=pl.BlockSpec((1, bsz, tn), lambda l, j: (l, 0, j)),
        out_shape=jax.ShapeDtypeStruct((depth, bsz, n), F32),
        compiler_params=_cparams(("arbitrary", "arbitrary")),
        name="adaln_mod",
    )(c.T, w_ada, b_ada.reshape(depth, 1, n))


def _norm_mod(x, g, sc, sh):
    ms = jnp.mean(x * x, axis=-1, keepdims=True)
    return (x * lax.rsqrt(ms + EPS)) * g * (1.0 + sc) + sh


def _rope_slab(acc, cos, sin, group):
    half = group // 2
    lane = lax.broadcasted_iota(jnp.int32, (1, LANES), 1) % group
    first = lane < half
    outs = []
    for c in range(acc.shape[1] // LANES):
        xc = acc[:, c * LANES:(c + 1) * LANES]
        swapped = jnp.where(first, pltpu.roll(xc, LANES - half, axis=1), pltpu.roll(xc, half, axis=1))
        outs.append(xc * cos + swapped * sin)
    return jnp.concatenate(outs, axis=1)


def _proj_kernel(x_ref, mod_ref, g_ref, w_ref, ca_ref, sa_ref, cb_ref, sb_ref,
                 qt_ref, k_ref, vt_ref, kmax_ref):
    tm = x_ref.shape[1]
    h = _norm_mod(x_ref[0], g_ref[...], mod_ref[0, 1:2, :], mod_ref[0, 0:1, :]).astype(BF16)
    ca, sa, cb, sb = ca_ref[...], sa_ref[...], cb_ref[...], sb_ref[...]
    width = W_DIFF + W_DIL + W_NA
    slabs = ((0, W_DIFF), (W_DIFF, W_DIL), (W_DIFF + W_DIL, W_NA))

    def slab(part, si):
        off, w = slabs[si]
        acc = jnp.dot(h, w_ref[:, part * width + off: part * width + off + w], preferred_element_type=F32)
        if part < 2 and si == 0:
            acc = _rope_slab(acc, ca, sa, DIFF_QK_DIM)
        elif part < 2 and si == 1:
            acc = _rope_slab(acc, cb, sb, HEAD_DIM)
        return acc

    q_scale = (LOG2E * DIFF_QK_DIM ** -0.5, LOG2E * HEAD_DIM ** -0.5, LOG2E * HEAD_DIM ** -0.5)
    for si, (off, w) in enumerate(slabs):
        q = slab(0, si) * q_scale[si]
        qt_ref[0, off:off + w, :] = q.T.astype(BF16)
        kb = slab(1, si).astype(BF16)
        k_ref[0, :, off:off + w] = kb
        kmax_ref[0, 0, :, off:off + w] = jnp.max(jnp.abs(kb.astype(F32)), axis=0, keepdims=True)
        v = slab(2, si)
        for t in range(tm // TILE):
            vt_ref[0, t, off:off + w, :] = v[t * TILE:(t + 1) * TILE, :].T.astype(BF16)


def _rope_tables(s, group):
    half = group // 2
    inv = ROPE_THETA ** (-jnp.arange(half, dtype=F32) / half)
    ang = jnp.arange(s, dtype=F32)[:, None] * inv
    cos, sin = jnp.cos(ang), jnp.sin(ang)
    reps = LANES // group
    return (jnp.tile(jnp.concatenate([cos, cos], axis=1), (1, reps)),
            jnp.tile(jnp.concatenate([-sin, sin], axis=1), (1, reps)))


def _projection(x, mod_l, g, w_perm, tables, tm):
    bsz, s, d = x.shape
    width = W_DIFF + W_DIL + W_NA
    tab_spec = pl.BlockSpec((tm, LANES), lambda b, i: (i, 0))
    return pl.pallas_call(
        _proj_kernel,
        grid=(bsz, s // tm),
        in_specs=[
            pl.BlockSpec((1, tm, d), lambda b, i: (b, i, 0)),
            pl.BlockSpec((1, 6, d), lambda b, i: (b, 0, 0)),
            pl.BlockSpec((1, d), lambda b, i: (0, 0)),
            pl.BlockSpec((d, 3 * width), lambda b, i: (0, 0)),
            tab_spec, tab_spec, tab_spec, tab_spec,
        ],
        out_specs=[
            pl.BlockSpec((1, width, tm), lambda b, i: (b, 0, i)),
            pl.BlockSpec((1, tm, width), lambda b, i: (b, i, 0)),
            pl.BlockSpec((1, tm // TILE, width, TILE), lambda b, i: (b, i, 0, 0)),
            pl.BlockSpec((1, 1, 1, width), lambda b, i: (b, i, 0, 0)),
        ],
        out_shape=[
            jax.ShapeDtypeStruct((bsz, width, s), BF16),
            jax.ShapeDtypeStruct((bsz, s, width), BF16),
            jax.ShapeDtypeStruct((bsz, s // TILE, width, TILE), BF16),
            jax.ShapeDtypeStruct((bsz, s // tm, 1, width), F32),
        ],
        compiler_params=_cparams(("arbitrary", "arbitrary")),
        name="norm_proj_rope",
    )(x, mod_l, g.reshape(1, d), w_perm, *tables)


def _masked_queries(qt, n_split):
    row = lax.broadcasted_iota(jnp.int32, (PAIR_W, 1), 0)
    step = PAIR_W // n_split
    return [jnp.where((row >= j * step) & (row < (j + 1) * step), qt, jnp.zeros_like(qt))
            for j in range(n_split)]


def _v_with_ones(vtb, hd):
    ones = jnp.ones((SUM_ROWS, vtb.shape[1]), BF16)
    return jnp.concatenate([vtb[hd * HEAD_DIM:(hd + 1) * HEAD_DIM, :], ones], axis=0)


def _diff_kernel(lam_init, dl_ref, g_ref, kmax_ref, qt_ref, k_ref, vt_ref, o_ref, q_s, s_s, m_s, acc_s):
    tq = qt_ref.shape[2]
    n_kt = vt_ref.shape[1]
    for j, qm in enumerate(_masked_queries(qt_ref[0], 4)):
        q_s[:, j * tq:(j + 1) * tq] = qm
    lane_tiles = [slice(nt * TILE, (nt + 1) * TILE) for nt in range(4 * tq // TILE)]

    def k_tile(kt):
        return k_ref[0, pl.ds(pl.multiple_of(kt * TILE, TILE), TILE), :]

    def v_tiles(kt):
        vtb = vt_ref[0, kt]
        return [_v_with_ones(vtb, hd) for hd in range(2)]

    def exact_step(kt, first):
        kb, v_ext = k_tile(kt), v_tiles(kt)
        for ln in lane_tiles:
            s_s[:, ln] = jnp.dot(kb, q_s[:, ln], preferred_element_type=F32)
        for ln in lane_tiles:
            tile_max = jnp.max(s_s[:, ln], axis=0, keepdims=True)
            m_new = tile_max if first else jnp.maximum(m_s[:, ln], tile_max)
            e = jnp.exp2(s_s[:, ln] - m_new).astype(BF16)
            pv = jnp.dot(v_ext[ln.start // (2 * tq)], e, preferred_element_type=F32)
            acc_s[:, ln] = pv if first else acc_s[:, ln] * jnp.exp2(m_s[:, ln] - m_new) + pv
            m_s[:, ln] = m_new

    def fixed_shift_steps(kts):
        def finish(vtb, ln, s):
            hd = ln.start // (2 * tq)
            e = jnp.exp2(s - m_s[:, ln])
            acc_s[0:HEAD_DIM, ln] += jnp.dot(vtb[hd * HEAD_DIM:(hd + 1) * HEAD_DIM, :], e.astype(BF16),
                                             preferred_element_type=F32)
            acc_s[HEAD_DIM:HEAD_DIM + 1, ln] += jnp.sum(e, axis=0, keepdims=True)

        pending = []
        for kt in kts:
            kb, vtb = k_tile(kt), vt_ref[0, kt]
            for ln in lane_tiles:
                pending.append((vtb, ln, jnp.dot(kb, q_s[:, ln], preferred_element_type=F32)))
                if len(pending) > QK_AHEAD:
                    finish(*pending.pop(0))
        for item in pending:
            finish(*item)

    exact_step(0, True)

    kmax = jnp.max(kmax_ref[0], axis=0) * (1.0 + 2.0 ** -7)
    kmax = jnp.broadcast_to(kmax, (SUM_ROWS, PAIR_W)).astype(BF16)
    bound = jnp.dot(kmax, jnp.abs(q_s[...]), preferred_element_type=F32)[0:1] * (1.0 + 2.0 ** -7)
    lag_ok = jnp.max(bound - m_s[...]) < LAG_MARGIN

    @pl.when(lag_ok)
    def _():
        def body(tt, carry):
            fixed_shift_steps([LAG_UNROLL * tt + 1 + u for u in range(LAG_UNROLL)])
            return carry
        n_trips = (n_kt - 1) // LAG_UNROLL
        lax.fori_loop(0, n_trips, body, 0)
        if n_trips * LAG_UNROLL + 1 < n_kt:
            fixed_shift_steps(list(range(n_trips * LAG_UNROLL + 1, n_kt)))

    @pl.when(jnp.logical_not(lag_ok))
    def _():
        def body(kt, carry):
            exact_step(kt, False)
            return carry
        lax.fori_loop(1, n_kt, body, 0)

    dl = dl_ref[...]
    lam = (jnp.exp(jnp.sum(dl[0:1] * dl[1:2], axis=1, keepdims=True))
           - jnp.exp(jnp.sum(dl[2:3] * dl[3:4], axis=1, keepdims=True)) + lam_init)
    g = g_ref[...]
    outs = []
    for hd in range(2):
        p = [acc_s[0:HEAD_DIM, (2 * hd + mp) * tq:(2 * hd + mp + 1) * tq]
             / acc_s[HEAD_DIM:HEAD_DIM + 1, (2 * hd + mp) * tq:(2 * hd + mp + 1) * tq] for mp in range(2)]
        o = p[0] - lam * p[1]
        ms = jnp.mean(o * o, axis=0, keepdims=True)
        outs.append((o * lax.rsqrt(ms + EPS)) * g * (1.0 - lam_init))
    o_ref[0] = jnp.concatenate(outs, axis=0).T.astype(BF16)


def _diff_attention(qt, k, vt, kmax, diff_lambda, subln, layer_idx, tq):
    bsz, _, s = qt.shape
    n_pairs = N_HEADS_DIFF // 2
    lam_init = 0.8 - 0.6 * math.exp(-0.3 * layer_idx)
    return pl.pallas_call(
        functools.partial(_diff_kernel, lam_init),
        grid=(bsz, n_pairs, s // tq),
        in_specs=[
            pl.BlockSpec(diff_lambda.shape, lambda b, p, i: (0, 0)),
            pl.BlockSpec((HEAD_DIM, 1), lambda b, p, i: (0, 0)),
            pl.BlockSpec((1, kmax.shape[1], 1, PAIR_W), lambda b, p, i: (b, 0, 0, p)),
            pl.BlockSpec((1, PAIR_W, tq), lambda b, p, i: (b, p, i)),
            pl.BlockSpec((1, s, PAIR_W), lambda b, p, i: (b, 0, p)),
            pl.BlockSpec((1, s // TILE, PAIR_W, TILE), lambda b, p, i: (b, 0, p, 0)),
        ],
        out_specs=pl.BlockSpec((1, tq, PAIR_W), lambda b, p, i: (b, i, p)),
        out_shape=jax.ShapeDtypeStruct((bsz, s, W_DIFF), BF16),
        scratch_shapes=[
            pltpu.VMEM((PAIR_W, 4 * tq), BF16),
            pltpu.VMEM((TILE, 4 * tq), F32),
            pltpu.VMEM((1, 4 * tq), F32),
            pltpu.VMEM((HEAD_DIM + SUM_ROWS, 4 * tq), F32),
        ],
        compiler_params=_cparams(("arbitrary", "arbitrary", "arbitrary")),
        name="diff_attention",
    )(diff_lambda, subln.reshape(HEAD_DIM, 1), kmax, qt, k, vt)


def _band_kernel(offsets, window_bias, bmax_ref, kmax_ref, qt_ref, k_ref, vt_ref, bias_ref, o_ref,
                 q_s, sc_s, s_s, acc_s):
    i = pl.program_id(2)
    n_blk = vt_ref.shape[1]
    n_off = len(offsets)
    centre = offsets.index(0)
    n_qt = qt_ref.shape[2] // TILE
    qt = qt_ref[0]
    for t in range(n_qt):
        for hd, qm in enumerate(_masked_queries(qt[:, t * TILE:(t + 1) * TILE], 2)):
            q_s[:, (2 * t + hd) * TILE:(2 * t + hd + 1) * TILE] = qm
    lane_tiles = [slice(c * TILE, (c + 1) * TILE) for c in range(2 * n_qt)]

    def q_tile(ln):
        return i * n_qt + ln.start // (2 * TILE)

    def key_tile(oi, ln):
        return jnp.clip(q_tile(ln) + offsets[oi], 0, n_blk - 1)

    def scores(oi, ln):
        qi = q_tile(ln)
        head_lanes = slice(ln.start % (2 * TILE), ln.start % (2 * TILE) + TILE)
        if window_bias:
            variant = jnp.where(qi == 0, 0, jnp.where(qi == n_blk - 1, 2, 1))
            bias = bias_ref[variant, 0, oi * TILE:(oi + 1) * TILE, head_lanes]
        else:
            kbi = qi + offsets[oi]
            bias = bias_ref[jnp.where((kbi >= 0) & (kbi < n_blk), oi, n_off), :, head_lanes]
        kb = k_ref[0, pl.ds(pl.multiple_of(key_tile(oi, ln) * TILE, TILE), TILE), :]
        return jnp.dot(kb, q_s[:, ln], preferred_element_type=F32) + bias

    def weighted_values(oi, ln, e):
        vtb = vt_ref[0, key_tile(oi, ln)]
        return jnp.dot(_v_with_ones(vtb, (ln.start // TILE) % 2), e, preferred_element_type=F32)

    for ln in lane_tiles:
        sc_s[:, ln] = scores(centre, ln)
    m_c = jnp.max(sc_s[...], axis=0, keepdims=True)

    kmax = jnp.max(kmax_ref[0], axis=0) * (1.0 + 2.0 ** -7)
    kmax = jnp.broadcast_to(kmax, (SUM_ROWS, PAIR_W)).astype(BF16)
    bound = jnp.dot(kmax, jnp.abs(q_s[...]), preferred_element_type=F32)[0:1] * (1.0 + 2.0 ** -7) + bmax_ref[...]
    shift_ok = jnp.max(bound - m_c) < LAG_MARGIN

    @pl.when(shift_ok)
    def _():
        acc = {ln.start: None for ln in lane_tiles}

        def finish(oi, ln, s):
            pv = weighted_values(oi, ln, jnp.exp2(s - m_c[:, ln]).astype(BF16))
            acc[ln.start] = pv if acc[ln.start] is None else acc[ln.start] + pv

        pending = []
        for oi in range(n_off):
            for ln in lane_tiles:
                pending.append((oi, ln, sc_s[:, ln] if oi == centre else scores(oi, ln)))
                if len(pending) > QK_AHEAD:
                    finish(*pending.pop(0))
        for item in pending:
            finish(*item)
        for ln in lane_tiles:
            acc_s[:, ln] = acc[ln.start]

    @pl.when(jnp.logical_not(shift_ok))
    def _():
        for ln in lane_tiles:
            col_max = m_c[:, ln]
            for oi in range(n_off):
                s_s[oi] = sc_s[:, ln] if oi == centre else scores(oi, ln)
                col_max = jnp.maximum(col_max, jnp.max(s_s[oi], axis=0, keepdims=True))
            acc = None
            for oi in range(n_off):
                pv = weighted_values(oi, ln, jnp.exp2(s_s[oi] - col_max).astype(BF16))
                acc = pv if acc is None else acc + pv
            acc_s[:, ln] = acc

    out = acc_s[0:HEAD_DIM, :] / acc_s[HEAD_DIM:HEAD_DIM + 1, :]
    for t in range(n_qt):
        pair = jnp.concatenate([out[:, lane_tiles[2 * t]], out[:, lane_tiles[2 * t + 1]]], axis=0)
        o_ref[0, t * TILE:(t + 1) * TILE, :] = pair.T.astype(BF16)


def _band_attention(qt, k, vt, kmax, bias, bias_spec, bias_max, offsets, window_bias, pair0, n_pairs, name):
    bsz, _, s = qt.shape
    tq = BAND_Q_TILES * TILE
    return pl.pallas_call(
        functools.partial(_band_kernel, offsets, window_bias),
        grid=(bsz, n_pairs, s // tq),
        in_specs=[
            pl.BlockSpec((1, 1), lambda b, p, i: (0, 0)),
            pl.BlockSpec((1, kmax.shape[1], 1, PAIR_W), lambda b, p, i: (b, 0, 0, pair0 + p)),
            pl.BlockSpec((1, PAIR_W, tq), lambda b, p, i: (b, pair0 + p, i)),
            pl.BlockSpec((1, s, PAIR_W), lambda b, p, i: (b, 0, pair0 + p)),
            pl.BlockSpec((1, s // TILE, PAIR_W, TILE), lambda b, p, i: (b, 0, pair0 + p, 0)),
            bias_spec,
        ],
        out_specs=pl.BlockSpec((1, tq, PAIR_W), lambda b, p, i: (b, i, p)),
        out_shape=jax.ShapeDtypeStruct((bsz, s, n_pairs * PAIR_W), BF16),
        scratch_shapes=[
            pltpu.VMEM((PAIR_W, 2 * tq), BF16),
            pltpu.VMEM((TILE, 2 * tq), F32),
            pltpu.VMEM((len(offsets), TILE, TILE), F32),
            pltpu.VMEM((HEAD_DIM + SUM_ROWS, 2 * tq), F32),
        ],
        compiler_params=_cparams(("arbitrary", "arbitrary", "arbitrary")),
        name=name,
    )(jnp.reshape(bias_max, (1, 1)).astype(F32), kmax, qt, k, vt, bias)


def _dilated_bias():
    reach = max(w // 2 for w, _ in DIL_PATTERNS)
    n_off = -(-reach // TILE)
    offsets = tuple(range(-n_off, n_off + 1))
    kj = np.arange(TILE)[:, None]
    qi = np.arange(TILE)[None, :]
    tabs = []
    for off in offsets:
        delta = off * TILE + kj - qi
        cnt = np.zeros((TILE, TILE), np.int32)
        for window, dil in DIL_PATTERNS:
            cnt += ((delta % dil == 0) & (np.abs(delta) <= window // 2)).astype(np.int32)
        tabs.append(np.where(cnt > 0, np.log2(np.maximum(cnt, 1).astype(np.float64)), NEG))
    tabs.append(np.full((TILE, TILE), NEG))
    return offsets, jnp.asarray(np.tile(np.stack(tabs), (1, 1, 2)), F32)


def _na_bias_kernel(rows, rpb_ref, o_ref):
    hd = pl.program_id(0)
    kh = min(NA_KH, rows)
    q_rows = TILE // GRID_W
    kc = lax.broadcasted_iota(jnp.int32, (GRID_W, LANES), 0)
    lane = lax.broadcasted_iota(jnp.int32, (GRID_W, LANES), 1)
    qc = lane % GRID_W
    dc = jnp.clip(kc - qc, -(NA_KW - 1), NA_KW - 1) + (NA_KW - 1)
    cs = jnp.clip(qc - NA_KW // 2, 0, GRID_W - NA_KW)
    col_ok = (kc >= cs) & (kc < cs + NA_KW)
    neg = jnp.full((GRID_W, LANES), NEG, F32)
    tiles = []
    for dr in range(2 * NA_KH - 1):
        t = neg
        for j in range(2 * NA_KW - 1):
            t = jnp.where(dc == j, rpb_ref[hd, dr, j] * LOG2E, t)
        tiles.append(jnp.where(col_ok, t, neg))
    for v, r0 in enumerate((0, 2 * q_rows, rows - q_rows)):
        for kr_rel in range(3 * q_rows):
            kr = r0 - q_rows + kr_rel
            for pair in range(q_rows // 2):
                halves = []
                for r in (r0 + 2 * pair, r0 + 2 * pair + 1):
                    rs = min(max(r - kh // 2, 0), rows - kh)
                    halves.append(tiles[kr - r + NA_KH - 1] if rs <= kr < rs + kh else neg)
                blk = jnp.where(lane < GRID_W, halves[0], halves[1])
                o_ref[v, 0, kr_rel * GRID_W:(kr_rel + 1) * GRID_W, pair * LANES:(pair + 1) * LANES] = blk


def _na_bias(rpb, rows):
    n_heads = rpb.shape[0]
    return pl.pallas_call(
        functools.partial(_na_bias_kernel, rows),
        grid=(n_heads,),
        in_specs=[pl.BlockSpec(memory_space=pltpu.SMEM)],
        out_specs=pl.BlockSpec((3, 1, 3 * TILE, TILE), lambda h: (0, h // 2, 0, h % 2)),
        out_shape=jax.ShapeDtypeStruct((3, n_heads // 2, 3 * TILE, 2 * TILE), F32),
        compiler_params=_cparams(("arbitrary",)),
        name="na_bias_table",
    )(rpb)


def _outproj_kernel(x_ref, mod_ref, oa_ref, ob_ref, oc_ref, w_ref, y_ref):
    mix = jnp.dot(oa_ref[0], w_ref[0:W_DIFF, :], preferred_element_type=F32)
    mix += jnp.dot(ob_ref[0], w_ref[W_DIFF:W_DIFF + W_DIL, :], preferred_element_type=F32)
    mix += jnp.dot(oc_ref[0], w_ref[W_DIFF + W_DIL:, :], preferred_element_type=F32)
    y_ref[0] = x_ref[0] + mod_ref[0, 2:3, :] * mix


def _out_projection(x, mod_l, o_a, o_b, o_c, w_out, tm):
    bsz, s, d = x.shape
    row = lambda w: pl.BlockSpec((1, tm, w), lambda b, i: (b, i, 0))
    return pl.pallas_call(
        _outproj_kernel,
        grid=(bsz, s // tm),
        in_specs=[
            row(d),
            pl.BlockSpec((1, 6, d), lambda b, i: (b, 0, 0)),
            row(W_DIFF), row(W_DIL), row(W_NA),
            pl.BlockSpec(w_out.shape, lambda b, i: (0, 0)),
        ],
        out_specs=row(d),
        out_shape=jax.ShapeDtypeStruct((bsz, s, d), F32),
        compiler_params=_cparams(("arbitrary", "arbitrary")),
        name="out_proj_residual",
    )(x, mod_l, o_a, o_b, o_c, w_out)


def _ffn_up_kernel(x_ref, xp_ref, xn_ref, mod_ref, g_ref, wg_ref, wu_ref, cw_ref, cb_ref,
                   a_ref, h_s, g_s):
    i = pl.program_id(1)
    tm = x_ref.shape[1]
    g, sc, sh = g_ref[...], mod_ref[0, 4:5, :], mod_ref[0, 3:4, :]
    keep_prev = (i > 0).astype(F32)
    keep_next = (i < pl.num_programs(1) - 1).astype(F32)
    h_s[0:HALO, :] = (_norm_mod(xp_ref[0], g, sc, sh) * keep_prev).astype(BF16)
    h_s[HALO:HALO + tm, :] = _norm_mod(x_ref[0], g, sc, sh).astype(BF16)
    h_s[HALO + tm:, :] = (_norm_mod(xn_ref[0], g, sc, sh) * keep_next).astype(BF16)
    for c in range(wg_ref.shape[0]):
        g_s[...] = jnp.dot(h_s[...], wg_ref[c], preferred_element_type=F32)
        cw = cw_ref[c]
        gc = cb_ref[c]
        for t in range(CONV_W):
            gc = gc + g_s[pl.ds(HALO - CONV_W // 2 + t, tm), :] * cw[t:t + 1, :]
        u = jnp.dot(h_s[HALO:HALO + tm, :], wu_ref[c], preferred_element_type=F32)
        a_ref[0, :, c * FF_CHUNK:(c + 1) * FF_CHUNK] = (gc * jax.nn.sigmoid(gc) * u).astype(BF16)


def _ffn_up(x, mod_l, g, wg, wu, cw, cb, tm):
    bsz, s, d = x.shape
    n_ch = wg.shape[0]
    hb = tm // HALO
    n_hb = s // HALO
    return pl.pallas_call(
        _ffn_up_kernel,
        grid=(bsz, s // tm),
        in_specs=[
            pl.BlockSpec((1, tm, d), lambda b, i: (b, i, 0)),
            pl.BlockSpec((1, HALO, d), lambda b, i: (b, jnp.maximum(i * hb - 1, 0), 0)),
            pl.BlockSpec((1, HALO, d), lambda b, i: (b, jnp.minimum((i + 1) * hb, n_hb - 1), 0)),
            pl.BlockSpec((1, 6, d), lambda b, i: (b, 0, 0)),
            pl.BlockSpec((1, d), lambda b, i: (0, 0)),
            pl.BlockSpec(wg.shape, lambda b, i: (0, 0, 0)),
            pl.BlockSpec(wu.shape, lambda b, i: (0, 0, 0)),
            pl.BlockSpec(cw.shape, lambda b, i: (0, 0, 0)),
            pl.BlockSpec(cb.shape, lambda b, i: (0, 0, 0)),
        ],
        out_specs=pl.BlockSpec((1, tm, n_ch * FF_CHUNK), lambda b, i: (b, i, 0)),
        out_shape=jax.ShapeDtypeStruct((bsz, s, n_ch * FF_CHUNK), BF16),
        scratch_shapes=[
            pltpu.VMEM((tm + 2 * HALO, d), BF16),
            pltpu.VMEM((tm + 2 * HALO, FF_CHUNK), F32),
        ],
        compiler_params=_cparams(("arbitrary", "arbitrary")),
        name="ffn_up_conv_glu",
    )(x, x, x, mod_l, g.reshape(1, d), wg, wu, cw, cb)


def _ffn_down_kernel(final, x_ref, mod_ref, a_ref, w_ref, gf_ref, y_ref):
    y = x_ref[0] + mod_ref[0, 5:6, :] * jnp.dot(a_ref[0], w_ref[...], preferred_element_type=F32)
    if final:
        ms = jnp.mean(y * y, axis=-1, keepdims=True)
        y = (y * lax.rsqrt(ms + EPS)) * gf_ref[...]
    y_ref[0] = y


def _ffn_down(x, mod_l, a, w_down, g_final, final, tm):
    bsz, s, d = x.shape
    return pl.pallas_call(
        functools.partial(_ffn_down_kernel, final),
        grid=(bsz, s // tm),
        in_specs=[
            pl.BlockSpec((1, tm, d), lambda b, i: (b, i, 0)),
            pl.BlockSpec((1, 6, d), lambda b, i: (b, 0, 0)),
            pl.BlockSpec((1, tm, a.shape[2]), lambda b, i: (b, i, 0)),
            pl.BlockSpec(w_down.shape, lambda b, i: (0, 0)),
            pl.BlockSpec((1, d), lambda b, i: (0, 0)),
        ],
        out_specs=pl.BlockSpec((1, tm, d), lambda b, i: (b, i, 0)),
        out_shape=jax.ShapeDtypeStruct((bsz, s, d), F32),
        compiler_params=_cparams(("arbitrary", "arbitrary")),
        name="ffn_down_residual",
    )(x, mod_l, a, w_down, g_final.reshape(1, d))


def kernel(x, c, w_ada, b_ada, g_attn, w_in, diff_lambda, diff_subln, na_rpb, w_out, g_ffn, w_up,
           conv_w, conv_b, w_down, g_final):
    bsz, s, d = x.shape
    depth = w_ada.shape[0]
    d_ff = w_down.shape[1]
    n_ch = d_ff // FF_CHUNK
    tm = 512
    assert s % tm == 0 and tm % TILE == 0 and d_ff % FF_CHUNK == 0 and TILE % (2 * GRID_W) == 0
    assert (s // TILE) % 2 == 0 and s // GRID_W >= 4 * (TILE // GRID_W)

    mod = _modulation(c, w_ada, b_ada).reshape(depth, bsz, 6, d)
    tables = _rope_tables(s, DIFF_QK_DIM) + _rope_tables(s, HEAD_DIM)
    dil_offsets, dil_bias = _dilated_bias()
    rows = s // GRID_W

    wa, wb, wc = W_DIFF, W_DIL, W_NA
    offs = np.cumsum([0, wa, wa, wa, wb, wb, wb, wc, wc, wc])
    order = (0, 3, 6, 1, 4, 7, 2, 5, 8)

    for l in range(depth):
        w_perm = jnp.concatenate([w_in[l][:, offs[j]:offs[j + 1]] for j in order], axis=1).astype(BF16)
        qt, k, vt, kmax = _projection(x, mod[l], g_attn[l], w_perm, tables, tm)

        o_a = _diff_attention(qt, k, vt, kmax, diff_lambda[l], diff_subln[l], l, 2 * TILE)
        o_b = _band_attention(
            qt, k, vt, kmax, dil_bias,
            pl.BlockSpec(dil_bias.shape, lambda b, p, i: (0, 0, 0)),
            jnp.float32(math.log2(len(DIL_PATTERNS))),
            dil_offsets, False, N_HEADS_DIFF // 2, N_HEADS_DIL // 2, "dilated_attention")
        o_c = _band_attention(
            qt, k, vt, kmax, _na_bias(na_rpb[l], rows),
            pl.BlockSpec((3, 1, 3 * TILE, 2 * TILE), lambda b, p, i: (0, p, 0, 0)),
            jnp.max(na_rpb[l]) * LOG2E,
            (-1, 0, 1), True, (N_HEADS_DIFF + N_HEADS_DIL) // 2, N_HEADS_NA // 2, "neighbourhood_attention")

        x = _out_projection(x, mod[l], o_a, o_b, o_c, w_out[l].astype(BF16), tm)

        wg = w_up[l][:, :d_ff].reshape(d, n_ch, FF_CHUNK).transpose(1, 0, 2).astype(BF16)
        wu = w_up[l][:, d_ff:].reshape(d, n_ch, FF_CHUNK).transpose(1, 0, 2).astype(BF16)
        cw = conv_w[l].reshape(CONV_W, n_ch, FF_CHUNK).transpose(1, 0, 2)
        cb = conv_b[l].reshape(n_ch, 1, FF_CHUNK)
        a = _ffn_up(x, mod[l], g_ffn[l], wg, wu, cw, cb, tm)
        x = _ffn_down(x, mod[l], a, w_down[l].astype(BF16), g_final, l == depth - 1, tm)
    return x
```

```python
import functools
import math

import numpy as np
import jax
import jax.numpy as jnp
from jax import lax
from jax.experimental import pallas as pl
from jax.experimental.pallas import tpu as pltpu

F32 = jnp.float32
BF16 = jnp.bfloat16

HEAD_DIM = 64
N_HEADS_DIFF = 4
N_HEADS_DIL = 6
N_HEADS_NA = 6
W_DIFF = N_HEADS_DIFF * HEAD_DIM
W_DIL = N_HEADS_DIL * HEAD_DIM
W_NA = N_HEADS_NA * HEAD_DIM
DIFF_QK_DIM = HEAD_DIM // 2
DIL_PATTERNS = ((128, 1), (512, 4), (2048, 16))
GRID_W = 64
NA_KH = 8
NA_KW = 16
CONV_W = 3
ROPE_THETA = 10000.0
EPS = 1e-6
NEG = -1e30

LANES = 128
PAIR_W = 2 * HEAD_DIM
TILE = 256
FF_CHUNK = 256
HALO = 16
VMEM_LIMIT = 56 * 1024 * 1024
LOG2E = math.log2(math.e)
SUM_ROWS = 16
LAG_MARGIN = 64.0
DIL_Q_TILES = 4
NA_Q_TILES = 8
LAG_UNROLL = 7
QK_AHEAD = 4


def _cparams(sem):
    return pltpu.CompilerParams(dimension_semantics=sem, vmem_limit_bytes=VMEM_LIMIT)


def _mod_kernel(ct_ref, w_ref, b_ref, o_ref):
    ct = ct_ref[...]
    s = ct * jax.nn.sigmoid(ct)
    w = w_ref[0]
    rows = [jnp.sum(w * s[:, b:b + 1], axis=0, keepdims=True) for b in range(ct.shape[1])]
    o_ref[0] = jnp.concatenate(rows, axis=0) + b_ref[0]


def _modulation(c, w_ada, b_ada):
    depth, d, n = w_ada.shape
    bsz = c.shape[0]
    tn = 768
    return pl.pallas_call(
        _mod_kernel,
        grid=(depth, n // tn),
        in_specs=[
            pl.BlockSpec((d, bsz), lambda l, j: (0, 0)),
            pl.BlockSpec((1, d, tn), lambda l, j: (l, 0, j)),
            pl.BlockSpec((1, 1, tn), lambda l, j: (l, 0, j)),
        ],
        out_specs=pl.BlockSpec((1, bsz, tn), lambda l, j: (l, 0, j)),
        out_shape=jax.ShapeDtypeStruct((depth, bsz, n), F32),
        compiler_params=_cparams(("arbitrary", "arbitrary")),
        name="adaln_mod",
    )(c.T, w_ada, b_ada.reshape(depth, 1, n))


def _norm_mod(x, g, sc, sh):
    ms = jnp.mean(x * x, axis=-1, keepdims=True)
    return (x * lax.rsqrt(ms + EPS)) * g * (1.0 + sc) + sh


def _rope_slab(acc, cos, sin, group, width):
    half = group // 2
    lane = lax.broadcasted_iota(jnp.int32, (1, LANES), 1) % group
    first = lane < half
    outs = []
    for c in range(width // LANES):
        xc = acc[:, c * LANES:(c + 1) * LANES]
        swapped = jnp.where(first, pltpu.roll(xc, LANES - half, axis=1), pltpu.roll(xc, half, axis=1))
        outs.append(xc * cos + swapped * sin)
    if width < acc.shape[1]:
        outs.append(acc[:, width:])
    return jnp.concatenate(outs, axis=1)


def _proj_kernel(x_ref, mod_ref, g_ref, w_ref, ca_ref, sa_ref, cb_ref, sb_ref,
                 qt_ref, k_ref, vt_ref, kmax_ref):
    tm = x_ref.shape[1]
    h = _norm_mod(x_ref[0], g_ref[...], mod_ref[0, 1:2, :], mod_ref[0, 0:1, :]).astype(BF16)
    ca, sa, cb, sb = ca_ref[...], sa_ref[...], cb_ref[...], sb_ref[...]
    width = W_DIFF + W_DIL + W_NA
    slabs = ((0, W_DIFF), (W_DIFF, W_DIL + W_NA))

    def slab(part, si):
        off, w = slabs[si]
        acc = jnp.dot(h, w_ref[:, part * width + off: part * width + off + w], preferred_element_type=F32)
        if part < 2 and si == 0:
            acc = _rope_slab(acc, ca, sa, DIFF_QK_DIM, W_DIFF)
        elif part < 2 and si == 1:
            acc = _rope_slab(acc, cb, sb, HEAD_DIM, W_DIL)
        return acc

    q_scale = (LOG2E * DIFF_QK_DIM ** -0.5, LOG2E * HEAD_DIM ** -0.5)
    for si, (off, w) in enumerate(slabs):
        q = slab(0, si) * q_scale[si]
        qt_ref[0, off:off + w, :] = q.T.astype(BF16)
        kb = slab(1, si).astype(BF16)
        k_ref[0, :, off:off + w] = kb
        kmax_ref[0, 0, :, off:off + w] = jnp.max(jnp.abs(kb.astype(F32)), axis=0, keepdims=True)
        v = slab(2, si)
        for t in range(tm // TILE):
            vt_ref[0, t, off:off + w, :] = v[t * TILE:(t + 1) * TILE, :].T.astype(BF16)


def _rope_tables(s, group):
    half = group // 2
    inv = ROPE_THETA ** (-jnp.arange(half, dtype=F32) / half)
    ang = jnp.arange(s, dtype=F32)[:, None] * inv
    cos, sin = jnp.cos(ang), jnp.sin(ang)
    reps = LANES // group
    return (jnp.tile(jnp.concatenate([cos, cos], axis=1), (1, reps)),
            jnp.tile(jnp.concatenate([-sin, sin], axis=1), (1, reps)))


def _projection(x, mod_l, g, w_perm, tables, tm):
    bsz, s, d = x.shape
    width = W_DIFF + W_DIL + W_NA
    tab_spec = pl.BlockSpec((tm, LANES), lambda b, i: (i, 0))
    return pl.pallas_call(
        _proj_kernel,
        grid=(bsz, s // tm),
        in_specs=[
            pl.BlockSpec((1, tm, d), lambda b, i: (b, i, 0)),
            pl.BlockSpec((1, 6, d), lambda b, i: (b, 0, 0)),
            pl.BlockSpec((1, d), lambda b, i: (0, 0)),
            pl.BlockSpec((d, 3 * width), lambda b, i: (0, 0)),
            tab_spec, tab_spec, tab_spec, tab_spec,
        ],
        out_specs=[
            pl.BlockSpec((1, width, tm), lambda b, i: (b, 0, i)),
            pl.BlockSpec((1, tm, width), lambda b, i: (b, i, 0)),
            pl.BlockSpec((1, tm // TILE, width, TILE), lambda b, i: (b, i, 0, 0)),
            pl.BlockSpec((1, 1, 1, width), lambda b, i: (b, i, 0, 0)),
        ],
        out_shape=[
            jax.ShapeDtypeStruct((bsz, width, s), BF16),
            jax.ShapeDtypeStruct((bsz, s, width), BF16),
            jax.ShapeDtypeStruct((bsz, s // TILE, width, TILE), BF16),
            jax.ShapeDtypeStruct((bsz, s // tm, 1, width), F32),
        ],
        compiler_params=_cparams(("arbitrary", "arbitrary")),
        name="norm_proj_rope",
    )(x, mod_l, g.reshape(1, d), w_perm, *tables)


def _masked_queries(qt, n_split):
    row = lax.broadcasted_iota(jnp.int32, (PAIR_W, 1), 0)
    step = PAIR_W // n_split
    return [jnp.where((row >= j * step) & (row < (j + 1) * step), qt, jnp.zeros_like(qt))
            for j in range(n_split)]


def _v_with_ones(vtb, hd):
    ones = jnp.ones((SUM_ROWS, vtb.shape[1]), BF16)
    return jnp.concatenate([vtb[hd * HEAD_DIM:(hd + 1) * HEAD_DIM, :], ones], axis=0)


def _score_bound(kmax_ref, q_s):
    kmax = jnp.max(kmax_ref[0], axis=0) * (1.0 + 2.0 ** -7)
    kmax = jnp.broadcast_to(kmax, (SUM_ROWS, PAIR_W)).astype(BF16)
    return jnp.dot(kmax, jnp.abs(q_s[...]), preferred_element_type=F32)[0:1] * (1.0 + 2.0 ** -7)


def _diff_kernel(lam_init, dl_ref, g_ref, kmax_ref, qt_ref, k_ref, vt_ref, o_ref, q_s, s_s, m_s, acc_s):
    tq = qt_ref.shape[2]
    n_kt = vt_ref.shape[1]
    for j, qm in enumerate(_masked_queries(qt_ref[0], 4)):
        q_s[:, j * tq:(j + 1) * tq] = qm
    lane_tiles = [slice(nt * TILE, (nt + 1) * TILE) for nt in range(4 * tq // TILE)]

    def k_tile(kt):
        return k_ref[0, pl.ds(pl.multiple_of(kt * TILE, TILE), TILE), :]

    def v_tiles(kt):
        vtb = vt_ref[0, kt]
        return [_v_with_ones(vtb, hd) for hd in range(2)]

    def exact_step(kt, first):
        kb, v_ext = k_tile(kt), v_tiles(kt)
        for ln in lane_tiles:
            s_s[:, ln] = jnp.dot(kb, q_s[:, ln], preferred_element_type=F32)
        for ln in lane_tiles:
            tile_max = jnp.max(s_s[:, ln], axis=0, keepdims=True)
            m_new = tile_max if first else jnp.maximum(m_s[:, ln], tile_max)
            e = jnp.exp2(s_s[:, ln] - m_new).astype(BF16)
            pv = jnp.dot(v_ext[ln.start // (2 * tq)], e, preferred_element_type=F32)
            acc_s[:, ln] = pv if first else acc_s[:, ln] * jnp.exp2(m_s[:, ln] - m_new) + pv
            m_s[:, ln] = m_new

    def fixed_shift_steps(kts):
        def finish(vtb, ln, s):
            hd = ln.start // (2 * tq)
            e = jnp.exp2(s - m_s[:, ln])
            acc_s[0:HEAD_DIM, ln] += jnp.dot(vtb[hd * HEAD_DIM:(hd + 1) * HEAD_DIM, :], e.astype(BF16),
                                             preferred_element_type=F32)
            acc_s[HEAD_DIM:HEAD_DIM + 1, ln] += jnp.sum(e, axis=0, keepdims=True)

        pending = []
        for kt in kts:
            kb, vtb = k_tile(kt), vt_ref[0, kt]
            for ln in lane_tiles:
                pending.append((vtb, ln, jnp.dot(kb, q_s[:, ln], preferred_element_type=F32)))
                if len(pending) > QK_AHEAD:
                    finish(*pending.pop(0))
        for item in pending:
            finish(*item)

    exact_step(0, True)
    lag_ok = jnp.max(_score_bound(kmax_ref, q_s) - m_s[...]) < LAG_MARGIN

    @pl.when(lag_ok)
    def _():
        def body(tt, carry):
            fixed_shift_steps([LAG_UNROLL * tt + 1 + u for u in range(LAG_UNROLL)])
            return carry
        n_trips = (n_kt - 1) // LAG_UNROLL
        lax.fori_loop(0, n_trips, body, 0)
        if n_trips * LAG_UNROLL + 1 < n_kt:
            fixed_shift_steps(list(range(n_trips * LAG_UNROLL + 1, n_kt)))

    @pl.when(jnp.logical_not(lag_ok))
    def _():
        def body(kt, carry):
            exact_step(kt, False)
            return carry
        lax.fori_loop(1, n_kt, body, 0)

    dl = dl_ref[...]
    lam = (jnp.exp(jnp.sum(dl[0:1] * dl[1:2], axis=1, keepdims=True))
           - jnp.exp(jnp.sum(dl[2:3] * dl[3:4], axis=1, keepdims=True)) + lam_init)
    g = g_ref[...]
    outs = []
    for hd in range(2):
        p = [acc_s[0:HEAD_DIM, (2 * hd + mp) * tq:(2 * hd + mp + 1) * tq]
             / acc_s[HEAD_DIM:HEAD_DIM + 1, (2 * hd + mp) * tq:(2 * hd + mp + 1) * tq] for mp in range(2)]
        o = p[0] - lam * p[1]
        ms = jnp.mean(o * o, axis=0, keepdims=True)
        outs.append((o * lax.rsqrt(ms + EPS)) * g * (1.0 - lam_init))
    o_ref[0] = jnp.concatenate(outs, axis=0).T.astype(BF16)


def _diff_attention(qt, k, vt, kmax, diff_lambda, subln, layer_idx, tq):
    bsz, _, s = qt.shape
    n_pairs = N_HEADS_DIFF // 2
    lam_init = 0.8 - 0.6 * math.exp(-0.3 * layer_idx)
    return pl.pallas_call(
        functools.partial(_diff_kernel, lam_init),
        grid=(bsz, n_pairs, s // tq),
        in_specs=[
            pl.BlockSpec(diff_lambda.shape, lambda b, p, i: (0, 0)),
            pl.BlockSpec((HEAD_DIM, 1), lambda b, p, i: (0, 0)),
            pl.BlockSpec((1, kmax.shape[1], 1, PAIR_W), lambda b, p, i: (b, 0, 0, p)),
            pl.BlockSpec((1, PAIR_W, tq), lambda b, p, i: (b, p, i)),
            pl.BlockSpec((1, s, PAIR_W), lambda b, p, i: (b, 0, p)),
            pl.BlockSpec((1, s // TILE, PAIR_W, TILE), lambda b, p, i: (b, 0, p, 0)),
        ],
        out_specs=pl.BlockSpec((1, tq, PAIR_W), lambda b, p, i: (b, i, p)),
        out_shape=jax.ShapeDtypeStruct((bsz, s, W_DIFF), BF16),
        scratch_shapes=[
            pltpu.VMEM((PAIR_W, 4 * tq), BF16),
            pltpu.VMEM((TILE, 4 * tq), F32),
            pltpu.VMEM((1, 4 * tq), F32),
            pltpu.VMEM((HEAD_DIM + SUM_ROWS, 4 * tq), F32),
        ],
        compiler_params=_cparams(("arbitrary", "arbitrary", "arbitrary")),
        name="diff_attention",
    )(diff_lambda, subln.reshape(HEAD_DIM, 1), kmax, qt, k, vt)


def _band_kernel(offsets, window_bias, bmax_ref, kmax_ref, qt_ref, k_ref, vt_ref, bias_ref, o_ref,
                 q_s, sc_s, s_s, acc_s):
    i = pl.program_id(2)
    n_blk = vt_ref.shape[1]
    n_off = len(offsets)
    centre = offsets.index(0)
    n_qt = qt_ref.shape[2] // TILE
    qt = qt_ref[0]
    for t in range(n_qt):
        for hd, qm in enumerate(_masked_queries(qt[:, t * TILE:(t + 1) * TILE], 2)):
            q_s[:, (2 * t + hd) * TILE:(2 * t + hd + 1) * TILE] = qm
    lane_tiles = [slice(c * TILE, (c + 1) * TILE) for c in range(2 * n_qt)]

    def q_tile(ln):
        return i * n_qt + ln.start // (2 * TILE)

    def key_tile(oi, ln):
        return jnp.clip(q_tile(ln) + offsets[oi], 0, n_blk - 1)

    def scores(oi, ln):
        qi = q_tile(ln)
        head_lanes = slice(ln.start % (2 * TILE), ln.start % (2 * TILE) + TILE)
        if window_bias:
            variant = jnp.where(qi == 0, 0, jnp.where(qi == n_blk - 1, 2, 1))
            bias = bias_ref[variant, 0, oi * TILE:(oi + 1) * TILE, head_lanes]
        else:
            kbi = qi + offsets[oi]
            bias = bias_ref[jnp.where((kbi >= 0) & (kbi < n_blk), oi, n_off), :, head_lanes]
        kb = k_ref[0, pl.ds(pl.multiple_of(key_tile(oi, ln) * TILE, TILE), TILE), :]
        return jnp.dot(kb, q_s[:, ln], preferred_element_type=F32) + bias

    def weighted_values(oi, ln, e):
        vtb = vt_ref[0, key_tile(oi, ln)]
        return jnp.dot(_v_with_ones(vtb, (ln.start // TILE) % 2), e, preferred_element_type=F32)

    for ln in lane_tiles:
        sc_s[:, ln] = scores(centre, ln)
    m_c = jnp.max(sc_s[...], axis=0, keepdims=True)
    shift_ok = jnp.max(_score_bound(kmax_ref, q_s) + bmax_ref[...] - m_c) < LAG_MARGIN

    @pl.when(shift_ok)
    def _():
        acc = {ln.start: None for ln in lane_tiles}

        def finish(oi, ln, s):
            pv = weighted_values(oi, ln, jnp.exp2(s - m_c[:, ln]).astype(BF16))
            acc[ln.start] = pv if acc[ln.start] is None else acc[ln.start] + pv

        pending = []
        for oi in range(n_off):
            for ln in lane_tiles:
                pending.append((oi, ln, sc_s[:, ln] if oi == centre else scores(oi, ln)))
                if len(pending) > QK_AHEAD:
                    finish(*pending.pop(0))
        for item in pending:
            finish(*item)
        for ln in lane_tiles:
            acc_s[:, ln] = acc[ln.start]

    @pl.when(jnp.logical_not(shift_ok))
    def _():
        for ln in lane_tiles:
            col_max = m_c[:, ln]
            for oi in range(n_off):
                s_s[oi] = sc_s[:, ln] if oi == centre else scores(oi, ln)
                col_max = jnp.maximum(col_max, jnp.max(s_s[oi], axis=0, keepdims=True))
            acc = None
            for oi in range(n_off):
                pv = weighted_values(oi, ln, jnp.exp2(s_s[oi] - col_max).astype(BF16))
                acc = pv if acc is None else acc + pv
            acc_s[:, ln] = acc

    out = acc_s[0:HEAD_DIM, :] / acc_s[HEAD_DIM:HEAD_DIM + 1, :]
    for t in range(n_qt):
        pair = jnp.concatenate([out[:, lane_tiles[2 * t]], out[:, lane_tiles[2 * t + 1]]], axis=0)
        o_ref[0, t * TILE:(t + 1) * TILE, :] = pair.T.astype(BF16)


def _band_attention(qt, k, vt, kmax, bias, bias_spec, bias_max, offsets, window_bias, q_tiles, pair0, n_pairs,
                    name):
    bsz, _, s = qt.shape
    tq = q_tiles * TILE
    return pl.pallas_call(
        functools.partial(_band_kernel, offsets, window_bias),
        grid=(bsz, n_pairs, s // tq),
        in_specs=[
            pl.BlockSpec((1, 1), lambda b, p, i: (0, 0)),
            pl.BlockSpec((1, kmax.shape[1], 1, PAIR_W), lambda b, p, i: (b, 0, 0, pair0 + p)),
            pl.BlockSpec((1, PAIR_W, tq), lambda b, p, i: (b, pair0 + p, i)),
            pl.BlockSpec((1, s, PAIR_W), lambda b, p, i: (b, 0, pair0 + p)),
            pl.BlockSpec((1, s // TILE, PAIR_W, TILE), lambda b, p, i: (b, 0, pair0 + p, 0)),
            bias_spec,
        ],
        out_specs=pl.BlockSpec((1, tq, PAIR_W), lambda b, p, i: (b, i, p)),
        out_shape=jax.ShapeDtypeStruct((bsz, s, n_pairs * PAIR_W), BF16),
        scratch_shapes=[
            pltpu.VMEM((PAIR_W, 2 * tq), BF16),
            pltpu.VMEM((TILE, 2 * tq), F32),
            pltpu.VMEM((len(offsets), TILE, TILE), F32),
            pltpu.VMEM((HEAD_DIM + SUM_ROWS, 2 * tq), F32),
        ],
        compiler_params=_cparams(("arbitrary", "arbitrary", "arbitrary")),
        name=name,
    )(jnp.reshape(bias_max, (1, 1)).astype(F32), kmax, qt, k, vt, bias)


def _dilated_bias():
    reach = max(w // 2 for w, _ in DIL_PATTERNS)
    n_off = -(-reach // TILE)
    offsets = tuple(range(-n_off, n_off + 1))
    kj = np.arange(TILE)[:, None]
    qi = np.arange(TILE)[None, :]
    tabs = []
    for off in offsets:
        delta = off * TILE + kj - qi
        cnt = np.zeros((TILE, TILE), np.int32)
        for window, dil in DIL_PATTERNS:
            cnt += ((delta % dil == 0) & (np.abs(delta) <= window // 2)).astype(np.int32)
        tabs.append(np.where(cnt > 0, np.log2(np.maximum(cnt, 1).astype(np.float64)), NEG))
    tabs.append(np.full((TILE, TILE), NEG))
    return offsets, jnp.asarray(np.tile(np.stack(tabs), (1, 1, 2)), F32)


def _na_bias_kernel(rows, rpb_ref, o_ref):
    hd = pl.program_id(0)
    kh = min(NA_KH, rows)
    q_rows = TILE // GRID_W
    kc = lax.broadcasted_iota(jnp.int32, (GRID_W, LANES), 0)
    lane = lax.broadcasted_iota(jnp.int32, (GRID_W, LANES), 1)
    qc = lane % GRID_W
    dc = jnp.clip(kc - qc, -(NA_KW - 1), NA_KW - 1) + (NA_KW - 1)
    cs = jnp.clip(qc - NA_KW // 2, 0, GRID_W - NA_KW)
    col_ok = (kc >= cs) & (kc < cs + NA_KW)
    neg = jnp.full((GRID_W, LANES), NEG, F32)
    tiles = []
    for dr in range(2 * NA_KH - 1):
        t = neg
        for j in range(2 * NA_KW - 1):
            t = jnp.where(dc == j, rpb_ref[hd, dr, j] * LOG2E, t)
        tiles.append(jnp.where(col_ok, t, neg))
    for v, r0 in enumerate((0, 2 * q_rows, rows - q_rows)):
        for kr_rel in range(3 * q_rows):
            kr = r0 - q_rows + kr_rel
            for pair in range(q_rows // 2):
                halves = []
                for r in (r0 + 2 * pair, r0 + 2 * pair + 1):
                    rs = min(max(r - kh // 2, 0), rows - kh)
                    halves.append(tiles[kr - r + NA_KH - 1] if rs <= kr < rs + kh else neg)
                blk = jnp.where(lane < GRID_W, halves[0], halves[1])
                o_ref[v, 0, kr_rel * GRID_W:(kr_rel + 1) * GRID_W, pair * LANES:(pair + 1) * LANES] = blk


def _na_bias(rpb, rows):
    n_heads = rpb.shape[0]
    return pl.pallas_call(
        functools.partial(_na_bias_kernel, rows),
        grid=(n_heads,),
        in_specs=[pl.BlockSpec(memory_space=pltpu.SMEM)],
        out_specs=pl.BlockSpec((3, 1, 3 * TILE, TILE), lambda h: (0, h // 2, 0, h % 2)),
        out_shape=jax.ShapeDtypeStruct((3, n_heads // 2, 3 * TILE, 2 * TILE), F32),
        compiler_params=_cparams(("arbitrary",)),
        name="na_bias_table",
    )(rpb)


def _outproj_kernel(x_ref, mod_ref, oa_ref, ob_ref, oc_ref, w_ref, y_ref):
    o = jnp.concatenate([oa_ref[0], ob_ref[0], oc_ref[0]], axis=1)
    mix = jnp.dot(o, w_ref[...], preferred_element_type=F32)
    y_ref[0] = x_ref[0] + mod_ref[0, 2:3, :] * mix


def _out_projection(x, mod_l, o_a, o_b, o_c, w_out, tm):
    bsz, s, d = x.shape
    row = lambda w: pl.BlockSpec((1, tm, w), lambda b, i: (b, i, 0))
    return pl.pallas_call(
        _outproj_kernel,
        grid=(bsz, s // tm),
        in_specs=[
            row(d),
            pl.BlockSpec((1, 6, d), lambda b, i: (b, 0, 0)),
            row(W_DIFF), row(W_DIL), row(W_NA),
            pl.BlockSpec(w_out.shape, lambda b, i: (0, 0)),
        ],
        out_specs=row(d),
        out_shape=jax.ShapeDtypeStruct((bsz, s, d), F32),
        compiler_params=_cparams(("arbitrary", "arbitrary")),
        name="out_proj_residual",
    )(x, mod_l, o_a, o_b, o_c, w_out)


def _ffn_up_kernel(x_ref, xp_ref, xn_ref, mod_ref, g_ref, wg_ref, wu_ref, cw_ref, cb_ref,
                   a_ref, h_s, g_s, u_s):
    i = pl.program_id(1)
    tm = x_ref.shape[1]
    g, sc, sh = g_ref[...], mod_ref[0, 4:5, :], mod_ref[0, 3:4, :]
    keep_prev = (i > 0).astype(F32)
    keep_next = (i < pl.num_programs(1) - 1).astype(F32)
    h_s[0:HALO, :] = (_norm_mod(xp_ref[0], g, sc, sh) * keep_prev).astype(BF16)
    h_s[HALO:HALO + tm, :] = _norm_mod(x_ref[0], g, sc, sh).astype(BF16)
    h_s[HALO + tm:, :] = (_norm_mod(xn_ref[0], g, sc, sh) * keep_next).astype(BF16)
    n_ch = wg_ref.shape[0]

    def matmuls(c):
        g_s[c % 2] = jnp.dot(h_s[...], wg_ref[c], preferred_element_type=F32)
        u_s[c % 2] = jnp.dot(h_s[HALO:HALO + tm, :], wu_ref[c], preferred_element_type=F32)

    def gate(c):
        cw = cw_ref[c]
        gc = cb_ref[c]
        for t in range(CONV_W):
            gc = gc + g_s[c % 2, pl.ds(HALO - CONV_W // 2 + t, tm), :] * cw[t:t + 1, :]
        a_ref[0, :, c * FF_CHUNK:(c + 1) * FF_CHUNK] = (gc * jax.nn.sigmoid(gc) * u_s[c % 2]).astype(BF16)

    matmuls(0)
    for c in range(n_ch):
        if c + 1 < n_ch:
            matmuls(c + 1)
        gate(c)


def _ffn_up(x, mod_l, g, wg, wu, cw, cb, tm):
    bsz, s, d = x.shape
    n_ch = wg.shape[0]
    hb = tm // HALO
    n_hb = s // HALO
    return pl.pallas_call(
        _ffn_up_kernel,
        grid=(bsz, s // tm),
        in_specs=[
            pl.BlockSpec((1, tm, d), lambda b, i: (b, i, 0)),
            pl.BlockSpec((1, HALO, d), lambda b, i: (b, jnp.maximum(i * hb - 1, 0), 0)),
            pl.BlockSpec((1, HALO, d), lambda b, i: (b, jnp.minimum((i + 1) * hb, n_hb - 1), 0)),
            pl.BlockSpec((1, 6, d), lambda b, i: (b, 0, 0)),
            pl.BlockSpec((1, d), lambda b, i: (0, 0)),
            pl.BlockSpec(wg.shape, lambda b, i: (0, 0, 0)),
            pl.BlockSpec(wu.shape, lambda b, i: (0, 0, 0)),
            pl.BlockSpec(cw.shape, lambda b, i: (0, 0, 0)),
            pl.BlockSpec(cb.shape, lambda b, i: (0, 0, 0)),
        ],
        out_specs=pl.BlockSpec((1, tm, n_ch * FF_CHUNK), lambda b, i: (b, i, 0)),
        out_shape=jax.ShapeDtypeStruct((bsz, s, n_ch * FF_CHUNK), BF16),
        scratch_shapes=[
            pltpu.VMEM((tm + 2 * HALO, d), BF16),
            pltpu.VMEM((2, tm + 2 * HALO, FF_CHUNK), F32),
            pltpu.VMEM((2, tm, FF_CHUNK), F32),
        ],
        compiler_params=_cparams(("arbitrary", "arbitrary")),
        name="ffn_up_conv_glu",
    )(x, x, x, mod_l, g.reshape(1, d), wg, wu, cw, cb)


def _ffn_down_kernel(final, x_ref, mod_ref, a_ref, w_ref, gf_ref, y_ref):
    y = x_ref[0] + mod_ref[0, 5:6, :] * jnp.dot(a_ref[0], w_ref[...], preferred_element_type=F32)
    if final:
        ms = jnp.mean(y * y, axis=-1, keepdims=True)
        y = (y * lax.rsqrt(ms + EPS)) * gf_ref[...]
    y_ref[0] = y


def _ffn_down(x, mod_l, a, w_down, g_final, final, tm):
    bsz, s, d = x.shape
    return pl.pallas_call(
        functools.partial(_ffn_down_kernel, final),
        grid=(bsz, s // tm),
        in_specs=[
            pl.BlockSpec((1, tm, d), lambda b, i: (b, i, 0)),
            pl.BlockSpec((1, 6, d), lambda b, i: (b, 0, 0)),
            pl.BlockSpec((1, tm, a.shape[2]), lambda b, i: (b, i, 0)),
            pl.BlockSpec(w_down.shape, lambda b, i: (0, 0)),
            pl.BlockSpec((1, d), lambda b, i: (0, 0)),
        ],
        out_specs=pl.BlockSpec((1, tm, d), lambda b, i: (b, i, 0)),
        out_shape=jax.ShapeDtypeStruct((bsz, s, d), F32),
        compiler_params=_cparams(("arbitrary", "arbitrary")),
        name="ffn_down_residual",
    )(x, mod_l, a, w_down, g_final.reshape(1, d))


def kernel(x, c, w_ada, b_ada, g_attn, w_in, diff_lambda, diff_subln, na_rpb, w_out, g_ffn, w_up,
           conv_w, conv_b, w_down, g_final):
    bsz, s, d = x.shape
    depth = w_ada.shape[0]
    d_ff = w_down.shape[1]
    n_ch = d_ff // FF_CHUNK
    tm = 512
    assert s % tm == 0 and tm % TILE == 0 and d_ff % FF_CHUNK == 0 and TILE % (2 * GRID_W) == 0
    assert (s // TILE) % max(DIL_Q_TILES, NA_Q_TILES) == 0 and s // GRID_W >= 4 * (TILE // GRID_W)

    mod = _modulation(c, w_ada, b_ada).reshape(depth, bsz, 6, d)
    tables = _rope_tables(s, DIFF_QK_DIM) + _rope_tables(s, HEAD_DIM)
    dil_offsets, dil_bias = _dilated_bias()
    rows = s // GRID_W

    wa, wb, wc = W_DIFF, W_DIL, W_NA
    offs = np.cumsum([0, wa, wa, wa, wb, wb, wb, wc, wc, wc])
    order = (0, 3, 6, 1, 4, 7, 2, 5, 8)

    for l in range(depth):
        w_perm = jnp.concatenate([w_in[l][:, offs[j]:offs[j + 1]] for j in order], axis=1).astype(BF16)
        qt, k, vt, kmax = _projection(x, mod[l], g_attn[l], w_perm, tables, tm)

        o_a = _diff_attention(qt, k, vt, kmax, diff_lambda[l], diff_subln[l], l, 2 * TILE)
        o_b = _band_attention(
            qt, k, vt, kmax, dil_bias,
            pl.BlockSpec(dil_bias.shape, lambda b, p, i: (0, 0, 0)),
            jnp.float32(math.log2(len(DIL_PATTERNS))),
            dil_offsets, False, DIL_Q_TILES, N_HEADS_DIFF // 2, N_HEADS_DIL // 2, "dilated_attention")
        o_c = _band_attention(
            qt, k, vt, kmax, _na_bias(na_rpb[l], rows),
            pl.BlockSpec((3, 1, 3 * TILE, 2 * TILE), lambda b, p, i: (0, p, 0, 0)),
            jnp.max(na_rpb[l]) * LOG2E,
            (-1, 0, 1), True, NA_Q_TILES, (N_HEADS_DIFF + N_HEADS_DIL) // 2, N_HEADS_NA // 2,
            "neighbourhood_attention")

        x = _out_projection(x, mod[l], o_a, o_b, o_c, w_out[l].astype(BF16), tm)

        wg = w_up[l][:, :d_ff].reshape(d, n_ch, FF_CHUNK).transpose(1, 0, 2).astype(BF16)
        wu = w_up[l][:, d_ff:].reshape(d, n_ch, FF_CHUNK).transpose(1, 0, 2).astype(BF16)
        cw = conv_w[l].reshape(CONV_W, n_ch, FF_CHUNK).transpose(1, 0, 2)
        cb = conv_b[l].reshape(n_ch, 1, FF_CHUNK)
        a = _ffn_up(x, mod[l], g_ffn[l], wg, wu, cw, cb, tm)
        x = _ffn_down(x, mod[l], a, w_down[l].astype(BF16), g_final, l == depth - 1, tm)
    return x
```

```python
import functools
import math

import numpy as np
import jax
import jax.numpy as jnp
from jax import lax
from jax.experimental import pallas as pl
from jax.experimental.pallas import tpu as pltpu

F32 = jnp.float32
BF16 = jnp.bfloat16

HEAD_DIM = 64
N_HEADS_DIFF = 4
N_HEADS_DIL = 6
N_HEADS_NA = 6
W_DIFF = N_HEADS_DIFF * HEAD_DIM
W_DIL = N_HEADS_DIL * HEAD_DIM
W_NA = N_HEADS_NA * HEAD_DIM
DIFF_QK_DIM = HEAD_DIM // 2
DIL_PATTERNS = ((128, 1), (512, 4), (2048, 16))
GRID_W = 64
NA_KH = 8
NA_KW = 16
CONV_W = 3
ROPE_THETA = 10000.0
EPS = 1e-6
NEG = -1e30

LANES = 128
PAIR_W = 2 * HEAD_DIM
TILE = 256
FF_CHUNK = 256
HALO = 16
VMEM_LIMIT = 56 * 1024 * 1024
LOG2E = math.log2(math.e)
SUM_ROWS = 16
LAG_MARGIN = 64.0
DIL_Q_TILES = 8
NA_Q_TILES = 8
LAG_UNROLL = 21
QK_AHEAD = 4


def _cparams(sem):
    return pltpu.CompilerParams(dimension_semantics=sem, vmem_limit_bytes=VMEM_LIMIT)


def _mod_kernel(ct_ref, w_ref, b_ref, o_ref):
    ct = ct_ref[...]
    s = ct * jax.nn.sigmoid(ct)
    w = w_ref[0]
    rows = [jnp.sum(w * s[:, b:b + 1], axis=0, keepdims=True) for b in range(ct.shape[1])]
    o_ref[0] = jnp.concatenate(rows, axis=0) + b_ref[0]


def _modulation(c, w_ada, b_ada):
    depth, d, n = w_ada.shape
    bsz = c.shape[0]
    tn = 768
    return pl.pallas_call(
        _mod_kernel,
        grid=(depth, n // tn),
        in_specs=[
            pl.BlockSpec((d, bsz), lambda l, j: (0, 0)),
            pl.BlockSpec((1, d, tn), lambda l, j: (l, 0, j)),
            pl.BlockSpec((1, 1, tn), lambda l, j: (l, 0, j)),
        ],
        out_specs=pl.BlockSpec((1, bsz, tn), lambda l, j: (l, 0, j)),
        out_shape=jax.ShapeDtypeStruct((depth, bsz, n), F32),
        compiler_params=_cparams(("arbitrary", "arbitrary")),
        name="adaln_mod",
    )(c.T, w_ada, b_ada.reshape(depth, 1, n))


def _norm_mod(x, g, sc, sh):
    ms = jnp.mean(x * x, axis=-1, keepdims=True)
    return (x * lax.rsqrt(ms + EPS)) * g * (1.0 + sc) + sh


def _rope_slab(acc, cos, sin, group, width):
    half = group // 2
    lane = lax.broadcasted_iota(jnp.int32, (1, LANES), 1) % group
    first = lane < half
    outs = []
    for c in range(width // LANES):
        xc = acc[:, c * LANES:(c + 1) * LANES]
        swapped = jnp.where(first, pltpu.roll(xc, LANES - half, axis=1), pltpu.roll(xc, half, axis=1))
        outs.append(xc * cos + swapped * sin)
    if width < acc.shape[1]:
        outs.append(acc[:, width:])
    return jnp.concatenate(outs, axis=1)


def _proj_kernel(x_ref, mod_ref, g_ref, w_ref, ca_ref, sa_ref, cb_ref, sb_ref,
                 qt_ref, k_ref, vt_ref, kmax_ref):
    tm = x_ref.shape[1]
    h = _norm_mod(x_ref[0], g_ref[...], mod_ref[0, 1:2, :], mod_ref[0, 0:1, :]).astype(BF16)
    ca, sa, cb, sb = ca_ref[...], sa_ref[...], cb_ref[...], sb_ref[...]
    width = W_DIFF + W_DIL + W_NA
    slabs = ((0, W_DIFF), (W_DIFF, W_DIL + W_NA))

    def slab(part, si):
        off, w = slabs[si]
        acc = jnp.dot(h, w_ref[:, part * width + off: part * width + off + w], preferred_element_type=F32)
        if part < 2 and si == 0:
            acc = _rope_slab(acc, ca, sa, DIFF_QK_DIM, W_DIFF)
        elif part < 2 and si == 1:
            acc = _rope_slab(acc, cb, sb, HEAD_DIM, W_DIL)
        return acc

    q_scale = (LOG2E * DIFF_QK_DIM ** -0.5, LOG2E * HEAD_DIM ** -0.5)
    for si, (off, w) in enumerate(slabs):
        q = slab(0, si) * q_scale[si]
        qt_ref[0, off:off + w, :] = q.T.astype(BF16)
        kb = slab(1, si).astype(BF16)
        k_ref[0, :, off:off + w] = kb
        kmax_ref[0, 0, :, off:off + w] = jnp.max(jnp.abs(kb.astype(F32)), axis=0, keepdims=True)
        v = slab(2, si)
        for t in range(tm // TILE):
            vt_ref[0, t, off:off + w, :] = v[t * TILE:(t + 1) * TILE, :].T.astype(BF16)


def _rope_tables(s, group):
    half = group // 2
    inv = ROPE_THETA ** (-jnp.arange(half, dtype=F32) / half)
    ang = jnp.arange(s, dtype=F32)[:, None] * inv
    cos, sin = jnp.cos(ang), jnp.sin(ang)
    reps = LANES // group
    return (jnp.tile(jnp.concatenate([cos, cos], axis=1), (1, reps)),
            jnp.tile(jnp.concatenate([-sin, sin], axis=1), (1, reps)))


def _projection(x, mod_l, g, w_perm, tables, tm):
    bsz, s, d = x.shape
    width = W_DIFF + W_DIL + W_NA
    tab_spec = pl.BlockSpec((tm, LANES), lambda b, i: (i, 0))
    return pl.pallas_call(
        _proj_kernel,
        grid=(bsz, s // tm),
        in_specs=[
            pl.BlockSpec((1, tm, d), lambda b, i: (b, i, 0)),
            pl.BlockSpec((1, 6, d), lambda b, i: (b, 0, 0)),
            pl.BlockSpec((1, d), lambda b, i: (0, 0)),
            pl.BlockSpec((d, 3 * width), lambda b, i: (0, 0)),
            tab_spec, tab_spec, tab_spec, tab_spec,
        ],
        out_specs=[
            pl.BlockSpec((1, width, tm), lambda b, i: (b, 0, i)),
            pl.BlockSpec((1, tm, width), lambda b, i: (b, i, 0)),
            pl.BlockSpec((1, tm // TILE, width, TILE), lambda b, i: (b, i, 0, 0)),
            pl.BlockSpec((1, 1, 1, width), lambda b, i: (b, i, 0, 0)),
        ],
        out_shape=[
            jax.ShapeDtypeStruct((bsz, width, s), BF16),
            jax.ShapeDtypeStruct((bsz, s, width), BF16),
            jax.ShapeDtypeStruct((bsz, s // TILE, width, TILE), BF16),
            jax.ShapeDtypeStruct((bsz, s // tm, 1, width), F32),
        ],
        compiler_params=_cparams(("arbitrary", "arbitrary")),
        name="norm_proj_rope",
    )(x, mod_l, g.reshape(1, d), w_perm, *tables)


def _masked_queries(qt, n_split):
    row = lax.broadcasted_iota(jnp.int32, (PAIR_W, 1), 0)
    step = PAIR_W // n_split
    return [jnp.where((row >= j * step) & (row < (j + 1) * step), qt, jnp.zeros_like(qt))
            for j in range(n_split)]


def _v_with_ones(vtb, hd):
    ones = jnp.ones((SUM_ROWS, vtb.shape[1]), BF16)
    return jnp.concatenate([vtb[hd * HEAD_DIM:(hd + 1) * HEAD_DIM, :], ones], axis=0)


def _score_bound(kmax_ref, q_s):
    kmax = jnp.max(kmax_ref[0], axis=0) * (1.0 + 2.0 ** -7)
    kmax = jnp.broadcast_to(kmax, (SUM_ROWS, PAIR_W)).astype(BF16)
    return jnp.dot(kmax, jnp.abs(q_s[...]), preferred_element_type=F32)[0:1] * (1.0 + 2.0 ** -7)


def _diff_kernel(lam_init, dl_ref, g_ref, kmax_ref, qt_ref, k_ref, vt_ref, o_ref, q_s, s_s, m_s, acc_s):
    tq = qt_ref.shape[2]
    n_kt = vt_ref.shape[1]
    for j, qm in enumerate(_masked_queries(qt_ref[0], 4)):
        q_s[:, j * tq:(j + 1) * tq] = qm
    lane_tiles = [slice(nt * TILE, (nt + 1) * TILE) for nt in range(4 * tq // TILE)]

    def k_tile(kt):
        return k_ref[0, pl.ds(pl.multiple_of(kt * TILE, TILE), TILE), :]

    def v_tiles(kt):
        vtb = vt_ref[0, kt]
        return [_v_with_ones(vtb, hd) for hd in range(2)]

    def exact_step(kt, first):
        kb, v_ext = k_tile(kt), v_tiles(kt)
        for ln in lane_tiles:
            s_s[:, ln] = jnp.dot(kb, q_s[:, ln], preferred_element_type=F32)
        for ln in lane_tiles:
            tile_max = jnp.max(s_s[:, ln], axis=0, keepdims=True)
            m_new = tile_max if first else jnp.maximum(m_s[:, ln], tile_max)
            e = jnp.exp2(s_s[:, ln] - m_new).astype(BF16)
            pv = jnp.dot(v_ext[ln.start // (2 * tq)], e, preferred_element_type=F32)
            acc_s[:, ln] = pv if first else acc_s[:, ln] * jnp.exp2(m_s[:, ln] - m_new) + pv
            m_s[:, ln] = m_new

    def fixed_shift_steps(kts):
        def finish(vtb, ln, s):
            hd = ln.start // (2 * tq)
            e = jnp.exp2(s - m_s[:, ln])
            acc_s[0:HEAD_DIM, ln] += jnp.dot(vtb[hd * HEAD_DIM:(hd + 1) * HEAD_DIM, :], e.astype(BF16),
                                             preferred_element_type=F32)
            acc_s[HEAD_DIM:HEAD_DIM + 1, ln] += jnp.sum(e, axis=0, keepdims=True)

        pending = []
        for kt in kts:
            kb, vtb = k_tile(kt), vt_ref[0, kt]
            for ln in lane_tiles:
                pending.append((vtb, ln, jnp.dot(kb, q_s[:, ln], preferred_element_type=F32)))
                if len(pending) > QK_AHEAD:
                    finish(*pending.pop(0))
        for item in pending:
            finish(*item)

    exact_step(0, True)
    lag_ok = jnp.max(_score_bound(kmax_ref, q_s) - m_s[...]) < LAG_MARGIN

    @pl.when(lag_ok)
    def _():
        def body(tt, carry):
            fixed_shift_steps([LAG_UNROLL * tt + 1 + u for u in range(LAG_UNROLL)])
            return carry
        n_trips = (n_kt - 1) // LAG_UNROLL
        lax.fori_loop(0, n_trips, body, 0)
        if n_trips * LAG_UNROLL + 1 < n_kt:
            fixed_shift_steps(list(range(n_trips * LAG_UNROLL + 1, n_kt)))

    @pl.when(jnp.logical_not(lag_ok))
    def _():
        def body(kt, carry):
            exact_step(kt, False)
            return carry
        lax.fori_loop(1, n_kt, body, 0)

    dl = dl_ref[...]
    lam = (jnp.exp(jnp.sum(dl[0:1] * dl[1:2], axis=1, keepdims=True))
           - jnp.exp(jnp.sum(dl[2:3] * dl[3:4], axis=1, keepdims=True)) + lam_init)
    g = g_ref[...]
    outs = []
    for hd in range(2):
        p = [acc_s[0:HEAD_DIM, (2 * hd + mp) * tq:(2 * hd + mp + 1) * tq]
             / acc_s[HEAD_DIM:HEAD_DIM + 1, (2 * hd + mp) * tq:(2 * hd + mp + 1) * tq] for mp in range(2)]
        o = p[0] - lam * p[1]
        ms = jnp.mean(o * o, axis=0, keepdims=True)
        outs.append((o * lax.rsqrt(ms + EPS)) * g * (1.0 - lam_init))
    o_ref[0] = jnp.concatenate(outs, axis=0).T.astype(BF16)


def _diff_attention(qt, k, vt, kmax, diff_lambda, subln, layer_idx, tq):
    bsz, _, s = qt.shape
    n_pairs = N_HEADS_DIFF // 2
    lam_init = 0.8 - 0.6 * math.exp(-0.3 * layer_idx)
    return pl.pallas_call(
        functools.partial(_diff_kernel, lam_init),
        grid=(bsz, n_pairs, s // tq),
        in_specs=[
            pl.BlockSpec(diff_lambda.shape, lambda b, p, i: (0, 0)),
            pl.BlockSpec((HEAD_DIM, 1), lambda b, p, i: (0, 0)),
            pl.BlockSpec((1, kmax.shape[1], 1, PAIR_W), lambda b, p, i: (b, 0, 0, p)),
            pl.BlockSpec((1, PAIR_W, tq), lambda b, p, i: (b, p, i)),
            pl.BlockSpec((1, s, PAIR_W), lambda b, p, i: (b, 0, p)),
            pl.BlockSpec((1, s // TILE, PAIR_W, TILE), lambda b, p, i: (b, 0, p, 0)),
        ],
        out_specs=pl.BlockSpec((1, tq, PAIR_W), lambda b, p, i: (b, i, p)),
        out_shape=jax.ShapeDtypeStruct((bsz, s, W_DIFF), BF16),
        scratch_shapes=[
            pltpu.VMEM((PAIR_W, 4 * tq), BF16),
            pltpu.VMEM((TILE, 4 * tq), F32),
            pltpu.VMEM((1, 4 * tq), F32),
            pltpu.VMEM((HEAD_DIM + SUM_ROWS, 4 * tq), F32),
        ],
        compiler_params=_cparams(("arbitrary", "arbitrary", "arbitrary")),
        name="diff_attention",
    )(diff_lambda, subln.reshape(HEAD_DIM, 1), kmax, qt, k, vt)


def _band_kernel(offsets, window_bias, bmax_ref, kmax_ref, qt_ref, k_ref, vt_ref, bias_ref, o_ref,
                 q_s, sc_s, s_s, acc_s):
    i = pl.program_id(2)
    n_blk = vt_ref.shape[1]
    n_off = len(offsets)
    centre = offsets.index(0)
    n_qt = qt_ref.shape[2] // TILE
    qt = qt_ref[0]
    for t in range(n_qt):
        for hd, qm in enumerate(_masked_queries(qt[:, t * TILE:(t + 1) * TILE], 2)):
            q_s[:, (2 * t + hd) * TILE:(2 * t + hd + 1) * TILE] = qm
    lane_tiles = [slice(c * TILE, (c + 1) * TILE) for c in range(2 * n_qt)]

    def q_tile(ln):
        return i * n_qt + ln.start // (2 * TILE)

    def key_tile(oi, ln):
        return jnp.clip(q_tile(ln) + offsets[oi], 0, n_blk - 1)

    def scores(oi, ln):
        qi = q_tile(ln)
        head_lanes = slice(ln.start % (2 * TILE), ln.start % (2 * TILE) + TILE)
        if window_bias:
            variant = jnp.where(qi == 0, 0, jnp.where(qi == n_blk - 1, 2, 1))
            bias = bias_ref[variant, 0, oi * TILE:(oi + 1) * TILE, head_lanes]
        else:
            kbi = qi + offsets[oi]
            bias = bias_ref[jnp.where((kbi >= 0) & (kbi < n_blk), oi, n_off), :, head_lanes]
        kb = k_ref[0, pl.ds(pl.multiple_of(key_tile(oi, ln) * TILE, TILE), TILE), :]
        return jnp.dot(kb, q_s[:, ln], preferred_element_type=F32) + bias

    def weighted_values(oi, ln, e):
        vtb = vt_ref[0, key_tile(oi, ln)]
        return jnp.dot(_v_with_ones(vtb, (ln.start // TILE) % 2), e, preferred_element_type=F32)

    for ln in lane_tiles:
        sc_s[:, ln] = scores(centre, ln)
    m_c = jnp.max(sc_s[...], axis=0, keepdims=True)
    shift_ok = jnp.max(_score_bound(kmax_ref, q_s) + bmax_ref[...] - m_c) < LAG_MARGIN

    @pl.when(shift_ok)
    def _():
        acc = {ln.start: None for ln in lane_tiles}

        def finish(oi, ln, s):
            pv = weighted_values(oi, ln, jnp.exp2(s - m_c[:, ln]).astype(BF16))
            acc[ln.start] = pv if acc[ln.start] is None else acc[ln.start] + pv

        pending = []
        for oi in range(n_off):
            for ln in lane_tiles:
                pending.append((oi, ln, sc_s[:, ln] if oi == centre else scores(oi, ln)))
                if len(pending) > QK_AHEAD:
                    finish(*pending.pop(0))
        for item in pending:
            finish(*item)
        for ln in lane_tiles:
            acc_s[:, ln] = acc[ln.start]

    @pl.when(jnp.logical_not(shift_ok))
    def _():
        for ln in lane_tiles:
            col_max = m_c[:, ln]
            for oi in range(n_off):
                s_s[oi] = sc_s[:, ln] if oi == centre else scores(oi, ln)
                col_max = jnp.maximum(col_max, jnp.max(s_s[oi], axis=0, keepdims=True))
            acc = None
            for oi in range(n_off):
                pv = weighted_values(oi, ln, jnp.exp2(s_s[oi] - col_max).astype(BF16))
                acc = pv if acc is None else acc + pv
            acc_s[:, ln] = acc

    out = acc_s[0:HEAD_DIM, :] / acc_s[HEAD_DIM:HEAD_DIM + 1, :]
    for t in range(n_qt):
        pair = jnp.concatenate([out[:, lane_tiles[2 * t]], out[:, lane_tiles[2 * t + 1]]], axis=0)
        o_ref[0, t * TILE:(t + 1) * TILE, :] = pair.T.astype(BF16)


def _band_attention(qt, k, vt, kmax, bias, bias_spec, bias_max, offsets, window_bias, q_tiles, pair0, n_pairs,
                    name):
    bsz, _, s = qt.shape
    tq = q_tiles * TILE
    return pl.pallas_call(
        functools.partial(_band_kernel, offsets, window_bias),
        grid=(bsz, n_pairs, s // tq),
        in_specs=[
            pl.BlockSpec((1, 1), lambda b, p, i: (0, 0)),
            pl.BlockSpec((1, kmax.shape[1], 1, PAIR_W), lambda b, p, i: (b, 0, 0, pair0 + p)),
            pl.BlockSpec((1, PAIR_W, tq), lambda b, p, i: (b, pair0 + p, i)),
            pl.BlockSpec((1, s, PAIR_W), lambda b, p, i: (b, 0, pair0 + p)),
            pl.BlockSpec((1, s // TILE, PAIR_W, TILE), lambda b, p, i: (b, 0, pair0 + p, 0)),
            bias_spec,
        ],
        out_specs=pl.BlockSpec((1, tq, PAIR_W), lambda b, p, i: (b, i, p)),
        out_shape=jax.ShapeDtypeStruct((bsz, s, n_pairs * PAIR_W), BF16),
        scratch_shapes=[
            pltpu.VMEM((PAIR_W, 2 * tq), BF16),
            pltpu.VMEM((TILE, 2 * tq), F32),
            pltpu.VMEM((len(offsets), TILE, TILE), F32),
            pltpu.VMEM((HEAD_DIM + SUM_ROWS, 2 * tq), F32),
        ],
        compiler_params=_cparams(("arbitrary", "arbitrary", "arbitrary")),
        name=name,
    )(jnp.reshape(bias_max, (1, 1)).astype(F32), kmax, qt, k, vt, bias)


def _dilated_bias():
    reach = max(w // 2 for w, _ in DIL_PATTERNS)
    n_off = -(-reach // TILE)
    offsets = tuple(range(-n_off, n_off + 1))
    kj = np.arange(TILE)[:, None]
    qi = np.arange(TILE)[None, :]
    tabs = []
    for off in offsets:
        delta = off * TILE + kj - qi
        cnt = np.zeros((TILE, TILE), np.int32)
        for window, dil in DIL_PATTERNS:
            cnt += ((delta % dil == 0) & (np.abs(delta) <= window // 2)).astype(np.int32)
        tabs.append(np.where(cnt > 0, np.log2(np.maximum(cnt, 1).astype(np.float64)), NEG))
    tabs.append(np.full((TILE, TILE), NEG))
    return offsets, jnp.asarray(np.tile(np.stack(tabs), (1, 1, 2)), F32)


def _na_bias_kernel(rows, rpb_ref, o_ref):
    hd = pl.program_id(0)
    kh = min(NA_KH, rows)
    q_rows = TILE // GRID_W
    kc = lax.broadcasted_iota(jnp.int32, (GRID_W, LANES), 0)
    lane = lax.broadcasted_iota(jnp.int32, (GRID_W, LANES), 1)
    qc = lane % GRID_W
    dc = jnp.clip(kc - qc, -(NA_KW - 1), NA_KW - 1) + (NA_KW - 1)
    cs = jnp.clip(qc - NA_KW // 2, 0, GRID_W - NA_KW)
    col_ok = (kc >= cs) & (kc < cs + NA_KW)
    neg = jnp.full((GRID_W, LANES), NEG, F32)
    tiles = []
    for dr in range(2 * NA_KH - 1):
        t = neg
        for j in range(2 * NA_KW - 1):
            t = jnp.where(dc == j, rpb_ref[hd, dr, j] * LOG2E, t)
        tiles.append(jnp.where(col_ok, t, neg))
    for v, r0 in enumerate((0, 2 * q_rows, rows - q_rows)):
        for kr_rel in range(3 * q_rows):
            kr = r0 - q_rows + kr_rel
            for pair in range(q_rows // 2):
                halves = []
                for r in (r0 + 2 * pair, r0 + 2 * pair + 1):
                    rs = min(max(r - kh // 2, 0), rows - kh)
                    halves.append(tiles[kr - r + NA_KH - 1] if rs <= kr < rs + kh else neg)
                blk = jnp.where(lane < GRID_W, halves[0], halves[1])
                o_ref[v, 0, kr_rel * GRID_W:(kr_rel + 1) * GRID_W, pair * LANES:(pair + 1) * LANES] = blk


def _na_bias(rpb, rows):
    n_heads = rpb.shape[0]
    return pl.pallas_call(
        functools.partial(_na_bias_kernel, rows),
        grid=(n_heads,),
        in_specs=[pl.BlockSpec(memory_space=pltpu.SMEM)],
        out_specs=pl.BlockSpec((3, 1, 3 * TILE, TILE), lambda h: (0, h // 2, 0, h % 2)),
        out_shape=jax.ShapeDtypeStruct((3, n_heads // 2, 3 * TILE, 2 * TILE), F32),
        compiler_params=_cparams(("arbitrary",)),
        name="na_bias_table",
    )(rpb)


def _outproj_kernel(x_ref, mod_ref, oa_ref, ob_ref, oc_ref, w_ref, y_ref):
    o = jnp.concatenate([oa_ref[0], ob_ref[0], oc_ref[0]], axis=1)
    mix = jnp.dot(o, w_ref[...], preferred_element_type=F32)
    y_ref[0] = x_ref[0] + mod_ref[0, 2:3, :] * mix


def _out_projection(x, mod_l, o_a, o_b, o_c, w_out, tm):
    bsz, s, d = x.shape
    row = lambda w: pl.BlockSpec((1, tm, w), lambda b, i: (b, i, 0))
    return pl.pallas_call(
        _outproj_kernel,
        grid=(bsz, s // tm),
        in_specs=[
            row(d),
            pl.BlockSpec((1, 6, d), lambda b, i: (b, 0, 0)),
            row(W_DIFF), row(W_DIL), row(W_NA),
            pl.BlockSpec(w_out.shape, lambda b, i: (0, 0)),
        ],
        out_specs=row(d),
        out_shape=jax.ShapeDtypeStruct((bsz, s, d), F32),
        compiler_params=_cparams(("arbitrary", "arbitrary")),
        name="out_proj_residual",
    )(x, mod_l, o_a, o_b, o_c, w_out)


def _ffn_up_kernel(x_ref, xp_ref, xn_ref, mod_ref, g_ref, wg_ref, wu_ref, cw_ref, cb_ref,
                   a_ref, h_s, g_s, u_s):
    i = pl.program_id(1)
    tm = x_ref.shape[1]
    g, sc, sh = g_ref[...], mod_ref[0, 4:5, :], mod_ref[0, 3:4, :]
    keep_prev = (i > 0).astype(F32)
    keep_next = (i < pl.num_programs(1) - 1).astype(F32)
    h_s[0:HALO, :] = (_norm_mod(xp_ref[0], g, sc, sh) * keep_prev).astype(BF16)
    h_s[HALO:HALO + tm, :] = _norm_mod(x_ref[0], g, sc, sh).astype(BF16)
    h_s[HALO + tm:, :] = (_norm_mod(xn_ref[0], g, sc, sh) * keep_next).astype(BF16)
    n_ch = wg_ref.shape[0]

    def matmuls(c):
        g_s[c % 2] = jnp.dot(h_s[...], wg_ref[c], preferred_element_type=F32)
        u_s[c % 2] = jnp.dot(h_s[HALO:HALO + tm, :], wu_ref[c], preferred_element_type=F32)

    def gate(c):
        cw = cw_ref[c]
        gc = cb_ref[c]
        for t in range(CONV_W):
            gc = gc + g_s[c % 2, pl.ds(HALO - CONV_W // 2 + t, tm), :] * cw[t:t + 1, :]
        a_ref[0, :, c * FF_CHUNK:(c + 1) * FF_CHUNK] = (gc * jax.nn.sigmoid(gc) * u_s[c % 2]).astype(BF16)

    matmuls(0)
    for c in range(n_ch):
        if c + 1 < n_ch:
            matmuls(c + 1)
        gate(c)


def _ffn_up(x, mod_l, g, wg, wu, cw, cb, tm):
    bsz, s, d = x.shape
    n_ch = wg.shape[0]
    hb = tm // HALO
    n_hb = s // HALO
    return pl.pallas_call(
        _ffn_up_kernel,
        grid=(bsz, s // tm),
        in_specs=[
            pl.BlockSpec((1, tm, d), lambda b, i: (b, i, 0)),
            pl.BlockSpec((1, HALO, d), lambda b, i: (b, jnp.maximum(i * hb - 1, 0), 0)),
            pl.BlockSpec((1, HALO, d), lambda b, i: (b, jnp.minimum((i + 1) * hb, n_hb - 1), 0)),
            pl.BlockSpec((1, 6, d), lambda b, i: (b, 0, 0)),
            pl.BlockSpec((1, d), lambda b, i: (0, 0)),
            pl.BlockSpec(wg.shape, lambda b, i: (0, 0, 0)),
            pl.BlockSpec(wu.shape, lambda b, i: (0, 0, 0)),
            pl.BlockSpec(cw.shape, lambda b, i: (0, 0, 0)),
            pl.BlockSpec(cb.shape, lambda b, i: (0, 0, 0)),
        ],
        out_specs=pl.BlockSpec((1, tm, n_ch * FF_CHUNK), lambda b, i: (b, i, 0)),
        out_shape=jax.ShapeDtypeStruct((bsz, s, n_ch * FF_CHUNK), BF16),
        scratch_shapes=[
            pltpu.VMEM((tm + 2 * HALO, d), BF16),
            pltpu.VMEM((2, tm + 2 * HALO, FF_CHUNK), F32),
            pltpu.VMEM((2, tm, FF_CHUNK), F32),
        ],
        compiler_params=_cparams(("arbitrary", "arbitrary")),
        name="ffn_up_conv_glu",
    )(x, x, x, mod_l, g.reshape(1, d), wg, wu, cw, cb)


def _ffn_down_kernel(final, x_ref, mod_ref, a_ref, w_ref, gf_ref, y_ref):
    y = x_ref[0] + mod_ref[0, 5:6, :] * jnp.dot(a_ref[0], w_ref[...], preferred_element_type=F32)
    if final:
        ms = jnp.mean(y * y, axis=-1, keepdims=True)
        y = (y * lax.rsqrt(ms + EPS)) * gf_ref[...]
    y_ref[0] = y


def _ffn_down(x, mod_l, a, w_down, g_final, final, tm):
    bsz, s, d = x.shape
    return pl.pallas_call(
        functools.partial(_ffn_down_kernel, final),
        grid=(bsz, s // tm),
        in_specs=[
            pl.BlockSpec((1, tm, d), lambda b, i: (b, i, 0)),
            pl.BlockSpec((1, 6, d), lambda b, i: (b, 0, 0)),
            pl.BlockSpec((1, tm, a.shape[2]), lambda b, i: (b, i, 0)),
            pl.BlockSpec(w_down.shape, lambda b, i: (0, 0)),
            pl.BlockSpec((1, d), lambda b, i: (0, 0)),
        ],
        out_specs=pl.BlockSpec((1, tm, d), lambda b, i: (b, i, 0)),
        out_shape=jax.ShapeDtypeStruct((bsz, s, d), F32),
        compiler_params=_cparams(("arbitrary", "arbitrary")),
        name="ffn_down_residual",
    )(x, mod_l, a, w_down, g_final.reshape(1, d))


def kernel(x, c, w_ada, b_ada, g_attn, w_in, diff_lambda, diff_subln, na_rpb, w_out, g_ffn, w_up,
           conv_w, conv_b, w_down, g_final):
    bsz, s, d = x.shape
    depth = w_ada.shape[0]
    d_ff = w_down.shape[1]
    n_ch = d_ff // FF_CHUNK
    tm = 512
    assert s % tm == 0 and tm % TILE == 0 and d_ff % FF_CHUNK == 0 and TILE % (2 * GRID_W) == 0
    assert (s // TILE) % max(DIL_Q_TILES, NA_Q_TILES) == 0 and s // GRID_W >= 4 * (TILE // GRID_W)

    mod = _modulation(c, w_ada, b_ada).reshape(depth, bsz, 6, d)
    tables = _rope_tables(s, DIFF_QK_DIM) + _rope_tables(s, HEAD_DIM)
    dil_offsets, dil_bias = _dilated_bias()
    rows = s // GRID_W

    wa, wb, wc = W_DIFF, W_DIL, W_NA
    offs = np.cumsum([0, wa, wa, wa, wb, wb, wb, wc, wc, wc])
    order = (0, 3, 6, 1, 4, 7, 2, 5, 8)

    for l in range(depth):
        w_perm = jnp.concatenate([w_in[l][:, offs[j]:offs[j + 1]] for j in order], axis=1).astype(BF16)
        qt, k, vt, kmax = _projection(x, mod[l], g_attn[l], w_perm, tables, tm)

        o_a = _diff_attention(qt, k, vt, kmax, diff_lambda[l], diff_subln[l], l, 2 * TILE)
        o_b = _band_attention(
            qt, k, vt, kmax, dil_bias,
            pl.BlockSpec(dil_bias.shape, lambda b, p, i: (0, 0, 0)),
            jnp.float32(math.log2(len(DIL_PATTERNS))),
            dil_offsets, False, DIL_Q_TILES, N_HEADS_DIFF // 2, N_HEADS_DIL // 2, "dilated_attention")
        o_c = _band_attention(
            qt, k, vt, kmax, _na_bias(na_rpb[l], rows),
            pl.BlockSpec((3, 1, 3 * TILE, 2 * TILE), lambda b, p, i: (0, p, 0, 0)),
            jnp.max(na_rpb[l]) * LOG2E,
            (-1, 0, 1), True, NA_Q_TILES, (N_HEADS_DIFF + N_HEADS_DIL) // 2, N_HEADS_NA // 2,
            "neighbourhood_attention")

        x = _out_projection(x, mod[l], o_a, o_b, o_c, w_out[l].astype(BF16), tm)

        wg = w_up[l][:, :d_ff].reshape(d, n_ch, FF_CHUNK).transpose(1, 0, 2).astype(BF16)
        wu = w_up[l][:, d_ff:].reshape(d, n_ch, FF_CHUNK).transpose(1, 0, 2).astype(BF16)
        cw = conv_w[l].reshape(CONV_W, n_ch, FF_CHUNK).transpose(1, 0, 2)
        cb = conv_b[l].reshape(n_ch, 1, FF_CHUNK)
        a = _ffn_up(x, mod[l], g_ffn[l], wg, wu, cw, cb, tm)
        x = _ffn_down(x, mod[l], a, w_down[l].astype(BF16), g_final, l == depth - 1, tm)
    return x
```

```python
import functools
import math

import numpy as np
import jax
import jax.numpy as jnp
from jax import lax
from jax.experimental import pallas as pl
from jax.experimental.pallas import tpu as pltpu

F32 = jnp.float32
BF16 = jnp.bfloat16

HEAD_DIM = 64
N_HEADS_DIFF = 4
N_HEADS_DIL = 6
N_HEADS_NA = 6
W_DIFF = N_HEADS_DIFF * HEAD_DIM
W_DIL = N_HEADS_DIL * HEAD_DIM
W_NA = N_HEADS_NA * HEAD_DIM
DIFF_QK_DIM = HEAD_DIM // 2
DIL_PATTERNS = ((128, 1), (512, 4), (2048, 16))
GRID_W = 64
NA_KH = 8
NA_KW = 16
CONV_W = 3
ROPE_THETA = 10000.0
EPS = 1e-6
NEG = -1e30

LANES = 128
PAIR_W = 2 * HEAD_DIM
TILE = 256
FF_CHUNK = 256
HALO = 16
VMEM_LIMIT = 56 * 1024 * 1024
LOG2E = math.log2(math.e)
SUM_ROWS = 16
LAG_MARGIN = 64.0
DIL_Q_TILES = 8
NA_Q_TILES = 8
LAG_UNROLL = 63
QK_AHEAD = 4


def _cparams(sem):
    return pltpu.CompilerParams(dimension_semantics=sem, vmem_limit_bytes=VMEM_LIMIT)


def _mod_kernel(ct_ref, w_ref, b_ref, o_ref):
    ct = ct_ref[...]
    s = ct * jax.nn.sigmoid(ct)
    w = w_ref[0]
    rows = [jnp.sum(w * s[:, b:b + 1], axis=0, keepdims=True) for b in range(ct.shape[1])]
    o_ref[0] = jnp.concatenate(rows, axis=0) + b_ref[0]


def _modulation(c, w_ada, b_ada):
    depth, d, n = w_ada.shape
    bsz = c.shape[0]
    tn = 768
    return pl.pallas_call(
        _mod_kernel,
        grid=(depth, n // tn),
        in_specs=[
            pl.BlockSpec((d, bsz), lambda l, j: (0, 0)),
            pl.BlockSpec((1, d, tn), lambda l, j: (l, 0, j)),
            pl.BlockSpec((1, 1, tn), lambda l, j: (l, 0, j)),
        ],
        out_specs=pl.BlockSpec((1, bsz, tn), lambda l, j: (l, 0, j)),
        out_shape=jax.ShapeDtypeStruct((depth, bsz, n), F32),
        compiler_params=_cparams(("arbitrary", "arbitrary")),
        name="adaln_mod",
    )(c.T, w_ada, b_ada.reshape(depth, 1, n))


def _norm_mod(x, g, sc, sh):
    ms = jnp.mean(x * x, axis=-1, keepdims=True)
    return (x * lax.rsqrt(ms + EPS)) * g * (1.0 + sc) + sh


def _rope_slab(acc, cos, sin, group, width):
    half = group // 2
    lane = lax.broadcasted_iota(jnp.int32, (1, LANES), 1) % group
    first = lane < half
    outs = []
    for c in range(width // LANES):
        xc = acc[:, c * LANES:(c + 1) * LANES]
        swapped = jnp.where(first, pltpu.roll(xc, LANES - half, axis=1), pltpu.roll(xc, half, axis=1))
        outs.append(xc * cos + swapped * sin)
    if width < acc.shape[1]:
        outs.append(acc[:, width:])
    return jnp.concatenate(outs, axis=1)


def _proj_kernel(x_ref, mod_ref, g_ref, w_ref, ca_ref, sa_ref, cb_ref, sb_ref,
                 qt_ref, k_ref, vt_ref, kmax_ref):
    tm = x_ref.shape[1]
    h = _norm_mod(x_ref[0], g_ref[...], mod_ref[0, 1:2, :], mod_ref[0, 0:1, :]).astype(BF16)
    ca, sa, cb, sb = ca_ref[...], sa_ref[...], cb_ref[...], sb_ref[...]
    width = W_DIFF + W_DIL + W_NA
    slabs = ((0, W_DIFF), (W_DIFF, W_DIL + W_NA))

    def slab(part, si):
        off, w = slabs[si]
        acc = jnp.dot(h, w_ref[:, part * width + off: part * width + off + w], preferred_element_type=F32)
        if part < 2 and si == 0:
            acc = _rope_slab(acc, ca, sa, DIFF_QK_DIM, W_DIFF)
        elif part < 2 and si == 1:
            acc = _rope_slab(acc, cb, sb, HEAD_DIM, W_DIL)
        return acc

    q_scale = (LOG2E * DIFF_QK_DIM ** -0.5, LOG2E * HEAD_DIM ** -0.5)
    for si, (off, w) in enumerate(slabs):
        q = slab(0, si) * q_scale[si]
        qt_ref[0, off:off + w, :] = q.T.astype(BF16)
        kb = slab(1, si).astype(BF16)
        k_ref[0, :, off:off + w] = kb
        kmax_ref[0, 0, :, off:off + w] = jnp.max(jnp.abs(kb.astype(F32)), axis=0, keepdims=True)
        v = slab(2, si)
        for t in range(tm // TILE):
            vt_ref[0, t, off:off + w, :] = v[t * TILE:(t + 1) * TILE, :].T.astype(BF16)


def _rope_tables(s, group):
    half = group // 2
    inv = ROPE_THETA ** (-jnp.arange(half, dtype=F32) / half)
    ang = jnp.arange(s, dtype=F32)[:, None] * inv
    cos, sin = jnp.cos(ang), jnp.sin(ang)
    reps = LANES // group
    return (jnp.tile(jnp.concatenate([cos, cos], axis=1), (1, reps)),
            jnp.tile(jnp.concatenate([-sin, sin], axis=1), (1, reps)))


def _projection(x, mod_l, g, w_perm, tables, tm):
    bsz, s, d = x.shape
    width = W_DIFF + W_DIL + W_NA
    tab_spec = pl.BlockSpec((tm, LANES), lambda b, i: (i, 0))
    return pl.pallas_call(
        _proj_kernel,
        grid=(bsz, s // tm),
        in_specs=[
            pl.BlockSpec((1, tm, d), lambda b, i: (b, i, 0)),
            pl.BlockSpec((1, 6, d), lambda b, i: (b, 0, 0)),
            pl.BlockSpec((1, d), lambda b, i: (0, 0)),
            pl.BlockSpec((d, 3 * width), lambda b, i: (0, 0)),
            tab_spec, tab_spec, tab_spec, tab_spec,
        ],
        out_specs=[
            pl.BlockSpec((1, width, tm), lambda b, i: (b, 0, i)),
            pl.BlockSpec((1, tm, width), lambda b, i: (b, i, 0)),
            pl.BlockSpec((1, tm // TILE, width, TILE), lambda b, i: (b, i, 0, 0)),
            pl.BlockSpec((1, 1, 1, width), lambda b, i: (b, i, 0, 0)),
        ],
        out_shape=[
            jax.ShapeDtypeStruct((bsz, width, s), BF16),
            jax.ShapeDtypeStruct((bsz, s, width), BF16),
            jax.ShapeDtypeStruct((bsz, s // TILE, width, TILE), BF16),
            jax.ShapeDtypeStruct((bsz, s // tm, 1, width), F32),
        ],
        compiler_params=_cparams(("arbitrary", "arbitrary")),
        name="norm_proj_rope",
    )(x, mod_l, g.reshape(1, d), w_perm, *tables)


def _masked_queries(qt, n_split):
    row = lax.broadcasted_iota(jnp.int32, (PAIR_W, 1), 0)
    step = PAIR_W // n_split
    return [jnp.where((row >= j * step) & (row < (j + 1) * step), qt, jnp.zeros_like(qt))
            for j in range(n_split)]


def _v_with_ones(vtb, hd):
    ones = jnp.ones((SUM_ROWS, vtb.shape[1]), BF16)
    return jnp.concatenate([vtb[hd * HEAD_DIM:(hd + 1) * HEAD_DIM, :], ones], axis=0)


def _score_bound(kmax_ref, q_s):
    kmax = jnp.max(kmax_ref[0], axis=0) * (1.0 + 2.0 ** -7)
    kmax = jnp.broadcast_to(kmax, (SUM_ROWS, PAIR_W)).astype(BF16)
    return jnp.dot(kmax, jnp.abs(q_s[...]), preferred_element_type=F32)[0:1] * (1.0 + 2.0 ** -7)


def _diff_kernel(lam_init, dl_ref, g_ref, kmax_ref, qt_ref, k_ref, vt_ref, o_ref, q_s, s_s, m_s, acc_s):
    tq = qt_ref.shape[2]
    n_kt = vt_ref.shape[1]
    for j, qm in enumerate(_masked_queries(qt_ref[0], 4)):
        q_s[:, j * tq:(j + 1) * tq] = qm
    lane_tiles = [slice(nt * TILE, (nt + 1) * TILE) for nt in range(4 * tq // TILE)]

    def k_tile(kt):
        return k_ref[0, pl.ds(pl.multiple_of(kt * TILE, TILE), TILE), :]

    def v_tiles(kt):
        vtb = vt_ref[0, kt]
        return [_v_with_ones(vtb, hd) for hd in range(2)]

    def exact_step(kt, first):
        kb, v_ext = k_tile(kt), v_tiles(kt)
        for ln in lane_tiles:
            s_s[:, ln] = jnp.dot(kb, q_s[:, ln], preferred_element_type=F32)
        for ln in lane_tiles:
            tile_max = jnp.max(s_s[:, ln], axis=0, keepdims=True)
            m_new = tile_max if first else jnp.maximum(m_s[:, ln], tile_max)
            e = jnp.exp2(s_s[:, ln] - m_new).astype(BF16)
            pv = jnp.dot(v_ext[ln.start // (2 * tq)], e, preferred_element_type=F32)
            acc_s[:, ln] = pv if first else acc_s[:, ln] * jnp.exp2(m_s[:, ln] - m_new) + pv
            m_s[:, ln] = m_new

    def fixed_shift_steps(kts):
        def finish(vtb, ln, s):
            hd = ln.start // (2 * tq)
            e = jnp.exp2(s - m_s[:, ln])
            acc_s[0:HEAD_DIM, ln] += jnp.dot(vtb[hd * HEAD_DIM:(hd + 1) * HEAD_DIM, :], e.astype(BF16),
                                             preferred_element_type=F32)
            acc_s[HEAD_DIM:HEAD_DIM + 1, ln] += jnp.sum(e, axis=0, keepdims=True)

        pending = []
        for kt in kts:
            kb, vtb = k_tile(kt), vt_ref[0, kt]
            for ln in lane_tiles:
                pending.append((vtb, ln, jnp.dot(kb, q_s[:, ln], preferred_element_type=F32)))
                if len(pending) > QK_AHEAD:
                    finish(*pending.pop(0))
        for item in pending:
            finish(*item)

    exact_step(0, True)
    lag_ok = jnp.max(_score_bound(kmax_ref, q_s) - m_s[...]) < LAG_MARGIN

    @pl.when(lag_ok)
    def _():
        def body(tt, carry):
            fixed_shift_steps([LAG_UNROLL * tt + 1 + u for u in range(LAG_UNROLL)])
            return carry
        n_trips = (n_kt - 1) // LAG_UNROLL
        lax.fori_loop(0, n_trips, body, 0)
        if n_trips * LAG_UNROLL + 1 < n_kt:
            fixed_shift_steps(list(range(n_trips * LAG_UNROLL + 1, n_kt)))

    @pl.when(jnp.logical_not(lag_ok))
    def _():
        def body(kt, carry):
            exact_step(kt, False)
            return carry
        lax.fori_loop(1, n_kt, body, 0)

    dl = dl_ref[...]
    lam = (jnp.exp(jnp.sum(dl[0:1] * dl[1:2], axis=1, keepdims=True))
           - jnp.exp(jnp.sum(dl[2:3] * dl[3:4], axis=1, keepdims=True)) + lam_init)
    g = g_ref[...]
    outs = []
    for hd in range(2):
        p = [acc_s[0:HEAD_DIM, (2 * hd + mp) * tq:(2 * hd + mp + 1) * tq]
             / acc_s[HEAD_DIM:HEAD_DIM + 1, (2 * hd + mp) * tq:(2 * hd + mp + 1) * tq] for mp in range(2)]
        o = p[0] - lam * p[1]
        ms = jnp.mean(o * o, axis=0, keepdims=True)
        outs.append((o * lax.rsqrt(ms + EPS)) * g * (1.0 - lam_init))
    o_ref[0] = jnp.concatenate(outs, axis=0).T.astype(BF16)


def _diff_attention(qt, k, vt, kmax, diff_lambda, subln, layer_idx, tq):
    bsz, _, s = qt.shape
    n_pairs = N_HEADS_DIFF // 2
    lam_init = 0.8 - 0.6 * math.exp(-0.3 * layer_idx)
    return pl.pallas_call(
        functools.partial(_diff_kernel, lam_init),
        grid=(bsz, n_pairs, s // tq),
        in_specs=[
            pl.BlockSpec(diff_lambda.shape, lambda b, p, i: (0, 0)),
            pl.BlockSpec((HEAD_DIM, 1), lambda b, p, i: (0, 0)),
            pl.BlockSpec((1, kmax.shape[1], 1, PAIR_W), lambda b, p, i: (b, 0, 0, p)),
            pl.BlockSpec((1, PAIR_W, tq), lambda b, p, i: (b, p, i)),
            pl.BlockSpec((1, s, PAIR_W), lambda b, p, i: (b, 0, p)),
            pl.BlockSpec((1, s // TILE, PAIR_W, TILE), lambda b, p, i: (b, 0, p, 0)),
        ],
        out_specs=pl.BlockSpec((1, tq, PAIR_W), lambda b, p, i: (b, i, p)),
        out_shape=jax.ShapeDtypeStruct((bsz, s, W_DIFF), BF16),
        scratch_shapes=[
            pltpu.VMEM((PAIR_W, 4 * tq), BF16),
            pltpu.VMEM((TILE, 4 * tq), F32),
            pltpu.VMEM((1, 4 * tq), F32),
            pltpu.VMEM((HEAD_DIM + SUM_ROWS, 4 * tq), F32),
        ],
        compiler_params=_cparams(("arbitrary", "arbitrary", "arbitrary")),
        name="diff_attention",
    )(diff_lambda, subln.reshape(HEAD_DIM, 1), kmax, qt, k, vt)


def _band_kernel(offsets, window_bias, bmax_ref, kmax_ref, qt_ref, k_ref, vt_ref, bias_ref, o_ref,
                 q_s, sc_s, s_s, acc_s):
    i = pl.program_id(2)
    n_blk = vt_ref.shape[1]
    n_off = len(offsets)
    centre = offsets.index(0)
    n_qt = qt_ref.shape[2] // TILE
    qt = qt_ref[0]
    for t in range(n_qt):
        for hd, qm in enumerate(_masked_queries(qt[:, t * TILE:(t + 1) * TILE], 2)):
            q_s[:, (2 * t + hd) * TILE:(2 * t + hd + 1) * TILE] = qm
    lane_tiles = [slice(c * TILE, (c + 1) * TILE) for c in range(2 * n_qt)]

    def q_tile(ln):
        return i * n_qt + ln.start // (2 * TILE)

    def key_tile(oi, ln):
        return jnp.clip(q_tile(ln) + offsets[oi], 0, n_blk - 1)

    def scores(oi, ln):
        qi = q_tile(ln)
        head_lanes = slice(ln.start % (2 * TILE), ln.start % (2 * TILE) + TILE)
        if window_bias:
            variant = jnp.where(qi == 0, 0, jnp.where(qi == n_blk - 1, 2, 1))
            bias = bias_ref[variant, 0, oi * TILE:(oi + 1) * TILE, head_lanes]
        else:
            kbi = qi + offsets[oi]
            bias = bias_ref[jnp.where((kbi >= 0) & (kbi < n_blk), oi, n_off), :, head_lanes]
        kb = k_ref[0, pl.ds(pl.multiple_of(key_tile(oi, ln) * TILE, TILE), TILE), :]
        return jnp.dot(kb, q_s[:, ln], preferred_element_type=F32) + bias

    def weighted_values(oi, ln, e):
        vtb = vt_ref[0, key_tile(oi, ln)]
        return jnp.dot(_v_with_ones(vtb, (ln.start // TILE) % 2), e, preferred_element_type=F32)

    for ln in lane_tiles:
        sc_s[:, ln] = scores(centre, ln)
    m_c = jnp.max(sc_s[...], axis=0, keepdims=True)
    shift_ok = jnp.max(_score_bound(kmax_ref, q_s) + bmax_ref[...] - m_c) < LAG_MARGIN

    @pl.when(shift_ok)
    def _():
        acc = {ln.start: None for ln in lane_tiles}

        def finish(oi, ln, s):
            pv = weighted_values(oi, ln, jnp.exp2(s - m_c[:, ln]).astype(BF16))
            acc[ln.start] = pv if acc[ln.start] is None else acc[ln.start] + pv

        pending = []
        for oi in range(n_off):
            for ln in lane_tiles:
                pending.append((oi, ln, sc_s[:, ln] if oi == centre else scores(oi, ln)))
                if len(pending) > QK_AHEAD:
                    finish(*pending.pop(0))
        for item in pending:
            finish(*item)
        for ln in lane_tiles:
            acc_s[:, ln] = acc[ln.start]

    @pl.when(jnp.logical_not(shift_ok))
    def _():
        for ln in lane_tiles:
            col_max = m_c[:, ln]
            for oi in range(n_off):
                s_s[oi] = sc_s[:, ln] if oi == centre else scores(oi, ln)
                col_max = jnp.maximum(col_max, jnp.max(s_s[oi], axis=0, keepdims=True))
            acc = None
            for oi in range(n_off):
                pv = weighted_values(oi, ln, jnp.exp2(s_s[oi] - col_max).astype(BF16))
                acc = pv if acc is None else acc + pv
            acc_s[:, ln] = acc

    out = acc_s[0:HEAD_DIM, :] / acc_s[HEAD_DIM:HEAD_DIM + 1, :]
    for t in range(n_qt):
        pair = jnp.concatenate([out[:, lane_tiles[2 * t]], out[:, lane_tiles[2 * t + 1]]], axis=0)
        o_ref[0, t * TILE:(t + 1) * TILE, :] = pair.T.astype(BF16)


def _band_attention(qt, k, vt, kmax, bias, bias_spec, bias_max, offsets, window_bias, q_tiles, pair0, n_pairs,
                    name):
    bsz, _, s = qt.shape
    tq = q_tiles * TILE
    return pl.pallas_call(
        functools.partial(_band_kernel, offsets, window_bias),
        grid=(bsz, n_pairs, s // tq),
        in_specs=[
            pl.BlockSpec((1, 1), lambda b, p, i: (0, 0)),
            pl.BlockSpec((1, kmax.shape[1], 1, PAIR_W), lambda b, p, i: (b, 0, 0, pair0 + p)),
            pl.BlockSpec((1, PAIR_W, tq), lambda b, p, i: (b, pair0 + p, i)),
            pl.BlockSpec((1, s, PAIR_W), lambda b, p, i: (b, 0, pair0 + p)),
            pl.BlockSpec((1, s // TILE, PAIR_W, TILE), lambda b, p, i: (b, 0, pair0 + p, 0)),
            bias_spec,
        ],
        out_specs=pl.BlockSpec((1, tq, PAIR_W), lambda b, p, i: (b, i, p)),
        out_shape=jax.ShapeDtypeStruct((bsz, s, n_pairs * PAIR_W), BF16),
        scratch_shapes=[
            pltpu.VMEM((PAIR_W, 2 * tq), BF16),
            pltpu.VMEM((TILE, 2 * tq), F32),
            pltpu.VMEM((len(offsets), TILE, TILE), F32),
            pltpu.VMEM((HEAD_DIM + SUM_ROWS, 2 * tq), F32),
        ],
        compiler_params=_cparams(("arbitrary", "arbitrary", "arbitrary")),
        name=name,
    )(jnp.reshape(bias_max, (1, 1)).astype(F32), kmax, qt, k, vt, bias)


def _dilated_bias():
    reach = max(w // 2 for w, _ in DIL_PATTERNS)
    n_off = -(-reach // TILE)
    offsets = tuple(range(-n_off, n_off + 1))
    kj = np.arange(TILE)[:, None]
    qi = np.arange(TILE)[None, :]
    tabs = []
    for off in offsets:
        delta = off * TILE + kj - qi
        cnt = np.zeros((TILE, TILE), np.int32)
        for window, dil in DIL_PATTERNS:
            cnt += ((delta % dil == 0) & (np.abs(delta) <= window // 2)).astype(np.int32)
        tabs.append(np.where(cnt > 0, np.log2(np.maximum(cnt, 1).astype(np.float64)), NEG))
    tabs.append(np.full((TILE, TILE), NEG))
    return offsets, jnp.asarray(np.tile(np.stack(tabs), (1, 1, 2)), F32)


def _na_bias_kernel(rows, rpb_ref, o_ref):
    hd = pl.program_id(0)
    kh = min(NA_KH, rows)
    q_rows = TILE // GRID_W
    kc = lax.broadcasted_iota(jnp.int32, (GRID_W, LANES), 0)
    lane = lax.broadcasted_iota(jnp.int32, (GRID_W, LANES), 1)
    qc = lane % GRID_W
    dc = jnp.clip(kc - qc, -(NA_KW - 1), NA_KW - 1) + (NA_KW - 1)
    cs = jnp.clip(qc - NA_KW // 2, 0, GRID_W - NA_KW)
    col_ok = (kc >= cs) & (kc < cs + NA_KW)
    neg = jnp.full((GRID_W, LANES), NEG, F32)
    tiles = []
    for dr in range(2 * NA_KH - 1):
        t = neg
        for j in range(2 * NA_KW - 1):
            t = jnp.where(dc == j, rpb_ref[hd, dr, j] * LOG2E, t)
        tiles.append(jnp.where(col_ok, t, neg))
    for v, r0 in enumerate((0, 2 * q_rows, rows - q_rows)):
        for kr_rel in range(3 * q_rows):
            kr = r0 - q_rows + kr_rel
            for pair in range(q_rows // 2):
                halves = []
                for r in (r0 + 2 * pair, r0 + 2 * pair + 1):
                    rs = min(max(r - kh // 2, 0), rows - kh)
                    halves.append(tiles[kr - r + NA_KH - 1] if rs <= kr < rs + kh else neg)
                blk = jnp.where(lane < GRID_W, halves[0], halves[1])
                o_ref[v, 0, kr_rel * GRID_W:(kr_rel + 1) * GRID_W, pair * LANES:(pair + 1) * LANES] = blk


def _na_bias(rpb, rows):
    n_heads = rpb.shape[0]
    return pl.pallas_call(
        functools.partial(_na_bias_kernel, rows),
        grid=(n_heads,),
        in_specs=[pl.BlockSpec(memory_space=pltpu.SMEM)],
        out_specs=pl.BlockSpec((3, 1, 3 * TILE, TILE), lambda h: (0, h // 2, 0, h % 2)),
        out_shape=jax.ShapeDtypeStruct((3, n_heads // 2, 3 * TILE, 2 * TILE), F32),
        compiler_params=_cparams(("arbitrary",)),
        name="na_bias_table",
    )(rpb)


def _outproj_kernel(x_ref, mod_ref, oa_ref, ob_ref, oc_ref, w_ref, y_ref):
    o = jnp.concatenate([oa_ref[0], ob_ref[0], oc_ref[0]], axis=1)
    mix = jnp.dot(o, w_ref[...], preferred_element_type=F32)
    y_ref[0] = x_ref[0] + mod_ref[0, 2:3, :] * mix


def _out_projection(x, mod_l, o_a, o_b, o_c, w_out, tm):
    bsz, s, d = x.shape
    row = lambda w: pl.BlockSpec((1, tm, w), lambda b, i: (b, i, 0))
    return pl.pallas_call(
        _outproj_kernel,
        grid=(bsz, s // tm),
        in_specs=[
            row(d),
            pl.BlockSpec((1, 6, d), lambda b, i: (b, 0, 0)),
            row(W_DIFF), row(W_DIL), row(W_NA),
            pl.BlockSpec(w_out.shape, lambda b, i: (0, 0)),
        ],
        out_specs=row(d),
        out_shape=jax.ShapeDtypeStruct((bsz, s, d), F32),
        compiler_params=_cparams(("arbitrary", "arbitrary")),
        name="out_proj_residual",
    )(x, mod_l, o_a, o_b, o_c, w_out)


def _ffn_up_kernel(x_ref, xp_ref, xn_ref, mod_ref, g_ref, wg_ref, wu_ref, cw_ref, cb_ref,
                   a_ref, h_s, g_s, u_s):
    i = pl.program_id(1)
    tm = x_ref.shape[1]
    g, sc, sh = g_ref[...], mod_ref[0, 4:5, :], mod_ref[0, 3:4, :]
    keep_prev = (i > 0).astype(F32)
    keep_next = (i < pl.num_programs(1) - 1).astype(F32)
    h_s[0:HALO, :] = (_norm_mod(xp_ref[0], g, sc, sh) * keep_prev).astype(BF16)
    h_s[HALO:HALO + tm, :] = _norm_mod(x_ref[0], g, sc, sh).astype(BF16)
    h_s[HALO + tm:, :] = (_norm_mod(xn_ref[0], g, sc, sh) * keep_next).astype(BF16)
    n_ch = wg_ref.shape[0]

    def matmuls(c):
        g_s[c % 2] = jnp.dot(h_s[...], wg_ref[c], preferred_element_type=F32)
        u_s[c % 2] = jnp.dot(h_s[HALO:HALO + tm, :], wu_ref[c], preferred_element_type=F32)

    def gate(c):
        cw = cw_ref[c]
        gc = cb_ref[c]
        for t in range(CONV_W):
            gc = gc + g_s[c % 2, pl.ds(HALO - CONV_W // 2 + t, tm), :] * cw[t:t + 1, :]
        a_ref[0, :, c * FF_CHUNK:(c + 1) * FF_CHUNK] = (gc * jax.nn.sigmoid(gc) * u_s[c % 2]).astype(BF16)

    matmuls(0)
    for c in range(n_ch):
        if c + 1 < n_ch:
            matmuls(c + 1)
        gate(c)


def _ffn_up(x, mod_l, g, wg, wu, cw, cb, tm):
    bsz, s, d = x.shape
    n_ch = wg.shape[0]
    hb = tm // HALO
    n_hb = s // HALO
    return pl.pallas_call(
        _ffn_up_kernel,
        grid=(bsz, s // tm),
        in_specs=[
            pl.BlockSpec((1, tm, d), lambda b, i: (b, i, 0)),
            pl.BlockSpec((1, HALO, d), lambda b, i: (b, jnp.maximum(i * hb - 1, 0), 0)),
            pl.BlockSpec((1, HALO, d), lambda b, i: (b, jnp.minimum((i + 1) * hb, n_hb - 1), 0)),
            pl.BlockSpec((1, 6, d), lambda b, i: (b, 0, 0)),
            pl.BlockSpec((1, d), lambda b, i: (0, 0)),
            pl.BlockSpec(wg.shape, lambda b, i: (0, 0, 0)),
            pl.BlockSpec(wu.shape, lambda b, i: (0, 0, 0)),
            pl.BlockSpec(cw.shape, lambda b, i: (0, 0, 0)),
            pl.BlockSpec(cb.shape, lambda b, i: (0, 0, 0)),
        ],
        out_specs=pl.BlockSpec((1, tm, n_ch * FF_CHUNK), lambda b, i: (b, i, 0)),
        out_shape=jax.ShapeDtypeStruct((bsz, s, n_ch * FF_CHUNK), BF16),
        scratch_shapes=[
            pltpu.VMEM((tm + 2 * HALO, d), BF16),
            pltpu.VMEM((2, tm + 2 * HALO, FF_CHUNK), F32),
            pltpu.VMEM((2, tm, FF_CHUNK), F32),
        ],
        compiler_params=_cparams(("arbitrary", "arbitrary")),
        name="ffn_up_conv_glu",
    )(x, x, x, mod_l, g.reshape(1, d), wg, wu, cw, cb)


def _ffn_down_kernel(final, x_ref, mod_ref, a_ref, w_ref, gf_ref, y_ref):
    y = x_ref[0] + mod_ref[0, 5:6, :] * jnp.dot(a_ref[0], w_ref[...], preferred_element_type=F32)
    if final:
        ms = jnp.mean(y * y, axis=-1, keepdims=True)
        y = (y * lax.rsqrt(ms + EPS)) * gf_ref[...]
    y_ref[0] = y


def _ffn_down(x, mod_l, a, w_down, g_final, final, tm):
    bsz, s, d = x.shape
    return pl.pallas_call(
        functools.partial(_ffn_down_kernel, final),
        grid=(bsz, s // tm),
        in_specs=[
            pl.BlockSpec((1, tm, d), lambda b, i: (b, i, 0)),
            pl.BlockSpec((1, 6, d), lambda b, i: (b, 0, 0)),
            pl.BlockSpec((1, tm, a.shape[2]), lambda b, i: (b, i, 0)),
            pl.BlockSpec(w_down.shape, lambda b, i: (0, 0)),
            pl.BlockSpec((1, d), lambda b, i: (0, 0)),
        ],
        out_specs=pl.BlockSpec((1, tm, d), lambda b, i: (b, i, 0)),
        out_shape=jax.ShapeDtypeStruct((bsz, s, d), F32),
        compiler_params=_cparams(("arbitrary", "arbitrary")),
        name="ffn_down_residual",
    )(x, mod_l, a, w_down, g_final.reshape(1, d))


def kernel(x, c, w_ada, b_ada, g_attn, w_in, diff_lambda, diff_subln, na_rpb, w_out, g_ffn, w_up,
           conv_w, conv_b, w_down, g_final):
    bsz, s, d = x.shape
    depth = w_ada.shape[0]
    d_ff = w_down.shape[1]
    n_ch = d_ff // FF_CHUNK
    tm = 512
    assert s % tm == 0 and tm % TILE == 0 and d_ff % FF_CHUNK == 0 and TILE % (2 * GRID_W) == 0
    assert (s // TILE) % max(DIL_Q_TILES, NA_Q_TILES) == 0 and s // GRID_W >= 4 * (TILE // GRID_W)

    mod = _modulation(c, w_ada, b_ada).reshape(depth, bsz, 6, d)
    tables = _rope_tables(s, DIFF_QK_DIM) + _rope_tables(s, HEAD_DIM)
    dil_offsets, dil_bias = _dilated_bias()
    rows = s // GRID_W

    wa, wb, wc = W_DIFF, W_DIL, W_NA
    offs = np.cumsum([0, wa, wa, wa, wb, wb, wb, wc, wc, wc])
    order = (0, 3, 6, 1, 4, 7, 2, 5, 8)

    for l in range(depth):
        w_perm = jnp.concatenate([w_in[l][:, offs[j]:offs[j + 1]] for j in order], axis=1).astype(BF16)
        qt, k, vt, kmax = _projection(x, mod[l], g_attn[l], w_perm, tables, tm)

        o_a = _diff_attention(qt, k, vt, kmax, diff_lambda[l], diff_subln[l], l, 2 * TILE)
        o_b = _band_attention(
            qt, k, vt, kmax, dil_bias,
            pl.BlockSpec(dil_bias.shape, lambda b, p, i: (0, 0, 0)),
            jnp.float32(math.log2(len(DIL_PATTERNS))),
            dil_offsets, False, DIL_Q_TILES, N_HEADS_DIFF // 2, N_HEADS_DIL // 2, "dilated_attention")
        o_c = _band_attention(
            qt, k, vt, kmax, _na_bias(na_rpb[l], rows),
            pl.BlockSpec((3, 1, 3 * TILE, 2 * TILE), lambda b, p, i: (0, p, 0, 0)),
            jnp.max(na_rpb[l]) * LOG2E,
            (-1, 0, 1), True, NA_Q_TILES, (N_HEADS_DIFF + N_HEADS_DIL) // 2, N_HEADS_NA // 2,
            "neighbourhood_attention")

        x = _out_projection(x, mod[l], o_a, o_b, o_c, w_out[l].astype(BF16), tm)

        wg = w_up[l][:, :d_ff].reshape(d, n_ch, FF_CHUNK).transpose(1, 0, 2).astype(BF16)
        wu = w_up[l][:, d_ff:].reshape(d, n_ch, FF_CHUNK).transpose(1, 0, 2).astype(BF16)
        cw = conv_w[l].reshape(CONV_W, n_ch, FF_CHUNK).transpose(1, 0, 2)
        cb = conv_b[l].reshape(n_ch, 1, FF_CHUNK)
        a = _ffn_up(x, mod[l], g_ffn[l], wg, wu, cw, cb, tm)
        x = _ffn_down(x, mod[l], a, w_down[l].astype(BF16), g_final, l == depth - 1, tm)
    return x
```

```python
import functools
import math

import numpy as np
import jax
import jax.numpy as jnp
from jax import lax
from jax.experimental import pallas as pl
from jax.experimental.pallas import tpu as pltpu

F32 = jnp.float32
BF16 = jnp.bfloat16

HEAD_DIM = 64
N_HEADS_DIFF = 4
N_HEADS_DIL = 6
N_HEADS_NA = 6
W_DIFF = N_HEADS_DIFF * HEAD_DIM
W_DIL = N_HEADS_DIL * HEAD_DIM
W_NA = N_HEADS_NA * HEAD_DIM
DIFF_QK_DIM = HEAD_DIM // 2
DIL_PATTERNS = ((128, 1), (512, 4), (2048, 16))
GRID_W = 64
NA_KH = 8
NA_KW = 16
CONV_W = 3
ROPE_THETA = 10000.0
EPS = 1e-6
NEG = -1e30

LANES = 128
PAIR_W = 2 * HEAD_DIM
TILE = 256
FF_CHUNK = 256
HALO = 16
VMEM_LIMIT = 56 * 1024 * 1024
LOG2E = math.log2(math.e)
SUM_ROWS = 16
LAG_MARGIN = 64.0
DIL_Q_TILES = 8
NA_Q_TILES = 8
LAG_UNROLL = 63
QK_AHEAD = 4


def _cparams(sem):
    return pltpu.CompilerParams(dimension_semantics=sem, vmem_limit_bytes=VMEM_LIMIT)


def _mod_kernel(ct_ref, w_ref, b_ref, o_ref):
    ct = ct_ref[...]
    s = ct * jax.nn.sigmoid(ct)
    w = w_ref[0]
    rows = [jnp.sum(w * s[:, b:b + 1], axis=0, keepdims=True) for b in range(ct.shape[1])]
    o_ref[0] = jnp.concatenate(rows, axis=0) + b_ref[0]


def _modulation(c, w_ada, b_ada):
    depth, d, n = w_ada.shape
    bsz = c.shape[0]
    tn = 768
    return pl.pallas_call(
        _mod_kernel,
        grid=(depth, n // tn),
        in_specs=[
            pl.BlockSpec((d, bsz), lambda l, j: (0, 0)),
            pl.BlockSpec((1, d, tn), lambda l, j: (l, 0, j)),
            pl.BlockSpec((1, 1, tn), lambda l, j: (l, 0, j)),
        ],
        out_specs=pl.BlockSpec((1, bsz, tn), lambda l, j: (l, 0, j)),
        out_shape=jax.ShapeDtypeStruct((depth, bsz, n), F32),
        compiler_params=_cparams(("arbitrary", "arbitrary")),
        name="adaln_mod",
    )(c.T, w_ada, b_ada.reshape(depth, 1, n))


def _norm_mod(x, g, sc, sh):
    ms = jnp.mean(x * x, axis=-1, keepdims=True)
    return (x * lax.rsqrt(ms + EPS)) * g * (1.0 + sc) + sh


def _rope_slab(acc, cos, sin, group, width):
    half = group // 2
    lane = lax.broadcasted_iota(jnp.int32, (1, LANES), 1) % group
    first = lane < half
    outs = []
    for c in range(width // LANES):
        xc = acc[:, c * LANES:(c + 1) * LANES]
        swapped = jnp.where(first, pltpu.roll(xc, LANES - half, axis=1), pltpu.roll(xc, half, axis=1))
        outs.append(xc * cos + swapped * sin)
    if width < acc.shape[1]:
        outs.append(acc[:, width:])
    return jnp.concatenate(outs, axis=1)


def _proj_kernel(x_ref, mod_ref, g_ref, w_ref, ca_ref, sa_ref, cb_ref, sb_ref,
                 qt_ref, k_ref, vt_ref, kmax_ref):
    tm = x_ref.shape[1]
    h = _norm_mod(x_ref[0], g_ref[...], mod_ref[0, 1:2, :], mod_ref[0, 0:1, :]).astype(BF16)
    ca, sa, cb, sb = ca_ref[...], sa_ref[...], cb_ref[...], sb_ref[...]
    width = W_DIFF + W_DIL + W_NA
    slabs = ((0, W_DIFF), (W_DIFF, W_DIL + W_NA))

    def slab(part, si):
        off, w = slabs[si]
        acc = jnp.dot(h, w_ref[:, part * width + off: part * width + off + w], preferred_element_type=F32)
        if part < 2 and si == 0:
            acc = _rope_slab(acc, ca, sa, DIFF_QK_DIM, W_DIFF)
        elif part < 2 and si == 1:
            acc = _rope_slab(acc, cb, sb, HEAD_DIM, W_DIL)
        return acc

    q_scale = (LOG2E * DIFF_QK_DIM ** -0.5, LOG2E * HEAD_DIM ** -0.5)
    for si, (off, w) in enumerate(slabs):
        q = slab(0, si) * q_scale[si]
        qt_ref[0, off:off + w, :] = q.T.astype(BF16)
        kb = slab(1, si).astype(BF16)
        k_ref[0, :, off:off + w] = kb
        kmax_ref[0, 0, :, off:off + w] = jnp.max(jnp.abs(kb.astype(F32)), axis=0, keepdims=True)
        v = slab(2, si)
        for t in range(tm // TILE):
            vt_ref[0, t, off:off + w, :] = v[t * TILE:(t + 1) * TILE, :].T.astype(BF16)


def _rope_tables(s, group):
    half = group // 2
    inv = ROPE_THETA ** (-jnp.arange(half, dtype=F32) / half)
    ang = jnp.arange(s, dtype=F32)[:, None] * inv
    cos, sin = jnp.cos(ang), jnp.sin(ang)
    reps = LANES // group
    return (jnp.tile(jnp.concatenate([cos, cos], axis=1), (1, reps)),
            jnp.tile(jnp.concatenate([-sin, sin], axis=1), (1, reps)))


def _projection(x, mod_l, g, w_perm, tables, tm):
    bsz, s, d = x.shape
    width = W_DIFF + W_DIL + W_NA
    tab_spec = pl.BlockSpec((tm, LANES), lambda b, i: (i, 0))
    return pl.pallas_call(
        _proj_kernel,
        grid=(bsz, s // tm),
        in_specs=[
            pl.BlockSpec((1, tm, d), lambda b, i: (b, i, 0)),
            pl.BlockSpec((1, 6, d), lambda b, i: (b, 0, 0)),
            pl.BlockSpec((1, d), lambda b, i: (0, 0)),
            pl.BlockSpec((d, 3 * width), lambda b, i: (0, 0)),
            tab_spec, tab_spec, tab_spec, tab_spec,
        ],
        out_specs=[
            pl.BlockSpec((1, width, tm), lambda b, i: (b, 0, i)),
            pl.BlockSpec((1, tm, width), lambda b, i: (b, i, 0)),
            pl.BlockSpec((1, tm // TILE, width, TILE), lambda b, i: (b, i, 0, 0)),
            pl.BlockSpec((1, 1, 1, width), lambda b, i: (b, i, 0, 0)),
        ],
        out_shape=[
            jax.ShapeDtypeStruct((bsz, width, s), BF16),
            jax.ShapeDtypeStruct((bsz, s, width), BF16),
            jax.ShapeDtypeStruct((bsz, s // TILE, width, TILE), BF16),
            jax.ShapeDtypeStruct((bsz, s // tm, 1, width), F32),
        ],
        compiler_params=_cparams(("arbitrary", "arbitrary")),
        name="norm_proj_rope",
    )(x, mod_l, g.reshape(1, d), w_perm, *tables)


def _masked_queries(qt, n_split):
    row = lax.broadcasted_iota(jnp.int32, (PAIR_W, 1), 0)
    step = PAIR_W // n_split
    return [jnp.where((row >= j * step) & (row < (j + 1) * step), qt, jnp.zeros_like(qt))
            for j in range(n_split)]


def _v_with_ones(vtb, hd):
    ones = jnp.ones((SUM_ROWS, vtb.shape[1]), BF16)
    return jnp.concatenate([vtb[hd * HEAD_DIM:(hd + 1) * HEAD_DIM, :], ones], axis=0)


def _score_bound(kmax_ref, q_s):
    kmax = jnp.max(kmax_ref[0], axis=0) * (1.0 + 2.0 ** -7)
    kmax = jnp.broadcast_to(kmax, (SUM_ROWS, PAIR_W)).astype(BF16)
    return jnp.dot(kmax, jnp.abs(q_s[...]), preferred_element_type=F32)[0:1] * (1.0 + 2.0 ** -7)


def _diff_kernel(lam_init, dl_ref, g_ref, kmax_ref, qt_ref, k_ref, vt_ref, o_ref, q_s, s_s, m_s, acc_s):
    tq = qt_ref.shape[2]
    n_kt = vt_ref.shape[1]
    for j, qm in enumerate(_masked_queries(qt_ref[0], 4)):
        q_s[:, j * tq:(j + 1) * tq] = qm
    lane_tiles = [slice(nt * TILE, (nt + 1) * TILE) for nt in range(4 * tq // TILE)]

    def k_tile(kt):
        return k_ref[0, pl.ds(pl.multiple_of(kt * TILE, TILE), TILE), :]

    def v_tiles(kt):
        vtb = vt_ref[0, kt]
        return [_v_with_ones(vtb, hd) for hd in range(2)]

    def exact_step(kt, first):
        kb, v_ext = k_tile(kt), v_tiles(kt)
        for ln in lane_tiles:
            s_s[:, ln] = jnp.dot(kb, q_s[:, ln], preferred_element_type=F32)
        for ln in lane_tiles:
            tile_max = jnp.max(s_s[:, ln], axis=0, keepdims=True)
            m_new = tile_max if first else jnp.maximum(m_s[:, ln], tile_max)
            e = jnp.exp2(s_s[:, ln] - m_new).astype(BF16)
            pv = jnp.dot(v_ext[ln.start // (2 * tq)], e, preferred_element_type=F32)
            acc_s[:, ln] = pv if first else acc_s[:, ln] * jnp.exp2(m_s[:, ln] - m_new) + pv
            m_s[:, ln] = m_new

    def fixed_shift_steps(kts):
        def finish(vtb, ln, s):
            hd = ln.start // (2 * tq)
            e = jnp.exp2(s - m_s[:, ln])
            acc_s[0:HEAD_DIM, ln] += jnp.dot(vtb[hd * HEAD_DIM:(hd + 1) * HEAD_DIM, :], e.astype(BF16),
                                             preferred_element_type=F32)
            acc_s[HEAD_DIM:HEAD_DIM + 1, ln] += jnp.sum(e, axis=0, keepdims=True)

        pending = []
        for kt in kts:
            kb, vtb = k_tile(kt), vt_ref[0, kt]
            for ln in lane_tiles:
                pending.append((vtb, ln, jnp.dot(kb, q_s[:, ln], preferred_element_type=F32)))
                if len(pending) > QK_AHEAD:
                    finish(*pending.pop(0))
        for item in pending:
            finish(*item)

    exact_step(0, True)
    lag_ok = jnp.max(_score_bound(kmax_ref, q_s) - m_s[...]) < LAG_MARGIN

    @pl.when(lag_ok)
    def _():
        def body(tt, carry):
            fixed_shift_steps([LAG_UNROLL * tt + 1 + u for u in range(LAG_UNROLL)])
            return carry
        n_trips = (n_kt - 1) // LAG_UNROLL
        lax.fori_loop(0, n_trips, body, 0)
        if n_trips * LAG_UNROLL + 1 < n_kt:
            fixed_shift_steps(list(range(n_trips * LAG_UNROLL + 1, n_kt)))

    @pl.when(jnp.logical_not(lag_ok))
    def _():
        def body(kt, carry):
            exact_step(kt, False)
            return carry
        lax.fori_loop(1, n_kt, body, 0)

    dl = dl_ref[...]
    lam = (jnp.exp(jnp.sum(dl[0:1] * dl[1:2], axis=1, keepdims=True))
           - jnp.exp(jnp.sum(dl[2:3] * dl[3:4], axis=1, keepdims=True)) + lam_init)
    g = g_ref[...]
    outs = []
    for hd in range(2):
        p = [acc_s[0:HEAD_DIM, (2 * hd + mp) * tq:(2 * hd + mp + 1) * tq]
             / acc_s[HEAD_DIM:HEAD_DIM + 1, (2 * hd + mp) * tq:(2 * hd + mp + 1) * tq] for mp in range(2)]
        o = p[0] - lam * p[1]
        ms = jnp.mean(o * o, axis=0, keepdims=True)
        outs.append((o * lax.rsqrt(ms + EPS)) * g * (1.0 - lam_init))
    o_ref[0] = jnp.concatenate(outs, axis=0).T.astype(BF16)


def _diff_attention(qt, k, vt, kmax, diff_lambda, subln, layer_idx, tq):
    bsz, _, s = qt.shape
    n_pairs = N_HEADS_DIFF // 2
    lam_init = 0.8 - 0.6 * math.exp(-0.3 * layer_idx)
    return pl.pallas_call(
        functools.partial(_diff_kernel, lam_init),
        grid=(bsz, n_pairs, s // tq),
        in_specs=[
            pl.BlockSpec(diff_lambda.shape, lambda b, p, i: (0, 0)),
            pl.BlockSpec((HEAD_DIM, 1), lambda b, p, i: (0, 0)),
            pl.BlockSpec((1, kmax.shape[1], 1, PAIR_W), lambda b, p, i: (b, 0, 0, p)),
            pl.BlockSpec((1, PAIR_W, tq), lambda b, p, i: (b, p, i)),
            pl.BlockSpec((1, s, PAIR_W), lambda b, p, i: (b, 0, p)),
            pl.BlockSpec((1, s // TILE, PAIR_W, TILE), lambda b, p, i: (b, 0, p, 0)),
        ],
        out_specs=pl.BlockSpec((1, tq, PAIR_W), lambda b, p, i: (b, i, p)),
        out_shape=jax.ShapeDtypeStruct((bsz, s, W_DIFF), BF16),
        scratch_shapes=[
            pltpu.VMEM((PAIR_W, 4 * tq), BF16),
            pltpu.VMEM((TILE, 4 * tq), F32),
            pltpu.VMEM((1, 4 * tq), F32),
            pltpu.VMEM((HEAD_DIM + SUM_ROWS, 4 * tq), F32),
        ],
        compiler_params=_cparams(("arbitrary", "arbitrary", "arbitrary")),
        name="diff_attention",
    )(diff_lambda, subln.reshape(HEAD_DIM, 1), kmax, qt, k, vt)


def _band_kernel(offsets, window_bias, bmax_ref, kmax_ref, qt_ref, k_ref, vt_ref, bias_ref, o_ref,
                 q_s, sc_s, s_s, acc_s):
    i = pl.program_id(2)
    n_blk = vt_ref.shape[1]
    n_off = len(offsets)
    centre = offsets.index(0)
    n_qt = qt_ref.shape[2] // TILE
    qt = qt_ref[0]
    for t in range(n_qt):
        for hd, qm in enumerate(_masked_queries(qt[:, t * TILE:(t + 1) * TILE], 2)):
            q_s[:, (2 * t + hd) * TILE:(2 * t + hd + 1) * TILE] = qm
    lane_tiles = [slice(c * TILE, (c + 1) * TILE) for c in range(2 * n_qt)]

    def q_tile(ln):
        return i * n_qt + ln.start // (2 * TILE)

    def key_tile(oi, ln):
        return jnp.clip(q_tile(ln) + offsets[oi], 0, n_blk - 1)

    def scores(oi, ln):
        qi = q_tile(ln)
        head_lanes = slice(ln.start % (2 * TILE), ln.start % (2 * TILE) + TILE)
        if window_bias:
            variant = jnp.where(qi == 0, 0, jnp.where(qi == n_blk - 1, 2, 1))
            bias = bias_ref[variant, 0, oi * TILE:(oi + 1) * TILE, head_lanes]
        else:
            kbi = qi + offsets[oi]
            bias = bias_ref[jnp.where((kbi >= 0) & (kbi < n_blk), oi, n_off), :, head_lanes]
        kb = k_ref[0, pl.ds(pl.multiple_of(key_tile(oi, ln) * TILE, TILE), TILE), :]
        return jnp.dot(kb, q_s[:, ln], preferred_element_type=F32) + bias

    def weighted_values(oi, ln, e):
        vtb = vt_ref[0, key_tile(oi, ln)]
        return jnp.dot(_v_with_ones(vtb, (ln.start // TILE) % 2), e, preferred_element_type=F32)

    for ln in lane_tiles:
        sc_s[:, ln] = scores(centre, ln)
    m_c = jnp.max(sc_s[...], axis=0, keepdims=True)
    shift_ok = jnp.max(_score_bound(kmax_ref, q_s) + bmax_ref[...] - m_c) < LAG_MARGIN

    @pl.when(shift_ok)
    def _():
        acc = {ln.start: None for ln in lane_tiles}

        def finish(oi, ln, s):
            pv = weighted_values(oi, ln, jnp.exp2(s - m_c[:, ln]).astype(BF16))
            acc[ln.start] = pv if acc[ln.start] is None else acc[ln.start] + pv

        pending = []
        for oi in range(n_off):
            for ln in lane_tiles:
                pending.append((oi, ln, sc_s[:, ln] if oi == centre else scores(oi, ln)))
                if len(pending) > QK_AHEAD:
                    finish(*pending.pop(0))
        for item in pending:
            finish(*item)
        for ln in lane_tiles:
            acc_s[:, ln] = acc[ln.start]

    @pl.when(jnp.logical_not(shift_ok))
    def _():
        for ln in lane_tiles:
            col_max = m_c[:, ln]
            for oi in range(n_off):
                s_s[oi] = sc_s[:, ln] if oi == centre else scores(oi, ln)
                col_max = jnp.maximum(col_max, jnp.max(s_s[oi], axis=0, keepdims=True))
            acc = None
            for oi in range(n_off):
                pv = weighted_values(oi, ln, jnp.exp2(s_s[oi] - col_max).astype(BF16))
                acc = pv if acc is None else acc + pv
            acc_s[:, ln] = acc

    out = acc_s[0:HEAD_DIM, :] / acc_s[HEAD_DIM:HEAD_DIM + 1, :]
    for t in range(n_qt):
        pair = jnp.concatenate([out[:, lane_tiles[2 * t]], out[:, lane_tiles[2 * t + 1]]], axis=0)
        o_ref[0, t * TILE:(t + 1) * TILE, :] = pair.T.astype(BF16)


def _band_attention(qt, k, vt, kmax, bias, bias_spec, bias_max, offsets, window_bias, q_tiles, pair0, n_pairs,
                    name):
    bsz, _, s = qt.shape
    tq = q_tiles * TILE
    return pl.pallas_call(
        functools.partial(_band_kernel, offsets, window_bias),
        grid=(bsz, n_pairs, s // tq),
        in_specs=[
            pl.BlockSpec((1, 1), lambda b, p, i: (0, 0)),
            pl.BlockSpec((1, kmax.shape[1], 1, PAIR_W), lambda b, p, i: (b, 0, 0, pair0 + p)),
            pl.BlockSpec((1, PAIR_W, tq), lambda b, p, i: (b, pair0 + p, i)),
            pl.BlockSpec((1, s, PAIR_W), lambda b, p, i: (b, 0, pair0 + p)),
            pl.BlockSpec((1, s // TILE, PAIR_W, TILE), lambda b, p, i: (b, 0, pair0 + p, 0)),
            bias_spec,
        ],
        out_specs=pl.BlockSpec((1, tq, PAIR_W), lambda b, p, i: (b, i, p)),
        out_shape=jax.ShapeDtypeStruct((bsz, s, n_pairs * PAIR_W), BF16),
        scratch_shapes=[
            pltpu.VMEM((PAIR_W, 2 * tq), BF16),
            pltpu.VMEM((TILE, 2 * tq), F32),
            pltpu.VMEM((len(offsets), TILE, TILE), F32),
            pltpu.VMEM((HEAD_DIM + SUM_ROWS, 2 * tq), F32),
        ],
        compiler_params=_cparams(("arbitrary", "arbitrary", "arbitrary")),
        name=name,
    )(jnp.reshape(bias_max, (1, 1)).astype(F32), kmax, qt, k, vt, bias)


def _dilated_bias():
    reach = max(w // 2 for w, _ in DIL_PATTERNS)
    n_off = -(-reach // TILE)
    offsets = tuple(range(-n_off, n_off + 1))
    kj = np.arange(TILE)[:, None]
    qi = np.arange(TILE)[None, :]
    tabs = []
    for off in offsets:
        delta = off * TILE + kj - qi
        cnt = np.zeros((TILE, TILE), np.int32)
        for window, dil in DIL_PATTERNS:
            cnt += ((delta % dil == 0) & (np.abs(delta) <= window // 2)).astype(np.int32)
        tabs.append(np.where(cnt > 0, np.log2(np.maximum(cnt, 1).astype(np.float64)), NEG))
    tabs.append(np.full((TILE, TILE), NEG))
    return offsets, jnp.asarray(np.tile(np.stack(tabs), (1, 1, 2)), F32)


def _na_bias_kernel(rows, rpb_ref, o_ref):
    hd = pl.program_id(0)
    kh = min(NA_KH, rows)
    q_rows = TILE // GRID_W
    kc = lax.broadcasted_iota(jnp.int32, (GRID_W, LANES), 0)
    lane = lax.broadcasted_iota(jnp.int32, (GRID_W, LANES), 1)
    qc = lane % GRID_W
    dc = jnp.clip(kc - qc, -(NA_KW - 1), NA_KW - 1) + (NA_KW - 1)
    cs = jnp.clip(qc - NA_KW // 2, 0, GRID_W - NA_KW)
    col_ok = (kc >= cs) & (kc < cs + NA_KW)
    neg = jnp.full((GRID_W, LANES), NEG, F32)
    tiles = []
    for dr in range(2 * NA_KH - 1):
        t = neg
        for j in range(2 * NA_KW - 1):
            t = jnp.where(dc == j, rpb_ref[hd, dr, j] * LOG2E, t)
        tiles.append(jnp.where(col_ok, t, neg))
    for v, r0 in enumerate((0, 2 * q_rows, rows - q_rows)):
        for kr_rel in range(3 * q_rows):
            kr = r0 - q_rows + kr_rel
            for pair in range(q_rows // 2):
                halves = []
                for r in (r0 + 2 * pair, r0 + 2 * pair + 1):
                    rs = min(max(r - kh // 2, 0), rows - kh)
                    halves.append(tiles[kr - r + NA_KH - 1] if rs <= kr < rs + kh else neg)
                blk = jnp.where(lane < GRID_W, halves[0], halves[1])
                o_ref[v, 0, kr_rel * GRID_W:(kr_rel + 1) * GRID_W, pair * LANES:(pair + 1) * LANES] = blk


def _na_bias(rpb, rows):
    n_heads = rpb.shape[0]
    return pl.pallas_call(
        functools.partial(_na_bias_kernel, rows),
        grid=(n_heads,),
        in_specs=[pl.BlockSpec(memory_space=pltpu.SMEM)],
        out_specs=pl.BlockSpec((3, 1, 3 * TILE, TILE), lambda h: (0, h // 2, 0, h % 2)),
        out_shape=jax.ShapeDtypeStruct((3, n_heads // 2, 3 * TILE, 2 * TILE), F32),
        compiler_params=_cparams(("arbitrary",)),
        name="na_bias_table",
    )(rpb)


def _outproj_kernel(x_ref, mod_ref, oa_ref, ob_ref, oc_ref, w_ref, y_ref):
    o = jnp.concatenate([oa_ref[0], ob_ref[0], oc_ref[0]], axis=1)
    mix = jnp.dot(o, w_ref[...], preferred_element_type=F32)
    y_ref[0] = x_ref[0] + mod_ref[0, 2:3, :] * mix


def _out_projection(x, mod_l, o_a, o_b, o_c, w_out, tm):
    bsz, s, d = x.shape
    row = lambda w: pl.BlockSpec((1, tm, w), lambda b, i: (b, i, 0))
    return pl.pallas_call(
        _outproj_kernel,
        grid=(bsz, s // tm),
        in_specs=[
            row(d),
            pl.BlockSpec((1, 6, d), lambda b, i: (b, 0, 0)),
            row(W_DIFF), row(W_DIL), row(W_NA),
            pl.BlockSpec(w_out.shape, lambda b, i: (0, 0)),
        ],
        out_specs=row(d),
        out_shape=jax.ShapeDtypeStruct((bsz, s, d), F32),
        compiler_params=_cparams(("arbitrary", "arbitrary")),
        name="out_proj_residual",
    )(x, mod_l, o_a, o_b, o_c, w_out)


def _ffn_up_kernel(x_ref, xp_ref, xn_ref, mod_ref, g_ref, wg_ref, wu_ref, cw_ref, cb_ref,
                   a_ref, h_s, g_s, u_s):
    i = pl.program_id(1)
    tm = x_ref.shape[1]
    g, sc, sh = g_ref[...], mod_ref[0, 4:5, :], mod_ref[0, 3:4, :]
    keep_prev = (i > 0).astype(F32)
    keep_next = (i < pl.num_programs(1) - 1).astype(F32)
    h_s[0:HALO, :] = (_norm_mod(xp_ref[0], g, sc, sh) * keep_prev).astype(BF16)
    h_s[HALO:HALO + tm, :] = _norm_mod(x_ref[0], g, sc, sh).astype(BF16)
    h_s[HALO + tm:, :] = (_norm_mod(xn_ref[0], g, sc, sh) * keep_next).astype(BF16)
    n_ch = wg_ref.shape[0]

    def matmuls(c):
        g_s[c % 2] = jnp.dot(h_s[...], wg_ref[c], preferred_element_type=F32)
        u_s[c % 2] = jnp.dot(h_s[HALO:HALO + tm, :], wu_ref[c], preferred_element_type=F32)

    def gate(c):
        cw = cw_ref[c]
        gc = cb_ref[c]
        for t in range(CONV_W):
            gc = gc + g_s[c % 2, pl.ds(HALO - CONV_W // 2 + t, tm), :] * cw[t:t + 1, :]
        a_ref[0, :, c * FF_CHUNK:(c + 1) * FF_CHUNK] = (gc * jax.nn.sigmoid(gc) * u_s[c % 2]).astype(BF16)

    matmuls(0)
    for c in range(n_ch):
        if c + 1 < n_ch:
            matmuls(c + 1)
        gate(c)


def _ffn_up(x, mod_l, g, wg, wu, cw, cb, tm):
    bsz, s, d = x.shape
    n_ch = wg.shape[0]
    hb = tm // HALO
    n_hb = s // HALO
    return pl.pallas_call(
        _ffn_up_kernel,
        grid=(bsz, s // tm),
        in_specs=[
            pl.BlockSpec((1, tm, d), lambda b, i: (b, i, 0)),
            pl.BlockSpec((1, HALO, d), lambda b, i: (b, jnp.maximum(i * hb - 1, 0), 0)),
            pl.BlockSpec((1, HALO, d), lambda b, i: (b, jnp.minimum((i + 1) * hb, n_hb - 1), 0)),
            pl.BlockSpec((1, 6, d), lambda b, i: (b, 0, 0)),
            pl.BlockSpec((1, d), lambda b, i: (0, 0)),
            pl.BlockSpec(wg.shape, lambda b, i: (0, 0, 0)),
            pl.BlockSpec(wu.shape, lambda b, i: (0, 0, 0)),
            pl.BlockSpec(cw.shape, lambda b, i: (0, 0, 0)),
            pl.BlockSpec(cb.shape, lambda b, i: (0, 0, 0)),
        ],
        out_specs=pl.BlockSpec((1, tm, n_ch * FF_CHUNK), lambda b, i: (b, i, 0)),
        out_shape=jax.ShapeDtypeStruct((bsz, s, n_ch * FF_CHUNK), BF16),
        scratch_shapes=[
            pltpu.VMEM((tm + 2 * HALO, d), BF16),
            pltpu.VMEM((2, tm + 2 * HALO, FF_CHUNK), F32),
            pltpu.VMEM((2, tm, FF_CHUNK), F32),
        ],
        compiler_params=_cparams(("arbitrary", "arbitrary")),
        name="ffn_up_conv_glu",
    )(x, x, x, mod_l, g.reshape(1, d), wg, wu, cw, cb)


def _ffn_down_kernel(final, x_ref, mod_ref, a_ref, w_ref, gf_ref, y_ref):
    y = x_ref[0] + mod_ref[0, 5:6, :] * jnp.dot(a_ref[0], w_ref[...], preferred_element_type=F32)
    if final:
        ms = jnp.mean(y * y, axis=-1, keepdims=True)
        y = (y * lax.rsqrt(ms + EPS)) * gf_ref[...]
    y_ref[0] = y


def _ffn_down(x, mod_l, a, w_down, g_final, final, tm):
    bsz, s, d = x.shape
    return pl.pallas_call(
        functools.partial(_ffn_down_kernel, final),
        grid=(bsz, s // tm),
        in_specs=[
            pl.BlockSpec((1, tm, d), lambda b, i: (b, i, 0)),
            pl.BlockSpec((1, 6, d), lambda b, i: (b, 0, 0)),
            pl.BlockSpec((1, tm, a.shape[2]), lambda b, i: (b, i, 0)),
            pl.BlockSpec(w_down.shape, lambda b, i: (0, 0)),
            pl.BlockSpec((1, d), lambda b, i: (0, 0)),
        ],
        out_specs=pl.BlockSpec((1, tm, d), lambda b, i: (b, i, 0)),
        out_shape=jax.ShapeDtypeStruct((bsz, s, d), F32),
        compiler_params=_cparams(("arbitrary", "arbitrary")),
        name="ffn_down_residual",
    )(x, mod_l, a, w_down, g_final.reshape(1, d))


def kernel(x, c, w_ada, b_ada, g_attn, w_in, diff_lambda, diff_subln, na_rpb, w_out, g_ffn, w_up,
           conv_w, conv_b, w_down, g_final):
    bsz, s, d = x.shape
    depth = w_ada.shape[0]
    d_ff = w_down.shape[1]
    n_ch = d_ff // FF_CHUNK
    tm = 512
    assert s % tm == 0 and tm % TILE == 0 and d_ff % FF_CHUNK == 0 and TILE % (2 * GRID_W) == 0
    assert (s // TILE) % max(DIL_Q_TILES, NA_Q_TILES) == 0 and s // GRID_W >= 4 * (TILE // GRID_W)

    mod = _modulation(c, w_ada, b_ada).reshape(depth, bsz, 6, d)
    tables = _rope_tables(s, DIFF_QK_DIM) + _rope_tables(s, HEAD_DIM)
    dil_offsets, dil_bias = _dilated_bias()
    rows = s // GRID_W

    wa, wb, wc = W_DIFF, W_DIL, W_NA
    offs = np.cumsum([0, wa, wa, wa, wb, wb, wb, wc, wc, wc])
    order = (0, 3, 6, 1, 4, 7, 2, 5, 8)

    for l in range(depth):
        w_perm = jnp.concatenate([w_in[l][:, offs[j]:offs[j + 1]] for j in order], axis=1).astype(BF16)
        qt, k, vt, kmax = _projection(x, mod[l], g_attn[l], w_perm, tables, tm)

        o_a = _diff_attention(qt, k, vt, kmax, diff_lambda[l], diff_subln[l], l, 2 * TILE)
        o_b = _band_attention(
            qt, k, vt, kmax, dil_bias,
            pl.BlockSpec(dil_bias.shape, lambda b, p, i: (0, 0, 0)),
            jnp.float32(math.log2(len(DIL_PATTERNS))),
            dil_offsets, False, DIL_Q_TILES, N_HEADS_DIFF // 2, N_HEADS_DIL // 2, "dilated_attention")
        o_c = _band_attention(
            qt, k, vt, kmax, _na_bias(na_rpb[l], rows),
            pl.BlockSpec((3, 1, 3 * TILE, 2 * TILE), lambda b, p, i: (0, p, 0, 0)),
            jnp.max(na_rpb[l]) * LOG2E,
            (-1, 0, 1), True, NA_Q_TILES, (N_HEADS_DIFF + N_HEADS_DIL) // 2, N_HEADS_NA // 2,
            "neighbourhood_attention")

        x = _out_projection(x, mod[l], o_a, o_b, o_c, w_out[l].astype(BF16), 2 * tm)

        wg = w_up[l][:, :d_ff].reshape(d, n_ch, FF_CHUNK).transpose(1, 0, 2).astype(BF16)
        wu = w_up[l][:, d_ff:].reshape(d, n_ch, FF_CHUNK).transpose(1, 0, 2).astype(BF16)
        cw = conv_w[l].reshape(CONV_W, n_ch, FF_CHUNK).transpose(1, 0, 2)
        cb = conv_b[l].reshape(n_ch, 1, FF_CHUNK)
        a = _ffn_up(x, mod[l], g_ffn[l], wg, wu, cw, cb, tm)
        x = _ffn_down(x, mod[l], a, w_down[l].astype(BF16), g_final, l == depth - 1, tm)
    return x
```

```python
import functools
import math

import numpy as np
import jax
import jax.numpy as jnp
from jax import lax
from jax.experimental import pallas as pl
from jax.experimental.pallas import tpu as pltpu

F32 = jnp.float32
BF16 = jnp.bfloat16

HEAD_DIM = 64
N_HEADS_DIFF = 4
N_HEADS_DIL = 6
N_HEADS_NA = 6
W_DIFF = N_HEADS_DIFF * HEAD_DIM
W_DIL = N_HEADS_DIL * HEAD_DIM
W_NA = N_HEADS_NA * HEAD_DIM
DIFF_QK_DIM = HEAD_DIM // 2
DIL_PATTERNS = ((128, 1), (512, 4), (2048, 16))
GRID_W = 64
NA_KH = 8
NA_KW = 16
CONV_W = 3
ROPE_THETA = 10000.0
EPS = 1e-6
NEG = -1e30

LANES = 128
PAIR_W = 2 * HEAD_DIM
TILE = 256
FF_CHUNK = 256
HALO = 16
VMEM_LIMIT = 56 * 1024 * 1024
LOG2E = math.log2(math.e)
SUM_ROWS = 16
LAG_MARGIN = 64.0
DIL_Q_TILES = 8
NA_Q_TILES = 8
LAG_UNROLL = 63
QK_AHEAD = 4


def _cparams(sem):
    return pltpu.CompilerParams(dimension_semantics=sem, vmem_limit_bytes=VMEM_LIMIT)


def _mod_kernel(ct_ref, w_ref, b_ref, o_ref):
    ct = ct_ref[...]
    s = ct * jax.nn.sigmoid(ct)
    w = w_ref[0]
    rows = [jnp.sum(w * s[:, b:b + 1], axis=0, keepdims=True) for b in range(ct.shape[1])]
    o_ref[0] = jnp.concatenate(rows, axis=0) + b_ref[0]


def _modulation(c, w_ada, b_ada):
    depth, d, n = w_ada.shape
    bsz = c.shape[0]
    tn = 768
    return pl.pallas_call(
        _mod_kernel,
        grid=(depth, n // tn),
        in_specs=[
            pl.BlockSpec((d, bsz), lambda l, j: (0, 0)),
            pl.BlockSpec((1, d, tn), lambda l, j: (l, 0, j)),
            pl.BlockSpec((1, 1, tn), lambda l, j: (l, 0, j)),
        ],
        out_specs=pl.BlockSpec((1, bsz, tn), lambda l, j: (l, 0, j)),
        out_shape=jax.ShapeDtypeStruct((depth, bsz, n), F32),
        compiler_params=_cparams(("arbitrary", "arbitrary")),
        name="adaln_mod",
    )(c.T, w_ada, b_ada.reshape(depth, 1, n))


def _norm_mod(x, g, sc, sh):
    ms = jnp.mean(x * x, axis=-1, keepdims=True)
    return (x * lax.rsqrt(ms + EPS)) * g * (1.0 + sc) + sh


def _rope_slab(acc, cos, sin, group, width):
    half = group // 2
    lane = lax.broadcasted_iota(jnp.int32, (1, LANES), 1) % group
    first = lane < half
    outs = []
    for c in range(width // LANES):
        xc = acc[:, c * LANES:(c + 1) * LANES]
        swapped = jnp.where(first, pltpu.roll(xc, LANES - half, axis=1), pltpu.roll(xc, half, axis=1))
        outs.append(xc * cos + swapped * sin)
    if width < acc.shape[1]:
        outs.append(acc[:, width:])
    return jnp.concatenate(outs, axis=1)


def _proj_kernel(x_ref, mod_ref, g_ref, w_ref, ca_ref, sa_ref, cb_ref, sb_ref,
                 qt_ref, k_ref, vt_ref, kmax_ref):
    tm = x_ref.shape[1]
    h = _norm_mod(x_ref[0], g_ref[...], mod_ref[0, 1:2, :], mod_ref[0, 0:1, :]).astype(BF16)
    ca, sa, cb, sb = ca_ref[...], sa_ref[...], cb_ref[...], sb_ref[...]
    width = W_DIFF + W_DIL + W_NA
    slabs = ((0, W_DIFF), (W_DIFF, W_DIL + W_NA))

    def slab(part, si):
        off, w = slabs[si]
        acc = jnp.dot(h, w_ref[:, part * width + off: part * width + off + w], preferred_element_type=F32)
        if part < 2 and si == 0:
            acc = _rope_slab(acc, ca, sa, DIFF_QK_DIM, W_DIFF)
        elif part < 2 and si == 1:
            acc = _rope_slab(acc, cb, sb, HEAD_DIM, W_DIL)
        return acc

    q_scale = (LOG2E * DIFF_QK_DIM ** -0.5, LOG2E * HEAD_DIM ** -0.5)
    for si, (off, w) in enumerate(slabs):
        q = slab(0, si) * q_scale[si]
        qt_ref[0, off:off + w, :] = q.T.astype(BF16)
        kb = slab(1, si).astype(BF16)
        k_ref[0, :, off:off + w] = kb
        kmax_ref[0, 0, :, off:off + w] = jnp.max(jnp.abs(kb.astype(F32)), axis=0, keepdims=True)
        v = slab(2, si)
        for t in range(tm // TILE):
            vt_ref[0, t, off:off + w, :] = v[t * TILE:(t + 1) * TILE, :].T.astype(BF16)


def _rope_tables(s, group):
    half = group // 2
    inv = ROPE_THETA ** (-jnp.arange(half, dtype=F32) / half)
    ang = jnp.arange(s, dtype=F32)[:, None] * inv
    cos, sin = jnp.cos(ang), jnp.sin(ang)
    reps = LANES // group
    return (jnp.tile(jnp.concatenate([cos, cos], axis=1), (1, reps)),
            jnp.tile(jnp.concatenate([-sin, sin], axis=1), (1, reps)))


def _projection(x, mod_l, g, w_perm, tables, tm):
    bsz, s, d = x.shape
    width = W_DIFF + W_DIL + W_NA
    tab_spec = pl.BlockSpec((tm, LANES), lambda b, i: (i, 0))
    return pl.pallas_call(
        _proj_kernel,
        grid=(bsz, s // tm),
        in_specs=[
            pl.BlockSpec((1, tm, d), lambda b, i: (b, i, 0)),
            pl.BlockSpec((1, 6, d), lambda b, i: (b, 0, 0)),
            pl.BlockSpec((1, d), lambda b, i: (0, 0)),
            pl.BlockSpec((d, 3 * width), lambda b, i: (0, 0)),
            tab_spec, tab_spec, tab_spec, tab_spec,
        ],
        out_specs=[
            pl.BlockSpec((1, width, tm), lambda b, i: (b, 0, i)),
            pl.BlockSpec((1, tm, width), lambda b, i: (b, i, 0)),
            pl.BlockSpec((1, tm // TILE, width, TILE), lambda b, i: (b, i, 0, 0)),
            pl.BlockSpec((1, 1, 1, width), lambda b, i: (b, i, 0, 0)),
        ],
        out_shape=[
            jax.ShapeDtypeStruct((bsz, width, s), BF16),
            jax.ShapeDtypeStruct((bsz, s, width), BF16),
            jax.ShapeDtypeStruct((bsz, s // TILE, width, TILE), BF16),
            jax.ShapeDtypeStruct((bsz, s // tm, 1, width), F32),
        ],
        compiler_params=_cparams(("arbitrary", "arbitrary")),
        name="norm_proj_rope",
    )(x, mod_l, g.reshape(1, d), w_perm, *tables)


def _masked_queries(qt, n_split):
    row = lax.broadcasted_iota(jnp.int32, (PAIR_W, 1), 0)
    step = PAIR_W // n_split
    return [jnp.where((row >= j * step) & (row < (j + 1) * step), qt, jnp.zeros_like(qt))
            for j in range(n_split)]


def _v_with_ones(vtb, hd):
    ones = jnp.ones((SUM_ROWS, vtb.shape[1]), BF16)
    return jnp.concatenate([vtb[hd * HEAD_DIM:(hd + 1) * HEAD_DIM, :], ones], axis=0)


def _score_bound(kmax_ref, q_s):
    kmax = jnp.max(kmax_ref[0], axis=0) * (1.0 + 2.0 ** -7)
    kmax = jnp.broadcast_to(kmax, (SUM_ROWS, PAIR_W)).astype(BF16)
    return jnp.dot(kmax, jnp.abs(q_s[...]), preferred_element_type=F32)[0:1] * (1.0 + 2.0 ** -7)


def _diff_kernel(lam_init, dl_ref, g_ref, kmax_ref, qt_ref, k_ref, vt_ref, o_ref, q_s, s_s, m_s, acc_s):
    tq = qt_ref.shape[2]
    n_kt = vt_ref.shape[1]
    for j, qm in enumerate(_masked_queries(qt_ref[0], 4)):
        q_s[:, j * tq:(j + 1) * tq] = qm
    lane_tiles = [slice(nt * TILE, (nt + 1) * TILE) for nt in range(4 * tq // TILE)]

    def k_tile(kt):
        return k_ref[0, pl.ds(pl.multiple_of(kt * TILE, TILE), TILE), :]

    def v_tiles(kt):
        vtb = vt_ref[0, kt]
        return [_v_with_ones(vtb, hd) for hd in range(2)]

    def exact_step(kt, first):
        kb, v_ext = k_tile(kt), v_tiles(kt)
        for ln in lane_tiles:
            s_s[:, ln] = jnp.dot(kb, q_s[:, ln], preferred_element_type=F32)
        for ln in lane_tiles:
            tile_max = jnp.max(s_s[:, ln], axis=0, keepdims=True)
            m_new = tile_max if first else jnp.maximum(m_s[:, ln], tile_max)
            e = jnp.exp2(s_s[:, ln] - m_new).astype(BF16)
            pv = jnp.dot(v_ext[ln.start // (2 * tq)], e, preferred_element_type=F32)
            acc_s[:, ln] = pv if first else acc_s[:, ln] * jnp.exp2(m_s[:, ln] - m_new) + pv
            m_s[:, ln] = m_new

    def fixed_shift_steps(kts):
        def finish(vtb, ln, s):
            hd = ln.start // (2 * tq)
            e = jnp.exp2(s - m_s[:, ln])
            acc_s[0:HEAD_DIM, ln] += jnp.dot(vtb[hd * HEAD_DIM:(hd + 1) * HEAD_DIM, :], e.astype(BF16),
                                             preferred_element_type=F32)
            acc_s[HEAD_DIM:HEAD_DIM + 1, ln] += jnp.sum(e, axis=0, keepdims=True)

        pending = []
        for kt in kts:
            kb, vtb = k_tile(kt), vt_ref[0, kt]
            for ln in lane_tiles:
                pending.append((vtb, ln, jnp.dot(kb, q_s[:, ln], preferred_element_type=F32)))
                if len(pending) > QK_AHEAD:
                    finish(*pending.pop(0))
        for item in pending:
            finish(*item)

    exact_step(0, True)
    lag_ok = jnp.max(_score_bound(kmax_ref, q_s) - m_s[...]) < LAG_MARGIN

    @pl.when(lag_ok)
    def _():
        def body(tt, carry):
            fixed_shift_steps([LAG_UNROLL * tt + 1 + u for u in range(LAG_UNROLL)])
            return carry
        n_trips = (n_kt - 1) // LAG_UNROLL
        lax.fori_loop(0, n_trips, body, 0)
        if n_trips * LAG_UNROLL + 1 < n_kt:
            fixed_shift_steps(list(range(n_trips * LAG_UNROLL + 1, n_kt)))

    @pl.when(jnp.logical_not(lag_ok))
    def _():
        def body(kt, carry):
            exact_step(kt, False)
            return carry
        lax.fori_loop(1, n_kt, body, 0)

    dl = dl_ref[...]
    lam = (jnp.exp(jnp.sum(dl[0:1] * dl[1:2], axis=1, keepdims=True))
           - jnp.exp(jnp.sum(dl[2:3] * dl[3:4], axis=1, keepdims=True)) + lam_init)
    g = g_ref[...]
    outs = []
    for hd in range(2):
        p = [acc_s[0:HEAD_DIM, (2 * hd + mp) * tq:(2 * hd + mp + 1) * tq]
             / acc_s[HEAD_DIM:HEAD_DIM + 1, (2 * hd + mp) * tq:(2 * hd + mp + 1) * tq] for mp in range(2)]
        o = p[0] - lam * p[1]
        ms = jnp.mean(o * o, axis=0, keepdims=True)
        outs.append((o * lax.rsqrt(ms + EPS)) * g * (1.0 - lam_init))
    o_ref[0] = jnp.concatenate(outs, axis=0).T.astype(BF16)


def _diff_attention(qt, k, vt, kmax, diff_lambda, subln, layer_idx, tq):
    bsz, _, s = qt.shape
    n_pairs = N_HEADS_DIFF // 2
    lam_init = 0.8 - 0.6 * math.exp(-0.3 * layer_idx)
    return pl.pallas_call(
        functools.partial(_diff_kernel, lam_init),
        grid=(bsz, n_pairs, s // tq),
        in_specs=[
            pl.BlockSpec(diff_lambda.shape, lambda b, p, i: (0, 0)),
            pl.BlockSpec((HEAD_DIM, 1), lambda b, p, i: (0, 0)),
            pl.BlockSpec((1, kmax.shape[1], 1, PAIR_W), lambda b, p, i: (b, 0, 0, p)),
            pl.BlockSpec((1, PAIR_W, tq), lambda b, p, i: (b, p, i)),
            pl.BlockSpec((1, s, PAIR_W), lambda b, p, i: (b, 0, p)),
            pl.BlockSpec((1, s // TILE, PAIR_W, TILE), lambda b, p, i: (b, 0, p, 0)),
        ],
        out_specs=pl.BlockSpec((1, tq, PAIR_W), lambda b, p, i: (b, i, p)),
        out_shape=jax.ShapeDtypeStruct((bsz, s, W_DIFF), BF16),
        scratch_shapes=[
            pltpu.VMEM((PAIR_W, 4 * tq), BF16),
            pltpu.VMEM((TILE, 4 * tq), F32),
            pltpu.VMEM((1, 4 * tq), F32),
            pltpu.VMEM((HEAD_DIM + SUM_ROWS, 4 * tq), F32),
        ],
        compiler_params=_cparams(("arbitrary", "arbitrary", "arbitrary")),
        name="diff_attention",
    )(diff_lambda, subln.reshape(HEAD_DIM, 1), kmax, qt, k, vt)


def _band_kernel(offsets, window_bias, bmax_ref, kmax_ref, qt_ref, k_ref, vt_ref, bias_ref, o_ref,
                 q_s, sc_s, s_s, acc_s):
    i = pl.program_id(2)
    n_blk = vt_ref.shape[1]
    n_off = len(offsets)
    centre = offsets.index(0)
    n_qt = qt_ref.shape[2] // TILE
    qt = qt_ref[0]
    for t in range(n_qt):
        for hd, qm in enumerate(_masked_queries(qt[:, t * TILE:(t + 1) * TILE], 2)):
            q_s[:, (2 * t + hd) * TILE:(2 * t + hd + 1) * TILE] = qm
    lane_tiles = [slice(c * TILE, (c + 1) * TILE) for c in range(2 * n_qt)]

    def q_tile(ln):
        return i * n_qt + ln.start // (2 * TILE)

    def key_tile(oi, ln):
        return jnp.clip(q_tile(ln) + offsets[oi], 0, n_blk - 1)

    def scores(oi, ln):
        qi = q_tile(ln)
        head_lanes = slice(ln.start % (2 * TILE), ln.start % (2 * TILE) + TILE)
        if window_bias:
            variant = jnp.where(qi == 0, 0, jnp.where(qi == n_blk - 1, 2, 1))
            bias = bias_ref[variant, 0, oi * TILE:(oi + 1) * TILE, head_lanes]
        else:
            kbi = qi + offsets[oi]
            bias = bias_ref[jnp.where((kbi >= 0) & (kbi < n_blk), oi, n_off), :, head_lanes]
        kb = k_ref[0, pl.ds(pl.multiple_of(key_tile(oi, ln) * TILE, TILE), TILE), :]
        return jnp.dot(kb, q_s[:, ln], preferred_element_type=F32) + bias

    def weighted_values(oi, ln, e):
        vtb = vt_ref[0, key_tile(oi, ln)]
        return jnp.dot(_v_with_ones(vtb, (ln.start // TILE) % 2), e, preferred_element_type=F32)

    for ln in lane_tiles:
        sc_s[:, ln] = scores(centre, ln)
    m_c = jnp.max(sc_s[...], axis=0, keepdims=True)
    shift_ok = jnp.max(_score_bound(kmax_ref, q_s) + bmax_ref[...] - m_c) < LAG_MARGIN

    @pl.when(shift_ok)
    def _():
        acc = {ln.start: None for ln in lane_tiles}

        def finish(oi, ln, s):
            pv = weighted_values(oi, ln, jnp.exp2(s - m_c[:, ln]).astype(BF16))
            acc[ln.start] = pv if acc[ln.start] is None else acc[ln.start] + pv

        pending = []
        for oi in range(n_off):
            for ln in lane_tiles:
                pending.append((oi, ln, sc_s[:, ln] if oi == centre else scores(oi, ln)))
                if len(pending) > QK_AHEAD:
                    finish(*pending.pop(0))
        for item in pending:
            finish(*item)
        for ln in lane_tiles:
            acc_s[:, ln] = acc[ln.start]

    @pl.when(jnp.logical_not(shift_ok))
    def _():
        for ln in lane_tiles:
            col_max = m_c[:, ln]
            for oi in range(n_off):
                s_s[oi] = sc_s[:, ln] if oi == centre else scores(oi, ln)
                col_max = jnp.maximum(col_max, jnp.max(s_s[oi], axis=0, keepdims=True))
            acc = None
            for oi in range(n_off):
                pv = weighted_values(oi, ln, jnp.exp2(s_s[oi] - col_max).astype(BF16))
                acc = pv if acc is None else acc + pv
            acc_s[:, ln] = acc

    out = acc_s[0:HEAD_DIM, :] / acc_s[HEAD_DIM:HEAD_DIM + 1, :]
    for t in range(n_qt):
        pair = jnp.concatenate([out[:, lane_tiles[2 * t]], out[:, lane_tiles[2 * t + 1]]], axis=0)
        o_ref[0, t * TILE:(t + 1) * TILE, :] = pair.T.astype(BF16)


def _band_attention(qt, k, vt, kmax, bias, bias_spec, bias_max, offsets, window_bias, q_tiles, pair0, n_pairs,
                    name):
    bsz, _, s = qt.shape
    tq = q_tiles * TILE
    return pl.pallas_call(
        functools.partial(_band_kernel, offsets, window_bias),
        grid=(bsz, n_pairs, s // tq),
        in_specs=[
            pl.BlockSpec((1, 1), lambda b, p, i: (0, 0)),
            pl.BlockSpec((1, kmax.shape[1], 1, PAIR_W), lambda b, p, i: (b, 0, 0, pair0 + p)),
            pl.BlockSpec((1, PAIR_W, tq), lambda b, p, i: (b, pair0 + p, i)),
            pl.BlockSpec((1, s, PAIR_W), lambda b, p, i: (b, 0, pair0 + p)),
            pl.BlockSpec((1, s // TILE, PAIR_W, TILE), lambda b, p, i: (b, 0, pair0 + p, 0)),
            bias_spec,
        ],
        out_specs=pl.BlockSpec((1, tq, PAIR_W), lambda b, p, i: (b, i, p)),
        out_shape=jax.ShapeDtypeStruct((bsz, s, n_pairs * PAIR_W), BF16),
        scratch_shapes=[
            pltpu.VMEM((PAIR_W, 2 * tq), BF16),
            pltpu.VMEM((TILE, 2 * tq), F32),
            pltpu.VMEM((len(offsets), TILE, TILE), F32),
            pltpu.VMEM((HEAD_DIM + SUM_ROWS, 2 * tq), F32),
        ],
        compiler_params=_cparams(("arbitrary", "arbitrary", "arbitrary")),
        name=name,
    )(jnp.reshape(bias_max, (1, 1)).astype(F32), kmax, qt, k, vt, bias)


def _dilated_bias():
    reach = max(w // 2 for w, _ in DIL_PATTERNS)
    n_off = -(-reach // TILE)
    offsets = tuple(range(-n_off, n_off + 1))
    kj = np.arange(TILE)[:, None]
    qi = np.arange(TILE)[None, :]
    tabs = []
    for off in offsets:
        delta = off * TILE + kj - qi
        cnt = np.zeros((TILE, TILE), np.int32)
        for window, dil in DIL_PATTERNS:
            cnt += ((delta % dil == 0) & (np.abs(delta) <= window // 2)).astype(np.int32)
        tabs.append(np.where(cnt > 0, np.log2(np.maximum(cnt, 1).astype(np.float64)), NEG))
    tabs.append(np.full((TILE, TILE), NEG))
    return offsets, jnp.asarray(np.tile(np.stack(tabs), (1, 1, 2)), F32)


def _na_bias_kernel(rows, rpb_ref, o_ref):
    hd = pl.program_id(0)
    kh = min(NA_KH, rows)
    q_rows = TILE // GRID_W
    kc = lax.broadcasted_iota(jnp.int32, (GRID_W, LANES), 0)
    lane = lax.broadcasted_iota(jnp.int32, (GRID_W, LANES), 1)
    qc = lane % GRID_W
    dc = jnp.clip(kc - qc, -(NA_KW - 1), NA_KW - 1) + (NA_KW - 1)
    cs = jnp.clip(qc - NA_KW // 2, 0, GRID_W - NA_KW)
    col_ok = (kc >= cs) & (kc < cs + NA_KW)
    neg = jnp.full((GRID_W, LANES), NEG, F32)
    tiles = []
    for dr in range(2 * NA_KH - 1):
        t = neg
        for j in range(2 * NA_KW - 1):
            t = jnp.where(dc == j, rpb_ref[hd, dr, j] * LOG2E, t)
        tiles.append(jnp.where(col_ok, t, neg))
    for v, r0 in enumerate((0, 2 * q_rows, rows - q_rows)):
        for kr_rel in range(3 * q_rows):
            kr = r0 - q_rows + kr_rel
            for pair in range(q_rows // 2):
                halves = []
                for r in (r0 + 2 * pair, r0 + 2 * pair + 1):
                    rs = min(max(r - kh // 2, 0), rows - kh)
                    halves.append(tiles[kr - r + NA_KH - 1] if rs <= kr < rs + kh else neg)
                blk = jnp.where(lane < GRID_W, halves[0], halves[1])
                o_ref[v, 0, kr_rel * GRID_W:(kr_rel + 1) * GRID_W, pair * LANES:(pair + 1) * LANES] = blk


def _na_bias(rpb, rows):
    n_heads = rpb.shape[0]
    return pl.pallas_call(
        functools.partial(_na_bias_kernel, rows),
        grid=(n_heads,),
        in_specs=[pl.BlockSpec(memory_space=pltpu.SMEM)],
        out_specs=pl.BlockSpec((3, 1, 3 * TILE, TILE), lambda h: (0, h // 2, 0, h % 2)),
        out_shape=jax.ShapeDtypeStruct((3, n_heads // 2, 3 * TILE, 2 * TILE), F32),
        compiler_params=_cparams(("arbitrary",)),
        name="na_bias_table",
    )(rpb)


def _outproj_kernel(x_ref, mod_ref, oa_ref, ob_ref, oc_ref, w_ref, y_ref):
    o = jnp.concatenate([oa_ref[0], ob_ref[0], oc_ref[0]], axis=1)
    mix = jnp.dot(o, w_ref[...], preferred_element_type=F32)
    y_ref[0] = x_ref[0] + mod_ref[0, 2:3, :] * mix


def _out_projection(x, mod_l, o_a, o_b, o_c, w_out, tm):
    bsz, s, d = x.shape
    row = lambda w: pl.BlockSpec((1, tm, w), lambda b, i: (b, i, 0))
    return pl.pallas_call(
        _outproj_kernel,
        grid=(bsz, s // tm),
        in_specs=[
            row(d),
            pl.BlockSpec((1, 6, d), lambda b, i: (b, 0, 0)),
            row(W_DIFF), row(W_DIL), row(W_NA),
            pl.BlockSpec(w_out.shape, lambda b, i: (0, 0)),
        ],
        out_specs=row(d),
        out_shape=jax.ShapeDtypeStruct((bsz, s, d), F32),
        compiler_params=_cparams(("arbitrary", "arbitrary")),
        name="out_proj_residual",
    )(x, mod_l, o_a, o_b, o_c, w_out)


def _ffn_up_kernel(x_ref, xp_ref, xn_ref, mod_ref, g_ref, wg_ref, wu_ref, cw_ref, cb_ref,
                   a_ref, h_s, g_s, u_s):
    i = pl.program_id(1)
    tm = x_ref.shape[1]
    g, sc, sh = g_ref[...], mod_ref[0, 4:5, :], mod_ref[0, 3:4, :]
    keep_prev = (i > 0).astype(F32)
    keep_next = (i < pl.num_programs(1) - 1).astype(F32)
    h_s[0:HALO, :] = (_norm_mod(xp_ref[0], g, sc, sh) * keep_prev).astype(BF16)
    h_s[HALO:HALO + tm, :] = _norm_mod(x_ref[0], g, sc, sh).astype(BF16)
    h_s[HALO + tm:, :] = (_norm_mod(xn_ref[0], g, sc, sh) * keep_next).astype(BF16)
    n_ch = wg_ref.shape[0]

    def matmuls(c):
        g_s[c % 2] = jnp.dot(h_s[...], wg_ref[c], preferred_element_type=F32)
        u_s[c % 2] = jnp.dot(h_s[HALO:HALO + tm, :], wu_ref[c], preferred_element_type=F32)

    def gate(c):
        cw = cw_ref[c]
        gc = cb_ref[c]
        for t in range(CONV_W):
            gc = gc + g_s[c % 2, pl.ds(HALO - CONV_W // 2 + t, tm), :] * cw[t:t + 1, :]
        a_ref[0, :, c * FF_CHUNK:(c + 1) * FF_CHUNK] = (gc * jax.nn.sigmoid(gc) * u_s[c % 2]).astype(BF16)

    matmuls(0)
    for c in range(n_ch):
        if c + 1 < n_ch:
            matmuls(c + 1)
        gate(c)


def _ffn_up(x, mod_l, g, wg, wu, cw, cb, tm):
    bsz, s, d = x.shape
    n_ch = wg.shape[0]
    hb = tm // HALO
    n_hb = s // HALO
    return pl.pallas_call(
        _ffn_up_kernel,
        grid=(bsz, s // tm),
        in_specs=[
            pl.BlockSpec((1, tm, d), lambda b, i: (b, i, 0)),
            pl.BlockSpec((1, HALO, d), lambda b, i: (b, jnp.maximum(i * hb - 1, 0), 0)),
            pl.BlockSpec((1, HALO, d), lambda b, i: (b, jnp.minimum((i + 1) * hb, n_hb - 1), 0)),
            pl.BlockSpec((1, 6, d), lambda b, i: (b, 0, 0)),
            pl.BlockSpec((1, d), lambda b, i: (0, 0)),
            pl.BlockSpec(wg.shape, lambda b, i: (0, 0, 0)),
            pl.BlockSpec(wu.shape, lambda b, i: (0, 0, 0)),
            pl.BlockSpec(cw.shape, lambda b, i: (0, 0, 0)),
            pl.BlockSpec(cb.shape, lambda b, i: (0, 0, 0)),
        ],
        out_specs=pl.BlockSpec((1, tm, n_ch * FF_CHUNK), lambda b, i: (b, i, 0)),
        out_shape=jax.ShapeDtypeStruct((bsz, s, n_ch * FF_CHUNK), BF16),
        scratch_shapes=[
            pltpu.VMEM((tm + 2 * HALO, d), BF16),
            pltpu.VMEM((2, tm + 2 * HALO, FF_CHUNK), F32),
            pltpu.VMEM((2, tm, FF_CHUNK), F32),
        ],
        compiler_params=_cparams(("arbitrary", "arbitrary")),
        name="ffn_up_conv_glu",
    )(x, x, x, mod_l, g.reshape(1, d), wg, wu, cw, cb)


def _ffn_down_kernel(final, x_ref, mod_ref, a_ref, w_ref, gf_ref, y_ref):
    y = x_ref[0] + mod_ref[0, 5:6, :] * jnp.dot(a_ref[0], w_ref[...], preferred_element_type=F32)
    if final:
        ms = jnp.mean(y * y, axis=-1, keepdims=True)
        y = (y * lax.rsqrt(ms + EPS)) * gf_ref[...]
    y_ref[0] = y


def _ffn_down(x, mod_l, a, w_down, g_final, final, tm):
    bsz, s, d = x.shape
    return pl.pallas_call(
        functools.partial(_ffn_down_kernel, final),
        grid=(bsz, s // tm),
        in_specs=[
            pl.BlockSpec((1, tm, d), lambda b, i: (b, i, 0)),
            pl.BlockSpec((1, 6, d), lambda b, i: (b, 0, 0)),
            pl.BlockSpec((1, tm, a.shape[2]), lambda b, i: (b, i, 0)),
            pl.BlockSpec(w_down.shape, lambda b, i: (0, 0)),
            pl.BlockSpec((1, d), lambda b, i: (0, 0)),
        ],
        out_specs=pl.BlockSpec((1, tm, d), lambda b, i: (b, i, 0)),
        out_shape=jax.ShapeDtypeStruct((bsz, s, d), F32),
        compiler_params=_cparams(("arbitrary", "arbitrary")),
        name="ffn_down_residual",
    )(x, mod_l, a, w_down, g_final.reshape(1, d))


def kernel(x, c, w_ada, b_ada, g_attn, w_in, diff_lambda, diff_subln, na_rpb, w_out, g_ffn, w_up,
           conv_w, conv_b, w_down, g_final):
    bsz, s, d = x.shape
    depth = w_ada.shape[0]
    d_ff = w_down.shape[1]
    n_ch = d_ff // FF_CHUNK
    tm = 512
    assert s % tm == 0 and tm % TILE == 0 and d_ff % FF_CHUNK == 0 and TILE % (2 * GRID_W) == 0
    assert (s // TILE) % max(DIL_Q_TILES, NA_Q_TILES) == 0 and s // GRID_W >= 4 * (TILE // GRID_W)

    mod = _modulation(c, w_ada, b_ada).reshape(depth, bsz, 6, d)
    tables = _rope_tables(s, DIFF_QK_DIM) + _rope_tables(s, HEAD_DIM)
    dil_offsets, dil_bias = _dilated_bias()
    rows = s // GRID_W

    wa, wb, wc = W_DIFF, W_DIL, W_NA
    offs = np.cumsum([0, wa, wa, wa, wb, wb, wb, wc, wc, wc])
    order = (0, 3, 6, 1, 4, 7, 2, 5, 8)

    for l in range(depth):
        w_perm = jnp.concatenate([w_in[l][:, offs[j]:offs[j + 1]] for j in order], axis=1).astype(BF16)
        qt, k, vt, kmax = _projection(x, mod[l], g_attn[l], w_perm, tables, 2 * tm)

        o_a = _diff_attention(qt, k, vt, kmax, diff_lambda[l], diff_subln[l], l, 2 * TILE)
        o_b = _band_attention(
            qt, k, vt, kmax, dil_bias,
            pl.BlockSpec(dil_bias.shape, lambda b, p, i: (0, 0, 0)),
            jnp.float32(math.log2(len(DIL_PATTERNS))),
            dil_offsets, False, DIL_Q_TILES, N_HEADS_DIFF // 2, N_HEADS_DIL // 2, "dilated_attention")
        o_c = _band_attention(
            qt, k, vt, kmax, _na_bias(na_rpb[l], rows),
            pl.BlockSpec((3, 1, 3 * TILE, 2 * TILE), lambda b, p, i: (0, p, 0, 0)),
            jnp.max(na_rpb[l]) * LOG2E,
            (-1, 0, 1), True, NA_Q_TILES, (N_HEADS_DIFF + N_HEADS_DIL) // 2, N_HEADS_NA // 2,
            "neighbourhood_attention")

        x = _out_projection(x, mod[l], o_a, o_b, o_c, w_out[l].astype(BF16), 2 * tm)

        wg = w_up[l][:, :d_ff].reshape(d, n_ch, FF_CHUNK).transpose(1, 0, 2).astype(BF16)
        wu = w_up[l][:, d_ff:].reshape(d, n_ch, FF_CHUNK).transpose(1, 0, 2).astype(BF16)
        cw = conv_w[l].reshape(CONV_W, n_ch, FF_CHUNK).transpose(1, 0, 2)
        cb = conv_b[l].reshape(n_ch, 1, FF_CHUNK)
        a = _ffn_up(x, mod[l], g_ffn[l], wg, wu, cw, cb, 2 * tm)
        x = _ffn_down(x, mod[l], a, w_down[l].astype(BF16), g_final, l == depth - 1, 2 * tm)
    return x
```

```python
import functools
import math

import numpy as np
import jax
import jax.numpy as jnp
from jax import lax
from jax.experimental import pallas as pl
from jax.experimental.pallas import tpu as pltpu

F32 = jnp.float32
BF16 = jnp.bfloat16

HEAD_DIM = 64
N_HEADS_DIFF = 4
N_HEADS_DIL = 6
N_HEADS_NA = 6
W_DIFF = N_HEADS_DIFF * HEAD_DIM
W_DIL = N_HEADS_DIL * HEAD_DIM
W_NA = N_HEADS_NA * HEAD_DIM
DIFF_QK_DIM = HEAD_DIM // 2
DIL_PATTERNS = ((128, 1), (512, 4), (2048, 16))
GRID_W = 64
NA_KH = 8
NA_KW = 16
CONV_W = 3
ROPE_THETA = 10000.0
EPS = 1e-6
NEG = -1e30

LANES = 128
PAIR_W = 2 * HEAD_DIM
TILE = 256
FF_CHUNK = 256
HALO = 16
VMEM_LIMIT = 56 * 1024 * 1024
LOG2E = math.log2(math.e)
SUM_ROWS = 16
LAG_MARGIN = 64.0
DIL_Q_TILES = 8
NA_Q_TILES = 8
LAG_UNROLL = 63
QK_AHEAD = 4


def _cparams(sem):
    return pltpu.CompilerParams(dimension_semantics=sem, vmem_limit_bytes=VMEM_LIMIT)


def _mod_kernel(ct_ref, w_ref, b_ref, o_ref):
    ct = ct_ref[...]
    s = ct * jax.nn.sigmoid(ct)
    w = w_ref[0]
    rows = [jnp.sum(w * s[:, b:b + 1], axis=0, keepdims=True) for b in range(ct.shape[1])]
    o_ref[0] = jnp.concatenate(rows, axis=0) + b_ref[0]


def _modulation(c, w_ada, b_ada):
    depth, d, n = w_ada.shape
    bsz = c.shape[0]
    tn = 768
    return pl.pallas_call(
        _mod_kernel,
        grid=(depth, n // tn),
        in_specs=[
            pl.BlockSpec((d, bsz), lambda l, j: (0, 0)),
            pl.BlockSpec((1, d, tn), lambda l, j: (l, 0, j)),
            pl.BlockSpec((1, 1, tn), lambda l, j: (l, 0, j)),
        ],
        out_specs=pl.BlockSpec((1, bsz, tn), lambda l, j: (l, 0, j)),
        out_shape=jax.ShapeDtypeStruct((depth, bsz, n), F32),
        compiler_params=_cparams(("arbitrary", "arbitrary")),
        name="adaln_mod",
    )(c.T, w_ada, b_ada.reshape(depth, 1, n))


def _norm_mod(x, g, sc, sh):
    ms = jnp.mean(x * x, axis=-1, keepdims=True)
    return (x * lax.rsqrt(ms + EPS)) * g * (1.0 + sc) + sh


def _rope_slab(acc, cos, sin, group, width):
    half = group // 2
    lane = lax.broadcasted_iota(jnp.int32, (1, LANES), 1) % group
    first = lane < half
    outs = []
    for c in range(width // LANES):
        xc = acc[:, c * LANES:(c + 1) * LANES]
        swapped = jnp.where(first, pltpu.roll(xc, LANES - half, axis=1), pltpu.roll(xc, half, axis=1))
        outs.append(xc * cos + swapped * sin)
    if width < acc.shape[1]:
        outs.append(acc[:, width:])
    return jnp.concatenate(outs, axis=1)


def _proj_kernel(x_ref, mod_ref, g_ref, w_ref, ca_ref, sa_ref, cb_ref, sb_ref,
                 qt_ref, k_ref, vt_ref, kmax_ref):
    tm = x_ref.shape[1]
    h = _norm_mod(x_ref[0], g_ref[...], mod_ref[0, 1:2, :], mod_ref[0, 0:1, :]).astype(BF16)
    ca, sa, cb, sb = ca_ref[...], sa_ref[...], cb_ref[...], sb_ref[...]
    width = W_DIFF + W_DIL + W_NA
    slabs = ((0, W_DIFF), (W_DIFF, W_DIL + W_NA))

    def slab(part, si):
        off, w = slabs[si]
        acc = jnp.dot(h, w_ref[:, part * width + off: part * width + off + w], preferred_element_type=F32)
        if part < 2 and si == 0:
            acc = _rope_slab(acc, ca, sa, DIFF_QK_DIM, W_DIFF)
        elif part < 2 and si == 1:
            acc = _rope_slab(acc, cb, sb, HEAD_DIM, W_DIL)
        return acc

    q_scale = (LOG2E * DIFF_QK_DIM ** -0.5, LOG2E * HEAD_DIM ** -0.5)
    for si, (off, w) in enumerate(slabs):
        q = slab(0, si) * q_scale[si]
        qt_ref[0, off:off + w, :] = q.T.astype(BF16)
        kb = slab(1, si).astype(BF16)
        k_ref[0, :, off:off + w] = kb
        kmax_ref[0, 0, :, off:off + w] = jnp.max(jnp.abs(kb.astype(F32)), axis=0, keepdims=True)
        v = slab(2, si)
        for t in range(tm // TILE):
            vt_ref[0, t, off:off + w, :] = v[t * TILE:(t + 1) * TILE, :].T.astype(BF16)


def _rope_tables(s, group):
    half = group // 2
    inv = ROPE_THETA ** (-jnp.arange(half, dtype=F32) / half)
    ang = inv[:, None] * jnp.arange(s, dtype=F32)[None, :]
    cos, sin = jnp.cos(ang), jnp.sin(ang)
    reps = LANES // group
    return (jnp.tile(jnp.concatenate([cos, cos], axis=0), (reps, 1)).T,
            jnp.tile(jnp.concatenate([-sin, sin], axis=0), (reps, 1)).T)


def _projection(x, mod_l, g, w_perm, tables, tm):
    bsz, s, d = x.shape
    width = W_DIFF + W_DIL + W_NA
    tab_spec = pl.BlockSpec((tm, LANES), lambda b, i: (i, 0))
    return pl.pallas_call(
        _proj_kernel,
        grid=(bsz, s // tm),
        in_specs=[
            pl.BlockSpec((1, tm, d), lambda b, i: (b, i, 0)),
            pl.BlockSpec((1, 6, d), lambda b, i: (b, 0, 0)),
            pl.BlockSpec((1, d), lambda b, i: (0, 0)),
            pl.BlockSpec((d, 3 * width), lambda b, i: (0, 0)),
            tab_spec, tab_spec, tab_spec, tab_spec,
        ],
        out_specs=[
            pl.BlockSpec((1, width, tm), lambda b, i: (b, 0, i)),
            pl.BlockSpec((1, tm, width), lambda b, i: (b, i, 0)),
            pl.BlockSpec((1, tm // TILE, width, TILE), lambda b, i: (b, i, 0, 0)),
            pl.BlockSpec((1, 1, 1, width), lambda b, i: (b, i, 0, 0)),
        ],
        out_shape=[
            jax.ShapeDtypeStruct((bsz, width, s), BF16),
            jax.ShapeDtypeStruct((bsz, s, width), BF16),
            jax.ShapeDtypeStruct((bsz, s // TILE, width, TILE), BF16),
            jax.ShapeDtypeStruct((bsz, s // tm, 1, width), F32),
        ],
        compiler_params=_cparams(("arbitrary", "arbitrary")),
        name="norm_proj_rope",
    )(x, mod_l, g.reshape(1, d), w_perm, *tables)


def _masked_queries(qt, n_split):
    row = lax.broadcasted_iota(jnp.int32, (PAIR_W, 1), 0)
    step = PAIR_W // n_split
    return [jnp.where((row >= j * step) & (row < (j + 1) * step), qt, jnp.zeros_like(qt))
            for j in range(n_split)]


def _v_with_ones(vtb, hd):
    ones = jnp.ones((SUM_ROWS, vtb.shape[1]), BF16)
    return jnp.concatenate([vtb[hd * HEAD_DIM:(hd + 1) * HEAD_DIM, :], ones], axis=0)


def _score_bound(kmax_ref, q_s):
    kmax = jnp.max(kmax_ref[0], axis=0) * (1.0 + 2.0 ** -7)
    kmax = jnp.broadcast_to(kmax, (SUM_ROWS, PAIR_W)).astype(BF16)
    return jnp.dot(kmax, jnp.abs(q_s[...]), preferred_element_type=F32)[0:1] * (1.0 + 2.0 ** -7)


def _diff_kernel(lam_init, dl_ref, g_ref, kmax_ref, qt_ref, k_ref, vt_ref, o_ref, q_s, s_s, m_s, acc_s):
    tq = qt_ref.shape[2]
    n_kt = vt_ref.shape[1]
    for j, qm in enumerate(_masked_queries(qt_ref[0], 4)):
        q_s[:, j * tq:(j + 1) * tq] = qm
    lane_tiles = [slice(nt * TILE, (nt + 1) * TILE) for nt in range(4 * tq // TILE)]

    def k_tile(kt):
        return k_ref[0, pl.ds(pl.multiple_of(kt * TILE, TILE), TILE), :]

    def v_tiles(kt):
        vtb = vt_ref[0, kt]
        return [_v_with_ones(vtb, hd) for hd in range(2)]

    def exact_step(kt, first):
        kb, v_ext = k_tile(kt), v_tiles(kt)
        for ln in lane_tiles:
            s_s[:, ln] = jnp.dot(kb, q_s[:, ln], preferred_element_type=F32)
        for ln in lane_tiles:
            tile_max = jnp.max(s_s[:, ln], axis=0, keepdims=True)
            m_new = tile_max if first else jnp.maximum(m_s[:, ln], tile_max)
            e = jnp.exp2(s_s[:, ln] - m_new).astype(BF16)
            pv = jnp.dot(v_ext[ln.start // (2 * tq)], e, preferred_element_type=F32)
            acc_s[:, ln] = pv if first else acc_s[:, ln] * jnp.exp2(m_s[:, ln] - m_new) + pv
            m_s[:, ln] = m_new

    def fixed_shift_steps(kts):
        def finish(vtb, ln, s):
            hd = ln.start // (2 * tq)
            e = jnp.exp2(s - m_s[:, ln])
            acc_s[0:HEAD_DIM, ln] += jnp.dot(vtb[hd * HEAD_DIM:(hd + 1) * HEAD_DIM, :], e.astype(BF16),
                                             preferred_element_type=F32)
            acc_s[HEAD_DIM:HEAD_DIM + 1, ln] += jnp.sum(e, axis=0, keepdims=True)

        pending = []
        for kt in kts:
            kb, vtb = k_tile(kt), vt_ref[0, kt]
            for ln in lane_tiles:
                pending.append((vtb, ln, jnp.dot(kb, q_s[:, ln], preferred_element_type=F32)))
                if len(pending) > QK_AHEAD:
                    finish(*pending.pop(0))
        for item in pending:
            finish(*item)

    exact_step(0, True)
    lag_ok = jnp.max(_score_bound(kmax_ref, q_s) - m_s[...]) < LAG_MARGIN

    @pl.when(lag_ok)
    def _():
        def body(tt, carry):
            fixed_shift_steps([LAG_UNROLL * tt + 1 + u for u in range(LAG_UNROLL)])
            return carry
        n_trips = (n_kt - 1) // LAG_UNROLL
        lax.fori_loop(0, n_trips, body, 0)
        if n_trips * LAG_UNROLL + 1 < n_kt:
            fixed_shift_steps(list(range(n_trips * LAG_UNROLL + 1, n_kt)))

    @pl.when(jnp.logical_not(lag_ok))
    def _():
        def body(kt, carry):
            exact_step(kt, False)
            return carry
        lax.fori_loop(1, n_kt, body, 0)

    dl = dl_ref[...]
    lam = (jnp.exp(jnp.sum(dl[0:1] * dl[1:2], axis=1, keepdims=True))
           - jnp.exp(jnp.sum(dl[2:3] * dl[3:4], axis=1, keepdims=True)) + lam_init)
    g = g_ref[...]
    outs = []
    for hd in range(2):
        p = [acc_s[0:HEAD_DIM, (2 * hd + mp) * tq:(2 * hd + mp + 1) * tq]
             / acc_s[HEAD_DIM:HEAD_DIM + 1, (2 * hd + mp) * tq:(2 * hd + mp + 1) * tq] for mp in range(2)]
        o = p[0] - lam * p[1]
        ms = jnp.mean(o * o, axis=0, keepdims=True)
        outs.append((o * lax.rsqrt(ms + EPS)) * g * (1.0 - lam_init))
    o_ref[0] = jnp.concatenate(outs, axis=0).T.astype(BF16)


def _diff_attention(qt, k, vt, kmax, diff_lambda, subln, layer_idx, tq):
    bsz, _, s = qt.shape
    n_pairs = N_HEADS_DIFF // 2
    lam_init = 0.8 - 0.6 * math.exp(-0.3 * layer_idx)
    return pl.pallas_call(
        functools.partial(_diff_kernel, lam_init),
        grid=(bsz, n_pairs, s // tq),
        in_specs=[
            pl.BlockSpec(diff_lambda.shape, lambda b, p, i: (0, 0)),
            pl.BlockSpec((HEAD_DIM, 1), lambda b, p, i: (0, 0)),
            pl.BlockSpec((1, kmax.shape[1], 1, PAIR_W), lambda b, p, i: (b, 0, 0, p)),
            pl.BlockSpec((1, PAIR_W, tq), lambda b, p, i: (b, p, i)),
            pl.BlockSpec((1, s, PAIR_W), lambda b, p, i: (b, 0, p)),
            pl.BlockSpec((1, s // TILE, PAIR_W, TILE), lambda b, p, i: (b, 0, p, 0)),
        ],
        out_specs=pl.BlockSpec((1, tq, PAIR_W), lambda b, p, i: (b, i, p)),
        out_shape=jax.ShapeDtypeStruct((bsz, s, W_DIFF), BF16),
        scratch_shapes=[
            pltpu.VMEM((PAIR_W, 4 * tq), BF16),
            pltpu.VMEM((TILE, 4 * tq), F32),
            pltpu.VMEM((1, 4 * tq), F32),
            pltpu.VMEM((HEAD_DIM + SUM_ROWS, 4 * tq), F32),
        ],
        compiler_params=_cparams(("arbitrary", "arbitrary", "arbitrary")),
        name="diff_attention",
    )(diff_lambda, subln.reshape(HEAD_DIM, 1), kmax, qt, k, vt)


def _band_kernel(offsets, window_bias, bmax_ref, kmax_ref, qt_ref, k_ref, vt_ref, bias_ref, o_ref,
                 q_s, sc_s, s_s, acc_s):
    i = pl.program_id(2)
    n_blk = vt_ref.shape[1]
    n_off = len(offsets)
    centre = offsets.index(0)
    n_qt = qt_ref.shape[2] // TILE
    qt = qt_ref[0]
    for t in range(n_qt):
        for hd, qm in enumerate(_masked_queries(qt[:, t * TILE:(t + 1) * TILE], 2)):
            q_s[:, (2 * t + hd) * TILE:(2 * t + hd + 1) * TILE] = qm
    lane_tiles = [slice(c * TILE, (c + 1) * TILE) for c in range(2 * n_qt)]

    def q_tile(ln):
        return i * n_qt + ln.start // (2 * TILE)

    def key_tile(oi, ln):
        return jnp.clip(q_tile(ln) + offsets[oi], 0, n_blk - 1)

    def scores(oi, ln):
        qi = q_tile(ln)
        head_lanes = slice(ln.start % (2 * TILE), ln.start % (2 * TILE) + TILE)
        if window_bias:
            variant = jnp.where(qi == 0, 0, jnp.where(qi == n_blk - 1, 2, 1))
            bias = bias_ref[variant, 0, oi * TILE:(oi + 1) * TILE, head_lanes]
        else:
            kbi = qi + offsets[oi]
            bias = bias_ref[jnp.where((kbi >= 0) & (kbi < n_blk), oi, n_off), :, head_lanes]
        kb = k_ref[0, pl.ds(pl.multiple_of(key_tile(oi, ln) * TILE, TILE), TILE), :]
        return jnp.dot(kb, q_s[:, ln], preferred_element_type=F32) + bias

    def weighted_values(oi, ln, e):
        vtb = vt_ref[0, key_tile(oi, ln)]
        return jnp.dot(_v_with_ones(vtb, (ln.start // TILE) % 2), e, preferred_element_type=F32)

    for ln in lane_tiles:
        sc_s[:, ln] = scores(centre, ln)
    m_c = jnp.max(sc_s[...], axis=0, keepdims=True)
    shift_ok = jnp.max(_score_bound(kmax_ref, q_s) + bmax_ref[...] - m_c) < LAG_MARGIN

    @pl.when(shift_ok)
    def _():
        acc = {ln.start: None for ln in lane_tiles}

        def finish(oi, ln, s):
            pv = weighted_values(oi, ln, jnp.exp2(s - m_c[:, ln]).astype(BF16))
            acc[ln.start] = pv if acc[ln.start] is None else acc[ln.start] + pv

        pending = []
        for oi in range(n_off):
            for ln in lane_tiles:
                pending.append((oi, ln, sc_s[:, ln] if oi == centre else scores(oi, ln)))
                if len(pending) > QK_AHEAD:
                    finish(*pending.pop(0))
        for item in pending:
            finish(*item)
        for ln in lane_tiles:
            acc_s[:, ln] = acc[ln.start]

    @pl.when(jnp.logical_not(shift_ok))
    def _():
        for ln in lane_tiles:
            col_max = m_c[:, ln]
            for oi in range(n_off):
                s_s[oi] = sc_s[:, ln] if oi == centre else scores(oi, ln)
                col_max = jnp.maximum(col_max, jnp.max(s_s[oi], axis=0, keepdims=True))
            acc = None
            for oi in range(n_off):
                pv = weighted_values(oi, ln, jnp.exp2(s_s[oi] - col_max).astype(BF16))
                acc = pv if acc is None else acc + pv
            acc_s[:, ln] = acc

    out = acc_s[0:HEAD_DIM, :] / acc_s[HEAD_DIM:HEAD_DIM + 1, :]
    for t in range(n_qt):
        pair = jnp.concatenate([out[:, lane_tiles[2 * t]], out[:, lane_tiles[2 * t + 1]]], axis=0)
        o_ref[0, t * TILE:(t + 1) * TILE, :] = pair.T.astype(BF16)


def _band_attention(qt, k, vt, kmax, bias, bias_spec, bias_max, offsets, window_bias, q_tiles, pair0, n_pairs,
                    name):
    bsz, _, s = qt.shape
    tq = q_tiles * TILE
    return pl.pallas_call(
        functools.partial(_band_kernel, offsets, window_bias),
        grid=(bsz, n_pairs, s // tq),
        in_specs=[
            pl.BlockSpec((1, 1), lambda b, p, i: (0, 0)),
            pl.BlockSpec((1, kmax.shape[1], 1, PAIR_W), lambda b, p, i: (b, 0, 0, pair0 + p)),
            pl.BlockSpec((1, PAIR_W, tq), lambda b, p, i: (b, pair0 + p, i)),
            pl.BlockSpec((1, s, PAIR_W), lambda b, p, i: (b, 0, pair0 + p)),
            pl.BlockSpec((1, s // TILE, PAIR_W, TILE), lambda b, p, i: (b, 0, pair0 + p, 0)),
            bias_spec,
        ],
        out_specs=pl.BlockSpec((1, tq, PAIR_W), lambda b, p, i: (b, i, p)),
        out_shape=jax.ShapeDtypeStruct((bsz, s, n_pairs * PAIR_W), BF16),
        scratch_shapes=[
            pltpu.VMEM((PAIR_W, 2 * tq), BF16),
            pltpu.VMEM((TILE, 2 * tq), F32),
            pltpu.VMEM((len(offsets), TILE, TILE), F32),
            pltpu.VMEM((HEAD_DIM + SUM_ROWS, 2 * tq), F32),
        ],
        compiler_params=_cparams(("arbitrary", "arbitrary", "arbitrary")),
        name=name,
    )(jnp.reshape(bias_max, (1, 1)).astype(F32), kmax, qt, k, vt, bias)


def _dilated_bias():
    reach = max(w // 2 for w, _ in DIL_PATTERNS)
    n_off = -(-reach // TILE)
    offsets = tuple(range(-n_off, n_off + 1))
    kj = np.arange(TILE)[:, None]
    qi = np.arange(TILE)[None, :]
    tabs = []
    for off in offsets:
        delta = off * TILE + kj - qi
        cnt = np.zeros((TILE, TILE), np.int32)
        for window, dil in DIL_PATTERNS:
            cnt += ((delta % dil == 0) & (np.abs(delta) <= window // 2)).astype(np.int32)
        tabs.append(np.where(cnt > 0, np.log2(np.maximum(cnt, 1).astype(np.float64)), NEG))
    tabs.append(np.full((TILE, TILE), NEG))
    return offsets, jnp.asarray(np.tile(np.stack(tabs), (1, 1, 2)), F32)


def _na_bias_kernel(rows, rpb_ref, o_ref):
    hd = pl.program_id(0)
    kh = min(NA_KH, rows)
    q_rows = TILE // GRID_W
    kc = lax.broadcasted_iota(jnp.int32, (GRID_W, LANES), 0)
    lane = lax.broadcasted_iota(jnp.int32, (GRID_W, LANES), 1)
    qc = lane % GRID_W
    dc = jnp.clip(kc - qc, -(NA_KW - 1), NA_KW - 1) + (NA_KW - 1)
    cs = jnp.clip(qc - NA_KW // 2, 0, GRID_W - NA_KW)
    col_ok = (kc >= cs) & (kc < cs + NA_KW)
    neg = jnp.full((GRID_W, LANES), NEG, F32)
    tiles = []
    for dr in range(2 * NA_KH - 1):
        t = neg
        for j in range(2 * NA_KW - 1):
            t = jnp.where(dc == j, rpb_ref[hd, dr, j] * LOG2E, t)
        tiles.append(jnp.where(col_ok, t, neg))
    for v, r0 in enumerate((0, 2 * q_rows, rows - q_rows)):
        for kr_rel in range(3 * q_rows):
            kr = r0 - q_rows + kr_rel
            for pair in range(q_rows // 2):
                halves = []
                for r in (r0 + 2 * pair, r0 + 2 * pair + 1):
                    rs = min(max(r - kh // 2, 0), rows - kh)
                    halves.append(tiles[kr - r + NA_KH - 1] if rs <= kr < rs + kh else neg)
                blk = jnp.where(lane < GRID_W, halves[0], halves[1])
                o_ref[v, 0, kr_rel * GRID_W:(kr_rel + 1) * GRID_W, pair * LANES:(pair + 1) * LANES] = blk


def _na_bias(rpb, rows):
    n_heads = rpb.shape[0]
    return pl.pallas_call(
        functools.partial(_na_bias_kernel, rows),
        grid=(n_heads,),
        in_specs=[pl.BlockSpec(memory_space=pltpu.SMEM)],
        out_specs=pl.BlockSpec((3, 1, 3 * TILE, TILE), lambda h: (0, h // 2, 0, h % 2)),
        out_shape=jax.ShapeDtypeStruct((3, n_heads // 2, 3 * TILE, 2 * TILE), F32),
        compiler_params=_cparams(("arbitrary",)),
        name="na_bias_table",
    )(rpb)


def _outproj_kernel(x_ref, mod_ref, oa_ref, ob_ref, oc_ref, w_ref, y_ref):
    o = jnp.concatenate([oa_ref[0], ob_ref[0], oc_ref[0]], axis=1)
    mix = jnp.dot(o, w_ref[...], preferred_element_type=F32)
    y_ref[0] = x_ref[0] + mod_ref[0, 2:3, :] * mix


def _out_projection(x, mod_l, o_a, o_b, o_c, w_out, tm):
    bsz, s, d = x.shape
    row = lambda w: pl.BlockSpec((1, tm, w), lambda b, i: (b, i, 0))
    return pl.pallas_call(
        _outproj_kernel,
        grid=(bsz, s // tm),
        in_specs=[
            row(d),
            pl.BlockSpec((1, 6, d), lambda b, i: (b, 0, 0)),
            row(W_DIFF), row(W_DIL), row(W_NA),
            pl.BlockSpec(w_out.shape, lambda b, i: (0, 0)),
        ],
        out_specs=row(d),
        out_shape=jax.ShapeDtypeStruct((bsz, s, d), F32),
        compiler_params=_cparams(("arbitrary", "arbitrary")),
        name="out_proj_residual",
    )(x, mod_l, o_a, o_b, o_c, w_out)


def _ffn_up_kernel(x_ref, xp_ref, xn_ref, mod_ref, g_ref, w_ref, cw_ref, cb_ref, a_ref, h_s, g_s, u_s):
    i = pl.program_id(1)
    tm = x_ref.shape[1]
    g, sc, sh = g_ref[...], mod_ref[0, 4:5, :], mod_ref[0, 3:4, :]
    keep_prev = (i > 0).astype(F32)
    keep_next = (i < pl.num_programs(1) - 1).astype(F32)
    h_s[0:HALO, :] = (_norm_mod(xp_ref[0], g, sc, sh) * keep_prev).astype(BF16)
    h_s[HALO:HALO + tm, :] = _norm_mod(x_ref[0], g, sc, sh).astype(BF16)
    h_s[HALO + tm:, :] = (_norm_mod(xn_ref[0], g, sc, sh) * keep_next).astype(BF16)
    d_ff = cw_ref.shape[1]
    n_ch = d_ff // FF_CHUNK

    def matmuls(c):
        cols = slice(c * FF_CHUNK, (c + 1) * FF_CHUNK)
        g_s[c % 2] = jnp.dot(h_s[...], w_ref[:, cols], preferred_element_type=F32)
        u_s[c % 2] = jnp.dot(h_s[HALO:HALO + tm, :], w_ref[:, d_ff + cols.start:d_ff + cols.stop],
                             preferred_element_type=F32)

    def gate(c):
        cw = cw_ref[:, c * FF_CHUNK:(c + 1) * FF_CHUNK]
        gc = cb_ref[:, c * FF_CHUNK:(c + 1) * FF_CHUNK]
        for t in range(CONV_W):
            gc = gc + g_s[c % 2, pl.ds(HALO - CONV_W // 2 + t, tm), :] * cw[t:t + 1, :]
        a_ref[0, :, c * FF_CHUNK:(c + 1) * FF_CHUNK] = (gc * jax.nn.sigmoid(gc) * u_s[c % 2]).astype(BF16)

    matmuls(0)
    for c in range(n_ch):
        if c + 1 < n_ch:
            matmuls(c + 1)
        gate(c)


def _ffn_up(x, mod_l, g, w_up, cw, cb, tm):
    bsz, s, d = x.shape
    d_ff = cw.shape[1]
    hb = tm // HALO
    n_hb = s // HALO
    return pl.pallas_call(
        _ffn_up_kernel,
        grid=(bsz, s // tm),
        in_specs=[
            pl.BlockSpec((1, tm, d), lambda b, i: (b, i, 0)),
            pl.BlockSpec((1, HALO, d), lambda b, i: (b, jnp.maximum(i * hb - 1, 0), 0)),
            pl.BlockSpec((1, HALO, d), lambda b, i: (b, jnp.minimum((i + 1) * hb, n_hb - 1), 0)),
            pl.BlockSpec((1, 6, d), lambda b, i: (b, 0, 0)),
            pl.BlockSpec((1, d), lambda b, i: (0, 0)),
            pl.BlockSpec(w_up.shape, lambda b, i: (0, 0)),
            pl.BlockSpec(cw.shape, lambda b, i: (0, 0)),
            pl.BlockSpec(cb.shape, lambda b, i: (0, 0)),
        ],
        out_specs=pl.BlockSpec((1, tm, d_ff), lambda b, i: (b, i, 0)),
        out_shape=jax.ShapeDtypeStruct((bsz, s, d_ff), BF16),
        scratch_shapes=[
            pltpu.VMEM((tm + 2 * HALO, d), BF16),
            pltpu.VMEM((2, tm + 2 * HALO, FF_CHUNK), F32),
            pltpu.VMEM((2, tm, FF_CHUNK), F32),
        ],
        compiler_params=_cparams(("arbitrary", "arbitrary")),
        name="ffn_up_conv_glu",
    )(x, x, x, mod_l, g.reshape(1, d), w_up, cw, cb)


def _ffn_down_kernel(final, x_ref, mod_ref, a_ref, w_ref, gf_ref, y_ref):
    y = x_ref[0] + mod_ref[0, 5:6, :] * jnp.dot(a_ref[0], w_ref[...], preferred_element_type=F32)
    if final:
        ms = jnp.mean(y * y, axis=-1, keepdims=True)
        y = (y * lax.rsqrt(ms + EPS)) * gf_ref[...]
    y_ref[0] = y


def _ffn_down(x, mod_l, a, w_down, g_final, final, tm):
    bsz, s, d = x.shape
    return pl.pallas_call(
        functools.partial(_ffn_down_kernel, final),
        grid=(bsz, s // tm),
        in_specs=[
            pl.BlockSpec((1, tm, d), lambda b, i: (b, i, 0)),
            pl.BlockSpec((1, 6, d), lambda b, i: (b, 0, 0)),
            pl.BlockSpec((1, tm, a.shape[2]), lambda b, i: (b, i, 0)),
            pl.BlockSpec(w_down.shape, lambda b, i: (0, 0)),
            pl.BlockSpec((1, d), lambda b, i: (0, 0)),
        ],
        out_specs=pl.BlockSpec((1, tm, d), lambda b, i: (b, i, 0)),
        out_shape=jax.ShapeDtypeStruct((bsz, s, d), F32),
        compiler_params=_cparams(("arbitrary", "arbitrary")),
        name="ffn_down_residual",
    )(x, mod_l, a, w_down, g_final.reshape(1, d))


def kernel(x, c, w_ada, b_ada, g_attn, w_in, diff_lambda, diff_subln, na_rpb, w_out, g_ffn, w_up,
           conv_w, conv_b, w_down, g_final):
    bsz, s, d = x.shape
    depth = w_ada.shape[0]
    d_ff = w_down.shape[1]
    tm = 512
    assert s % tm == 0 and tm % TILE == 0 and d_ff % FF_CHUNK == 0 and TILE % (2 * GRID_W) == 0
    assert (s // TILE) % max(DIL_Q_TILES, NA_Q_TILES) == 0 and s // GRID_W >= 4 * (TILE // GRID_W)

    mod = _modulation(c, w_ada, b_ada).reshape(depth, bsz, 6, d)
    tables = _rope_tables(s, DIFF_QK_DIM) + _rope_tables(s, HEAD_DIM)
    dil_offsets, dil_bias = _dilated_bias()
    rows = s // GRID_W

    wa, wb, wc = W_DIFF, W_DIL, W_NA
    offs = np.cumsum([0, wa, wa, wa, wb, wb, wb, wc, wc, wc])
    order = (0, 3, 6, 1, 4, 7, 2, 5, 8)

    for l in range(depth):
        w_perm = jnp.concatenate([w_in[l][:, offs[j]:offs[j + 1]] for j in order], axis=1).astype(BF16)
        qt, k, vt, kmax = _projection(x, mod[l], g_attn[l], w_perm, tables, 2 * tm)

        o_a = _diff_attention(qt, k, vt, kmax, diff_lambda[l], diff_subln[l], l, 2 * TILE)
        o_b = _band_attention(
            qt, k, vt, kmax, dil_bias,
            pl.BlockSpec(dil_bias.shape, lambda b, p, i: (0, 0, 0)),
            jnp.float32(math.log2(len(DIL_PATTERNS))),
            dil_offsets, False, DIL_Q_TILES, N_HEADS_DIFF // 2, N_HEADS_DIL // 2, "dilated_attention")
        o_c = _band_attention(
            qt, k, vt, kmax, _na_bias(na_rpb[l], rows),
            pl.BlockSpec((3, 1, 3 * TILE, 2 * TILE), lambda b, p, i: (0, p, 0, 0)),
            jnp.max(na_rpb[l]) * LOG2E,
            (-1, 0, 1), True, NA_Q_TILES, (N_HEADS_DIFF + N_HEADS_DIL) // 2, N_HEADS_NA // 2,
            "neighbourhood_attention")

        x = _out_projection(x, mod[l], o_a, o_b, o_c, w_out[l].astype(BF16), 2 * tm)

        a = _ffn_up(x, mod[l], g_ffn[l], w_up[l].astype(BF16), conv_w[l], conv_b[l].reshape(1, d_ff), 2 * tm)
        x = _ffn_down(x, mod[l], a, w_down[l].astype(BF16), g_final, l == depth - 1, 2 * tm)
    return x
```

```python
import functools
import math

import numpy as np
import jax
import jax.numpy as jnp
from jax import lax
from jax.experimental import pallas as pl
from jax.experimental.pallas import tpu as pltpu

F32 = jnp.float32
BF16 = jnp.bfloat16

HEAD_DIM = 64
N_HEADS_DIFF = 4
N_HEADS_DIL = 6
N_HEADS_NA = 6
W_DIFF = N_HEADS_DIFF * HEAD_DIM
W_DIL = N_HEADS_DIL * HEAD_DIM
W_NA = N_HEADS_NA * HEAD_DIM
DIFF_QK_DIM = HEAD_DIM // 2
DIL_PATTERNS = ((128, 1), (512, 4), (2048, 16))
GRID_W = 64
NA_KH = 8
NA_KW = 16
CONV_W = 3
ROPE_THETA = 10000.0
EPS = 1e-6
NEG = -1e30

LANES = 128
PAIR_W = 2 * HEAD_DIM
TILE = 256
ROW_TILE = 1024
FF_CHUNK = 256
HALO = 16
VMEM_LIMIT = 56 * 1024 * 1024
LOG2E = math.log2(math.e)
SUM_ROWS = 16
LAG_MARGIN = 64.0
DIL_Q_TILES = 8
NA_Q_TILES = 8
LAG_UNROLL = 63
QK_AHEAD = 4


def _cparams(sem):
    return pltpu.CompilerParams(dimension_semantics=sem, vmem_limit_bytes=VMEM_LIMIT)


def _mod_kernel(ct_ref, w_ref, b_ref, o_ref):
    ct = ct_ref[...]
    s = ct * jax.nn.sigmoid(ct)
    w = w_ref[0]
    rows = [jnp.sum(w * s[:, b:b + 1], axis=0, keepdims=True) for b in range(ct.shape[1])]
    o_ref[0] = jnp.concatenate(rows, axis=0) + b_ref[0]


def _modulation(c, w_ada, b_ada):
    depth, d, n = w_ada.shape
    bsz = c.shape[0]
    tn = 768
    return pl.pallas_call(
        _mod_kernel,
        grid=(depth, n // tn),
        in_specs=[
            pl.BlockSpec((d, bsz), lambda l, j: (0, 0)),
            pl.BlockSpec((1, d, tn), lambda l, j: (l, 0, j)),
            pl.BlockSpec((1, 1, tn), lambda l, j: (l, 0, j)),
        ],
        out_specs=pl.BlockSpec((1, bsz, tn), lambda l, j: (l, 0, j)),
        out_shape=jax.ShapeDtypeStruct((depth, bsz, n), F32),
        compiler_params=_cparams(("arbitrary", "arbitrary")),
        name="adaln_mod",
    )(c.T, w_ada, b_ada.reshape(depth, 1, n))


def _norm_mod(x, g, sc, sh):
    ms = jnp.mean(x * x, axis=-1, keepdims=True)
    return (x * lax.rsqrt(ms + EPS)) * g * (1.0 + sc) + sh


def _rope_slab(acc, cos, sin, group, width):
    half = group // 2
    lane = lax.broadcasted_iota(jnp.int32, (1, LANES), 1) % group
    first = lane < half
    outs = []
    for c in range(width // LANES):
        xc = acc[:, c * LANES:(c + 1) * LANES]
        swapped = jnp.where(first, pltpu.roll(xc, LANES - half, axis=1), pltpu.roll(xc, half, axis=1))
        outs.append(xc * cos + swapped * sin)
    if width < acc.shape[1]:
        outs.append(acc[:, width:])
    return jnp.concatenate(outs, axis=1)


def _proj_kernel(x_ref, mod_ref, g_ref, w_ref, ca_ref, sa_ref, cb_ref, sb_ref,
                 qt_ref, k_ref, vt_ref, kmax_ref):
    tm = x_ref.shape[1]
    h = _norm_mod(x_ref[0], g_ref[...], mod_ref[0, 1:2, :], mod_ref[0, 0:1, :]).astype(BF16)
    ca, sa, cb, sb = ca_ref[...], sa_ref[...], cb_ref[...], sb_ref[...]
    width = W_DIFF + W_DIL + W_NA
    slabs = ((0, W_DIFF), (W_DIFF, W_DIL + W_NA))

    def slab(part, si):
        off, w = slabs[si]
        acc = jnp.dot(h, w_ref[:, part * width + off: part * width + off + w], preferred_element_type=F32)
        if part < 2 and si == 0:
            acc = _rope_slab(acc, ca, sa, DIFF_QK_DIM, W_DIFF)
        elif part < 2 and si == 1:
            acc = _rope_slab(acc, cb, sb, HEAD_DIM, W_DIL)
        return acc

    q_scale = (LOG2E * DIFF_QK_DIM ** -0.5, LOG2E * HEAD_DIM ** -0.5)
    for si, (off, w) in enumerate(slabs):
        q = slab(0, si) * q_scale[si]
        qt_ref[0, off:off + w, :] = q.T.astype(BF16)
        kb = slab(1, si).astype(BF16)
        k_ref[0, :, off:off + w] = kb
        kmax_ref[0, 0, :, off:off + w] = jnp.max(jnp.abs(kb.astype(F32)), axis=0, keepdims=True)
        v = slab(2, si)
        for t in range(tm // TILE):
            vt_ref[0, t, off:off + w, :] = v[t * TILE:(t + 1) * TILE, :].T.astype(BF16)


def _rope_tables(s, group):
    half = group // 2
    inv = ROPE_THETA ** (-jnp.arange(half, dtype=F32) / half)
    ang = inv[:, None] * jnp.arange(s, dtype=F32)[None, :]
    cos, sin = jnp.cos(ang), jnp.sin(ang)
    reps = LANES // group
    return (jnp.tile(jnp.concatenate([cos, cos], axis=0), (reps, 1)).T,
            jnp.tile(jnp.concatenate([-sin, sin], axis=0), (reps, 1)).T)


def _projection(x, mod_l, g, w_perm, tables, tm):
    bsz, s, d = x.shape
    width = W_DIFF + W_DIL + W_NA
    tab_spec = pl.BlockSpec((tm, LANES), lambda b, i: (i, 0))
    return pl.pallas_call(
        _proj_kernel,
        grid=(bsz, s // tm),
        in_specs=[
            pl.BlockSpec((1, tm, d), lambda b, i: (b, i, 0)),
            pl.BlockSpec((1, 6, d), lambda b, i: (b, 0, 0)),
            pl.BlockSpec((1, d), lambda b, i: (0, 0)),
            pl.BlockSpec((d, 3 * width), lambda b, i: (0, 0)),
            tab_spec, tab_spec, tab_spec, tab_spec,
        ],
        out_specs=[
            pl.BlockSpec((1, width, tm), lambda b, i: (b, 0, i)),
            pl.BlockSpec((1, tm, width), lambda b, i: (b, i, 0)),
            pl.BlockSpec((1, tm // TILE, width, TILE), lambda b, i: (b, i, 0, 0)),
            pl.BlockSpec((1, 1, 1, width), lambda b, i: (b, i, 0, 0)),
        ],
        out_shape=[
            jax.ShapeDtypeStruct((bsz, width, s), BF16),
            jax.ShapeDtypeStruct((bsz, s, width), BF16),
            jax.ShapeDtypeStruct((bsz, s // TILE, width, TILE), BF16),
            jax.ShapeDtypeStruct((bsz, s // tm, 1, width), F32),
        ],
        compiler_params=_cparams(("arbitrary", "arbitrary")),
        name="norm_proj_rope",
    )(x, mod_l, g.reshape(1, d), w_perm, *tables)


def _masked_queries(qt, n_split):
    row = lax.broadcasted_iota(jnp.int32, (PAIR_W, 1), 0)
    step = PAIR_W // n_split
    return [jnp.where((row >= j * step) & (row < (j + 1) * step), qt, jnp.zeros_like(qt))
            for j in range(n_split)]


def _v_with_ones(vtb, hd):
    ones = jnp.ones((SUM_ROWS, vtb.shape[1]), BF16)
    return jnp.concatenate([vtb[hd * HEAD_DIM:(hd + 1) * HEAD_DIM, :], ones], axis=0)


def _score_bound(kmax_ref, q_s):
    kmax = jnp.max(kmax_ref[0], axis=0) * (1.0 + 2.0 ** -7)
    kmax = jnp.broadcast_to(kmax, (SUM_ROWS, PAIR_W)).astype(BF16)
    return jnp.dot(kmax, jnp.abs(q_s[...]), preferred_element_type=F32)[0:1] * (1.0 + 2.0 ** -7)


def _diff_kernel(lam_init, dl_ref, g_ref, kmax_ref, qt_ref, k_ref, vt_ref, o_ref, q_s, s_s, m_s, acc_s):
    tq = qt_ref.shape[2]
    n_kt = vt_ref.shape[1]
    for j, qm in enumerate(_masked_queries(qt_ref[0], 4)):
        q_s[:, j * tq:(j + 1) * tq] = qm
    lane_tiles = [slice(nt * TILE, (nt + 1) * TILE) for nt in range(4 * tq // TILE)]

    def k_tile(kt):
        return k_ref[0, pl.ds(pl.multiple_of(kt * TILE, TILE), TILE), :]

    def v_tiles(kt):
        vtb = vt_ref[0, kt]
        return [_v_with_ones(vtb, hd) for hd in range(2)]

    def exact_step(kt, first):
        kb, v_ext = k_tile(kt), v_tiles(kt)
        for ln in lane_tiles:
            s_s[:, ln] = jnp.dot(kb, q_s[:, ln], preferred_element_type=F32)
        for ln in lane_tiles:
            tile_max = jnp.max(s_s[:, ln], axis=0, keepdims=True)
            m_new = tile_max if first else jnp.maximum(m_s[:, ln], tile_max)
            e = jnp.exp2(s_s[:, ln] - m_new).astype(BF16)
            pv = jnp.dot(v_ext[ln.start // (2 * tq)], e, preferred_element_type=F32)
            acc_s[:, ln] = pv if first else acc_s[:, ln] * jnp.exp2(m_s[:, ln] - m_new) + pv
            m_s[:, ln] = m_new

    def fixed_shift_steps(kts):
        def finish(vtb, ln, s):
            hd = ln.start // (2 * tq)
            e = jnp.exp2(s - m_s[:, ln])
            acc_s[0:HEAD_DIM, ln] += jnp.dot(vtb[hd * HEAD_DIM:(hd + 1) * HEAD_DIM, :], e.astype(BF16),
                                             preferred_element_type=F32)
            acc_s[HEAD_DIM:HEAD_DIM + 1, ln] += jnp.sum(e, axis=0, keepdims=True)

        pending = []
        for kt in kts:
            kb, vtb = k_tile(kt), vt_ref[0, kt]
            for ln in lane_tiles:
                pending.append((vtb, ln, jnp.dot(kb, q_s[:, ln], preferred_element_type=F32)))
                if len(pending) > QK_AHEAD:
                    finish(*pending.pop(0))
        for item in pending:
            finish(*item)

    exact_step(0, True)
    lag_ok = jnp.max(_score_bound(kmax_ref, q_s) - m_s[...]) < LAG_MARGIN

    @pl.when(lag_ok)
    def _():
        def body(tt, carry):
            fixed_shift_steps([LAG_UNROLL * tt + 1 + u for u in range(LAG_UNROLL)])
            return carry
        n_trips = (n_kt - 1) // LAG_UNROLL
        lax.fori_loop(0, n_trips, body, 0)
        if n_trips * LAG_UNROLL + 1 < n_kt:
            fixed_shift_steps(list(range(n_trips * LAG_UNROLL + 1, n_kt)))

    @pl.when(jnp.logical_not(lag_ok))
    def _():
        def body(kt, carry):
            exact_step(kt, False)
            return carry
        lax.fori_loop(1, n_kt, body, 0)

    dl = dl_ref[...]
    lam = (jnp.exp(jnp.sum(dl[0:1] * dl[1:2], axis=1, keepdims=True))
           - jnp.exp(jnp.sum(dl[2:3] * dl[3:4], axis=1, keepdims=True)) + lam_init)
    g = g_ref[...]
    outs = []
    for hd in range(2):
        p = [acc_s[0:HEAD_DIM, (2 * hd + mp) * tq:(2 * hd + mp + 1) * tq]
             / acc_s[HEAD_DIM:HEAD_DIM + 1, (2 * hd + mp) * tq:(2 * hd + mp + 1) * tq] for mp in range(2)]
        o = p[0] - lam * p[1]
        ms = jnp.mean(o * o, axis=0, keepdims=True)
        outs.append((o * lax.rsqrt(ms + EPS)) * g * (1.0 - lam_init))
    o_ref[0] = jnp.concatenate(outs, axis=0).T.astype(BF16)


def _diff_attention(qt, k, vt, kmax, diff_lambda, subln, layer_idx, tq):
    bsz, _, s = qt.shape
    n_pairs = N_HEADS_DIFF // 2
    lam_init = 0.8 - 0.6 * math.exp(-0.3 * layer_idx)
    return pl.pallas_call(
        functools.partial(_diff_kernel, lam_init),
        grid=(bsz, n_pairs, s // tq),
        in_specs=[
            pl.BlockSpec(diff_lambda.shape, lambda b, p, i: (0, 0)),
            pl.BlockSpec((HEAD_DIM, 1), lambda b, p, i: (0, 0)),
            pl.BlockSpec((1, kmax.shape[1], 1, PAIR_W), lambda b, p, i: (b, 0, 0, p)),
            pl.BlockSpec((1, PAIR_W, tq), lambda b, p, i: (b, p, i)),
            pl.BlockSpec((1, s, PAIR_W), lambda b, p, i: (b, 0, p)),
            pl.BlockSpec((1, s // TILE, PAIR_W, TILE), lambda b, p, i: (b, 0, p, 0)),
        ],
        out_specs=pl.BlockSpec((1, tq, PAIR_W), lambda b, p, i: (b, i, p)),
        out_shape=jax.ShapeDtypeStruct((bsz, s, W_DIFF), BF16),
        scratch_shapes=[
            pltpu.VMEM((PAIR_W, 4 * tq), BF16),
            pltpu.VMEM((TILE, 4 * tq), F32),
            pltpu.VMEM((1, 4 * tq), F32),
            pltpu.VMEM((HEAD_DIM + SUM_ROWS, 4 * tq), F32),
        ],
        compiler_params=_cparams(("arbitrary", "arbitrary", "arbitrary")),
        name="diff_attention",
    )(diff_lambda, subln.reshape(HEAD_DIM, 1), kmax, qt, k, vt)


def _band_kernel(offsets, window_bias, bmax_ref, kmax_ref, qt_ref, k_ref, vt_ref, bias_ref, o_ref,
                 q_s, sc_s, s_s, acc_s):
    i = pl.program_id(2)
    n_blk = vt_ref.shape[1]
    n_off = len(offsets)
    centre = offsets.index(0)
    n_qt = qt_ref.shape[2] // TILE
    qt = qt_ref[0]
    for t in range(n_qt):
        for hd, qm in enumerate(_masked_queries(qt[:, t * TILE:(t + 1) * TILE], 2)):
            q_s[:, (2 * t + hd) * TILE:(2 * t + hd + 1) * TILE] = qm
    lane_tiles = [slice(c * TILE, (c + 1) * TILE) for c in range(2 * n_qt)]

    def q_tile(ln):
        return i * n_qt + ln.start // (2 * TILE)

    def key_tile(oi, ln):
        return jnp.clip(q_tile(ln) + offsets[oi], 0, n_blk - 1)

    def scores(oi, ln):
        qi = q_tile(ln)
        head_lanes = slice(ln.start % (2 * TILE), ln.start % (2 * TILE) + TILE)
        if window_bias:
            variant = jnp.where(qi == 0, 0, jnp.where(qi == n_blk - 1, 2, 1))
            bias = bias_ref[variant, 0, oi * TILE:(oi + 1) * TILE, head_lanes]
        else:
            kbi = qi + offsets[oi]
            bias = bias_ref[jnp.where((kbi >= 0) & (kbi < n_blk), oi, n_off), :, head_lanes]
        kb = k_ref[0, pl.ds(pl.multiple_of(key_tile(oi, ln) * TILE, TILE), TILE), :]
        return jnp.dot(kb, q_s[:, ln], preferred_element_type=F32) + bias

    def weighted_values(oi, ln, e):
        vtb = vt_ref[0, key_tile(oi, ln)]
        return jnp.dot(_v_with_ones(vtb, (ln.start // TILE) % 2), e, preferred_element_type=F32)

    for ln in lane_tiles:
        sc_s[:, ln] = scores(centre, ln)
    m_c = jnp.max(sc_s[...], axis=0, keepdims=True)
    shift_ok = jnp.max(_score_bound(kmax_ref, q_s) + bmax_ref[...] - m_c) < LAG_MARGIN

    @pl.when(shift_ok)
    def _():
        acc = {ln.start: None for ln in lane_tiles}

        def finish(oi, ln, s):
            pv = weighted_values(oi, ln, jnp.exp2(s - m_c[:, ln]).astype(BF16))
            acc[ln.start] = pv if acc[ln.start] is None else acc[ln.start] + pv

        pending = []
        for oi in range(n_off):
            for ln in lane_tiles:
                pending.append((oi, ln, sc_s[:, ln] if oi == centre else scores(oi, ln)))
                if len(pending) > QK_AHEAD:
                    finish(*pending.pop(0))
        for item in pending:
            finish(*item)
        for ln in lane_tiles:
            acc_s[:, ln] = acc[ln.start]

    @pl.when(jnp.logical_not(shift_ok))
    def _():
        for ln in lane_tiles:
            col_max = m_c[:, ln]
            for oi in range(n_off):
                s_s[oi] = sc_s[:, ln] if oi == centre else scores(oi, ln)
                col_max = jnp.maximum(col_max, jnp.max(s_s[oi], axis=0, keepdims=True))
            acc = None
            for oi in range(n_off):
                pv = weighted_values(oi, ln, jnp.exp2(s_s[oi] - col_max).astype(BF16))
                acc = pv if acc is None else acc + pv
            acc_s[:, ln] = acc

    out = acc_s[0:HEAD_DIM, :] / acc_s[HEAD_DIM:HEAD_DIM + 1, :]
    for t in range(n_qt):
        pair = jnp.concatenate([out[:, lane_tiles[2 * t]], out[:, lane_tiles[2 * t + 1]]], axis=0)
        o_ref[0, t * TILE:(t + 1) * TILE, :] = pair.T.astype(BF16)


def _band_attention(qt, k, vt, kmax, bias, bias_spec, bias_max, offsets, window_bias, q_tiles, pair0, n_pairs,
                    name):
    bsz, _, s = qt.shape
    tq = q_tiles * TILE
    return pl.pallas_call(
        functools.partial(_band_kernel, offsets, window_bias),
        grid=(bsz, n_pairs, s // tq),
        in_specs=[
            pl.BlockSpec((1, 1), lambda b, p, i: (0, 0)),
            pl.BlockSpec((1, kmax.shape[1], 1, PAIR_W), lambda b, p, i: (b, 0, 0, pair0 + p)),
            pl.BlockSpec((1, PAIR_W, tq), lambda b, p, i: (b, pair0 + p, i)),
            pl.BlockSpec((1, s, PAIR_W), lambda b, p, i: (b, 0, pair0 + p)),
            pl.BlockSpec((1, s // TILE, PAIR_W, TILE), lambda b, p, i: (b, 0, pair0 + p, 0)),
            bias_spec,
        ],
        out_specs=pl.BlockSpec((1, tq, PAIR_W), lambda b, p, i: (b, i, p)),
        out_shape=jax.ShapeDtypeStruct((bsz, s, n_pairs * PAIR_W), BF16),
        scratch_shapes=[
            pltpu.VMEM((PAIR_W, 2 * tq), BF16),
            pltpu.VMEM((TILE, 2 * tq), F32),
            pltpu.VMEM((len(offsets), TILE, TILE), F32),
            pltpu.VMEM((HEAD_DIM + SUM_ROWS, 2 * tq), F32),
        ],
        compiler_params=_cparams(("arbitrary", "arbitrary", "arbitrary")),
        name=name,
    )(jnp.reshape(bias_max, (1, 1)).astype(F32), kmax, qt, k, vt, bias)


def _dilated_bias():
    reach = max(w // 2 for w, _ in DIL_PATTERNS)
    n_off = -(-reach // TILE)
    offsets = tuple(range(-n_off, n_off + 1))
    kj = np.arange(TILE)[:, None]
    qi = np.arange(TILE)[None, :]
    tabs = []
    for off in offsets:
        delta = off * TILE + kj - qi
        cnt = np.zeros((TILE, TILE), np.int32)
        for window, dil in DIL_PATTERNS:
            cnt += ((delta % dil == 0) & (np.abs(delta) <= window // 2)).astype(np.int32)
        tabs.append(np.where(cnt > 0, np.log2(np.maximum(cnt, 1).astype(np.float64)), NEG))
    tabs.append(np.full((TILE, TILE), NEG))
    return offsets, jnp.asarray(np.tile(np.stack(tabs), (1, 1, 2)), F32)


def _na_bias_kernel(rows, rpb_ref, o_ref):
    hd = pl.program_id(0)
    kh = min(NA_KH, rows)
    q_rows = TILE // GRID_W
    kc = lax.broadcasted_iota(jnp.int32, (GRID_W, LANES), 0)
    lane = lax.broadcasted_iota(jnp.int32, (GRID_W, LANES), 1)
    qc = lane % GRID_W
    dc = jnp.clip(kc - qc, -(NA_KW - 1), NA_KW - 1) + (NA_KW - 1)
    cs = jnp.clip(qc - NA_KW // 2, 0, GRID_W - NA_KW)
    col_ok = (kc >= cs) & (kc < cs + NA_KW)
    neg = jnp.full((GRID_W, LANES), NEG, F32)
    tiles = []
    for dr in range(2 * NA_KH - 1):
        t = neg
        for j in range(2 * NA_KW - 1):
            t = jnp.where(dc == j, rpb_ref[hd, dr, j] * LOG2E, t)
        tiles.append(jnp.where(col_ok, t, neg))
    for v, r0 in enumerate((0, 2 * q_rows, rows - q_rows)):
        for kr_rel in range(3 * q_rows):
            kr = r0 - q_rows + kr_rel
            for pair in range(q_rows // 2):
                halves = []
                for r in (r0 + 2 * pair, r0 + 2 * pair + 1):
                    rs = min(max(r - kh // 2, 0), rows - kh)
                    halves.append(tiles[kr - r + NA_KH - 1] if rs <= kr < rs + kh else neg)
                blk = jnp.where(lane < GRID_W, halves[0], halves[1])
                o_ref[v, 0, kr_rel * GRID_W:(kr_rel + 1) * GRID_W, pair * LANES:(pair + 1) * LANES] = blk


def _na_bias(rpb, rows):
    n_heads = rpb.shape[0]
    return pl.pallas_call(
        functools.partial(_na_bias_kernel, rows),
        grid=(n_heads,),
        in_specs=[pl.BlockSpec(memory_space=pltpu.SMEM)],
        out_specs=pl.BlockSpec((3, 1, 3 * TILE, TILE), lambda h: (0, h // 2, 0, h % 2)),
        out_shape=jax.ShapeDtypeStruct((3, n_heads // 2, 3 * TILE, 2 * TILE), F32),
        compiler_params=_cparams(("arbitrary",)),
        name="na_bias_table",
    )(rpb)


def _outproj_kernel(x_ref, mod_ref, oa_ref, ob_ref, oc_ref, w_ref, y_ref):
    o = jnp.concatenate([oa_ref[0], ob_ref[0], oc_ref[0]], axis=1)
    mix = jnp.dot(o, w_ref[...], preferred_element_type=F32)
    y_ref[0] = x_ref[0] + mod_ref[0, 2:3, :] * mix


def _out_projection(x, mod_l, o_a, o_b, o_c, w_out, tm):
    bsz, s, d = x.shape
    row = lambda w: pl.BlockSpec((1, tm, w), lambda b, i: (b, i, 0))
    return pl.pallas_call(
        _outproj_kernel,
        grid=(bsz, s // tm),
        in_specs=[
            row(d),
            pl.BlockSpec((1, 6, d), lambda b, i: (b, 0, 0)),
            row(W_DIFF), row(W_DIL), row(W_NA),
            pl.BlockSpec(w_out.shape, lambda b, i: (0, 0)),
        ],
        out_specs=row(d),
        out_shape=jax.ShapeDtypeStruct((bsz, s, d), F32),
        compiler_params=_cparams(("arbitrary", "arbitrary")),
        name="out_proj_residual",
    )(x, mod_l, o_a, o_b, o_c, w_out)


def _ffn_up_kernel(x_ref, xp_ref, xn_ref, mod_ref, g_ref, w_ref, cw_ref, cb_ref, a_ref, h_s, g_s, u_s):
    i = pl.program_id(1)
    tm = x_ref.shape[1]
    g, sc, sh = g_ref[...], mod_ref[0, 4:5, :], mod_ref[0, 3:4, :]
    keep_prev = (i > 0).astype(F32)
    keep_next = (i < pl.num_programs(1) - 1).astype(F32)
    h_s[0:HALO, :] = (_norm_mod(xp_ref[0], g, sc, sh) * keep_prev).astype(BF16)
    h_s[HALO:HALO + tm, :] = _norm_mod(x_ref[0], g, sc, sh).astype(BF16)
    h_s[HALO + tm:, :] = (_norm_mod(xn_ref[0], g, sc, sh) * keep_next).astype(BF16)
    d_ff = cw_ref.shape[1]
    n_ch = d_ff // FF_CHUNK

    def matmuls(c):
        cols = slice(c * FF_CHUNK, (c + 1) * FF_CHUNK)
        g_s[c % 2] = jnp.dot(h_s[...], w_ref[:, cols], preferred_element_type=F32)
        u_s[c % 2] = jnp.dot(h_s[HALO:HALO + tm, :], w_ref[:, d_ff + cols.start:d_ff + cols.stop],
                             preferred_element_type=F32)

    def gate(c):
        cw = cw_ref[:, c * FF_CHUNK:(c + 1) * FF_CHUNK]
        gc = cb_ref[:, c * FF_CHUNK:(c + 1) * FF_CHUNK]
        for t in range(CONV_W):
            gc = gc + g_s[c % 2, pl.ds(HALO - CONV_W // 2 + t, tm), :] * cw[t:t + 1, :]
        a_ref[0, :, c * FF_CHUNK:(c + 1) * FF_CHUNK] = (gc * jax.nn.sigmoid(gc) * u_s[c % 2]).astype(BF16)

    matmuls(0)
    for c in range(n_ch):
        if c + 1 < n_ch:
            matmuls(c + 1)
        gate(c)


def _ffn_up(x, mod_l, g, w_up, cw, cb, tm):
    bsz, s, d = x.shape
    d_ff = cw.shape[1]
    hb = tm // HALO
    n_hb = s // HALO
    return pl.pallas_call(
        _ffn_up_kernel,
        grid=(bsz, s // tm),
        in_specs=[
            pl.BlockSpec((1, tm, d), lambda b, i: (b, i, 0)),
            pl.BlockSpec((1, HALO, d), lambda b, i: (b, jnp.maximum(i * hb - 1, 0), 0)),
            pl.BlockSpec((1, HALO, d), lambda b, i: (b, jnp.minimum((i + 1) * hb, n_hb - 1), 0)),
            pl.BlockSpec((1, 6, d), lambda b, i: (b, 0, 0)),
            pl.BlockSpec((1, d), lambda b, i: (0, 0)),
            pl.BlockSpec(w_up.shape, lambda b, i: (0, 0)),
            pl.BlockSpec(cw.shape, lambda b, i: (0, 0)),
            pl.BlockSpec(cb.shape, lambda b, i: (0, 0)),
        ],
        out_specs=pl.BlockSpec((1, tm, d_ff), lambda b, i: (b, i, 0)),
        out_shape=jax.ShapeDtypeStruct((bsz, s, d_ff), BF16),
        scratch_shapes=[
            pltpu.VMEM((tm + 2 * HALO, d), BF16),
            pltpu.VMEM((2, tm + 2 * HALO, FF_CHUNK), F32),
            pltpu.VMEM((2, tm, FF_CHUNK), F32),
        ],
        compiler_params=_cparams(("arbitrary", "arbitrary")),
        name="ffn_up_conv_glu",
    )(x, x, x, mod_l, g.reshape(1, d), w_up, cw, cb)


def _ffn_down_kernel(final, x_ref, mod_ref, a_ref, w_ref, gf_ref, y_ref):
    y = x_ref[0] + mod_ref[0, 5:6, :] * jnp.dot(a_ref[0], w_ref[...], preferred_element_type=F32)
    if final:
        ms = jnp.mean(y * y, axis=-1, keepdims=True)
        y = (y * lax.rsqrt(ms + EPS)) * gf_ref[...]
    y_ref[0] = y


def _ffn_down(x, mod_l, a, w_down, g_final, final, tm):
    bsz, s, d = x.shape
    return pl.pallas_call(
        functools.partial(_ffn_down_kernel, final),
        grid=(bsz, s // tm),
        in_specs=[
            pl.BlockSpec((1, tm, d), lambda b, i: (b, i, 0)),
            pl.BlockSpec((1, 6, d), lambda b, i: (b, 0, 0)),
            pl.BlockSpec((1, tm, a.shape[2]), lambda b, i: (b, i, 0)),
            pl.BlockSpec(w_down.shape, lambda b, i: (0, 0)),
            pl.BlockSpec((1, d), lambda b, i: (0, 0)),
        ],
        out_specs=pl.BlockSpec((1, tm, d), lambda b, i: (b, i, 0)),
        out_shape=jax.ShapeDtypeStruct((bsz, s, d), F32),
        compiler_params=_cparams(("arbitrary", "arbitrary")),
        name="ffn_down_residual",
    )(x, mod_l, a, w_down, g_final.reshape(1, d))


def kernel(x, c, w_ada, b_ada, g_attn, w_in, diff_lambda, diff_subln, na_rpb, w_out, g_ffn, w_up,
           conv_w, conv_b, w_down, g_final):
    bsz, s, d = x.shape
    depth = w_ada.shape[0]
    d_ff = w_down.shape[1]
    tm = ROW_TILE
    assert s % tm == 0 and tm % TILE == 0 and d_ff % FF_CHUNK == 0 and TILE % (2 * GRID_W) == 0
    assert (s // TILE) % max(DIL_Q_TILES, NA_Q_TILES) == 0 and s // GRID_W >= 4 * (TILE // GRID_W)

    mod = _modulation(c, w_ada, b_ada).reshape(depth, bsz, 6, d)
    tables = _rope_tables(s, DIFF_QK_DIM) + _rope_tables(s, HEAD_DIM)
    dil_offsets, dil_bias = _dilated_bias()
    rows = s // GRID_W

    wa, wb, wc = W_DIFF, W_DIL, W_NA
    offs = np.cumsum([0, wa, wa, wa, wb, wb, wb, wc, wc, wc])
    order = (0, 3, 6, 1, 4, 7, 2, 5, 8)

    for l in range(depth):
        w_perm = jnp.concatenate([w_in[l][:, offs[j]:offs[j + 1]] for j in order], axis=1).astype(BF16)
        qt, k, vt, kmax = _projection(x, mod[l], g_attn[l], w_perm, tables, tm)

        o_a = _diff_attention(qt, k, vt, kmax, diff_lambda[l], diff_subln[l], l, 2 * TILE)
        o_b = _band_attention(
            qt, k, vt, kmax, dil_bias,
            pl.BlockSpec(dil_bias.shape, lambda b, p, i: (0, 0, 0)),
            jnp.float32(math.log2(len(DIL_PATTERNS))),
            dil_offsets, False, DIL_Q_TILES, N_HEADS_DIFF // 2, N_HEADS_DIL // 2, "dilated_attention")
        o_c = _band_attention(
            qt, k, vt, kmax, _na_bias(na_rpb[l], rows),
            pl.BlockSpec((3, 1, 3 * TILE, 2 * TILE), lambda b, p, i: (0, p, 0, 0)),
            jnp.max(na_rpb[l]) * LOG2E,
            (-1, 0, 1), True, NA_Q_TILES, (N_HEADS_DIFF + N_HEADS_DIL) // 2, N_HEADS_NA // 2,
            "neighbourhood_attention")

        x = _out_projection(x, mod[l], o_a, o_b, o_c, w_out[l].astype(BF16), tm)

        a = _ffn_up(x, mod[l], g_ffn[l], w_up[l].astype(BF16), conv_w[l], conv_b[l].reshape(1, d_ff), tm)
        x = _ffn_down(x, mod[l], a, w_down[l].astype(BF16), g_final, l == depth - 1, tm)
    return x
```

```python
import functools
import math

import numpy as np
import jax
import jax.numpy as jnp
from jax import lax
from jax.experimental import pallas as pl
from jax.experimental.pallas import tpu as pltpu

F32 = jnp.float32
BF16 = jnp.bfloat16

HEAD_DIM = 64
N_HEADS_DIFF = 4
N_HEADS_DIL = 6
N_HEADS_NA = 6
W_DIFF = N_HEADS_DIFF * HEAD_DIM
W_DIL = N_HEADS_DIL * HEAD_DIM
W_NA = N_HEADS_NA * HEAD_DIM
DIFF_QK_DIM = HEAD_DIM // 2
DIL_PATTERNS = ((128, 1), (512, 4), (2048, 16))
GRID_W = 64
NA_KH = 8
NA_KW = 16
CONV_W = 3
ROPE_THETA = 10000.0
EPS = 1e-6
NEG = -1e30

LANES = 128
PAIR_W = 2 * HEAD_DIM
TILE = 256
ROW_TILE = 1024
FF_CHUNK = 256
HALO = 16
VMEM_LIMIT = 56 * 1024 * 1024
LOG2E = math.log2(math.e)
SUM_ROWS = 16
LAG_MARGIN = 64.0
DIL_Q_TILES = 8
NA_Q_TILES = 8
LAG_UNROLL = 63
QK_AHEAD = 4


def _cparams(sem, fusible_input=None, n_inputs=0):
    fuse = None if fusible_input is None else [j == fusible_input for j in range(n_inputs)]
    return pltpu.CompilerParams(dimension_semantics=sem, vmem_limit_bytes=VMEM_LIMIT, allow_input_fusion=fuse)


def _mod_kernel(ct_ref, w_ref, b_ref, o_ref):
    ct = ct_ref[...]
    s = ct * jax.nn.sigmoid(ct)
    w = w_ref[0]
    rows = [jnp.sum(w * s[:, b:b + 1], axis=0, keepdims=True) for b in range(ct.shape[1])]
    o_ref[0] = jnp.concatenate(rows, axis=0) + b_ref[0]


def _modulation(c, w_ada, b_ada):
    depth, d, n = w_ada.shape
    bsz = c.shape[0]
    tn = 768
    return pl.pallas_call(
        _mod_kernel,
        grid=(depth, n // tn),
        in_specs=[
            pl.BlockSpec((d, bsz), lambda l, j: (0, 0)),
            pl.BlockSpec((1, d, tn), lambda l, j: (l, 0, j)),
            pl.BlockSpec((1, 1, tn), lambda l, j: (l, 0, j)),
        ],
        out_specs=pl.BlockSpec((1, bsz, tn), lambda l, j: (l, 0, j)),
        out_shape=jax.ShapeDtypeStruct((depth, bsz, n), F32),
        compiler_params=_cparams(("arbitrary", "arbitrary")),
        name="adaln_mod",
    )(c.T, w_ada, b_ada.reshape(depth, 1, n))


def _norm_mod(x, g, sc, sh):
    ms = jnp.mean(x * x, axis=-1, keepdims=True)
    return (x * lax.rsqrt(ms + EPS)) * g * (1.0 + sc) + sh


def _rope_slab(acc, cos, sin, group, width):
    half = group // 2
    lane = lax.broadcasted_iota(jnp.int32, (1, LANES), 1) % group
    first = lane < half
    outs = []
    for c in range(width // LANES):
        xc = acc[:, c * LANES:(c + 1) * LANES]
        swapped = jnp.where(first, pltpu.roll(xc, LANES - half, axis=1), pltpu.roll(xc, half, axis=1))
        outs.append(xc * cos + swapped * sin)
    if width < acc.shape[1]:
        outs.append(acc[:, width:])
    return jnp.concatenate(outs, axis=1)


def _proj_kernel(x_ref, mod_ref, g_ref, w_ref, ca_ref, sa_ref, cb_ref, sb_ref,
                 qt_ref, k_ref, vt_ref, kmax_ref):
    tm = x_ref.shape[1]
    h = _norm_mod(x_ref[0], g_ref[...], mod_ref[0, 1:2, :], mod_ref[0, 0:1, :]).astype(BF16)
    ca, sa, cb, sb = ca_ref[...], sa_ref[...], cb_ref[...], sb_ref[...]
    width = W_DIFF + W_DIL + W_NA
    slabs = ((0, W_DIFF), (W_DIFF, W_DIL + W_NA))

    def slab(part, si):
        off, w = slabs[si]
        acc = jnp.dot(h, w_ref[:, part * width + off: part * width + off + w], preferred_element_type=F32)
        if part < 2 and si == 0:
            acc = _rope_slab(acc, ca, sa, DIFF_QK_DIM, W_DIFF)
        elif part < 2 and si == 1:
            acc = _rope_slab(acc, cb, sb, HEAD_DIM, W_DIL)
        return acc

    q_scale = (LOG2E * DIFF_QK_DIM ** -0.5, LOG2E * HEAD_DIM ** -0.5)
    for si, (off, w) in enumerate(slabs):
        q = slab(0, si) * q_scale[si]
        qt_ref[0, off:off + w, :] = q.T.astype(BF16)
        kb = slab(1, si).astype(BF16)
        k_ref[0, :, off:off + w] = kb
        kmax_ref[0, 0, :, off:off + w] = jnp.max(jnp.abs(kb.astype(F32)), axis=0, keepdims=True)
        v = slab(2, si)
        for t in range(tm // TILE):
            vt_ref[0, t, off:off + w, :] = v[t * TILE:(t + 1) * TILE, :].T.astype(BF16)


def _rope_tables(s, group):
    half = group // 2
    inv = ROPE_THETA ** (-jnp.arange(half, dtype=F32) / half)
    ang = inv[:, None] * jnp.arange(s, dtype=F32)[None, :]
    cos, sin = jnp.cos(ang), jnp.sin(ang)
    reps = LANES // group
    return (jnp.tile(jnp.concatenate([cos, cos], axis=0), (reps, 1)).T,
            jnp.tile(jnp.concatenate([-sin, sin], axis=0), (reps, 1)).T)


def _projection(x, mod_l, g, w_perm, tables, tm):
    bsz, s, d = x.shape
    width = W_DIFF + W_DIL + W_NA
    tab_spec = pl.BlockSpec((tm, LANES), lambda b, i: (i, 0))
    return pl.pallas_call(
        _proj_kernel,
        grid=(bsz, s // tm),
        in_specs=[
            pl.BlockSpec((1, tm, d), lambda b, i: (b, i, 0)),
            pl.BlockSpec((1, 6, d), lambda b, i: (b, 0, 0)),
            pl.BlockSpec((1, d), lambda b, i: (0, 0)),
            pl.BlockSpec((d, 3 * width), lambda b, i: (0, 0)),
            tab_spec, tab_spec, tab_spec, tab_spec,
        ],
        out_specs=[
            pl.BlockSpec((1, width, tm), lambda b, i: (b, 0, i)),
            pl.BlockSpec((1, tm, width), lambda b, i: (b, i, 0)),
            pl.BlockSpec((1, tm // TILE, width, TILE), lambda b, i: (b, i, 0, 0)),
            pl.BlockSpec((1, 1, 1, width), lambda b, i: (b, i, 0, 0)),
        ],
        out_shape=[
            jax.ShapeDtypeStruct((bsz, width, s), BF16),
            jax.ShapeDtypeStruct((bsz, s, width), BF16),
            jax.ShapeDtypeStruct((bsz, s // TILE, width, TILE), BF16),
            jax.ShapeDtypeStruct((bsz, s // tm, 1, width), F32),
        ],
        compiler_params=_cparams(("arbitrary", "arbitrary")),
        name="norm_proj_rope",
    )(x, mod_l, g.reshape(1, d), w_perm, *tables)


def _masked_queries(qt, n_split):
    row = lax.broadcasted_iota(jnp.int32, (PAIR_W, 1), 0)
    step = PAIR_W // n_split
    return [jnp.where((row >= j * step) & (row < (j + 1) * step), qt, jnp.zeros_like(qt))
            for j in range(n_split)]


def _v_with_ones(vtb, hd):
    ones = jnp.ones((SUM_ROWS, vtb.shape[1]), BF16)
    return jnp.concatenate([vtb[hd * HEAD_DIM:(hd + 1) * HEAD_DIM, :], ones], axis=0)


def _score_bound(kmax_ref, q_s):
    kmax = jnp.max(kmax_ref[0], axis=0) * (1.0 + 2.0 ** -7)
    kmax = jnp.broadcast_to(kmax, (SUM_ROWS, PAIR_W)).astype(BF16)
    return jnp.dot(kmax, jnp.abs(q_s[...]), preferred_element_type=F32)[0:1] * (1.0 + 2.0 ** -7)


def _diff_kernel(lam_init, dl_ref, g_ref, kmax_ref, qt_ref, k_ref, vt_ref, o_ref, q_s, s_s, m_s, acc_s):
    tq = qt_ref.shape[2]
    n_kt = vt_ref.shape[1]
    for j, qm in enumerate(_masked_queries(qt_ref[0], 4)):
        q_s[:, j * tq:(j + 1) * tq] = qm
    lane_tiles = [slice(nt * TILE, (nt + 1) * TILE) for nt in range(4 * tq // TILE)]

    def k_tile(kt):
        return k_ref[0, pl.ds(pl.multiple_of(kt * TILE, TILE), TILE), :]

    def v_tiles(kt):
        vtb = vt_ref[0, kt]
        return [_v_with_ones(vtb, hd) for hd in range(2)]

    def exact_step(kt, first):
        kb, v_ext = k_tile(kt), v_tiles(kt)
        for ln in lane_tiles:
            s_s[:, ln] = jnp.dot(kb, q_s[:, ln], preferred_element_type=F32)
        for ln in lane_tiles:
            tile_max = jnp.max(s_s[:, ln], axis=0, keepdims=True)
            m_new = tile_max if first else jnp.maximum(m_s[:, ln], tile_max)
            e = jnp.exp2(s_s[:, ln] - m_new).astype(BF16)
            pv = jnp.dot(v_ext[ln.start // (2 * tq)], e, preferred_element_type=F32)
            acc_s[:, ln] = pv if first else acc_s[:, ln] * jnp.exp2(m_s[:, ln] - m_new) + pv
            m_s[:, ln] = m_new

    def fixed_shift_steps(kts):
        def finish(vtb, ln, s):
            hd = ln.start // (2 * tq)
            e = jnp.exp2(s - m_s[:, ln])
            acc_s[0:HEAD_DIM, ln] += jnp.dot(vtb[hd * HEAD_DIM:(hd + 1) * HEAD_DIM, :], e.astype(BF16),
                                             preferred_element_type=F32)
            acc_s[HEAD_DIM:HEAD_DIM + 1, ln] += jnp.sum(e, axis=0, keepdims=True)

        pending = []
        for kt in kts:
            kb, vtb = k_tile(kt), vt_ref[0, kt]
            for ln in lane_tiles:
                pending.append((vtb, ln, jnp.dot(kb, q_s[:, ln], preferred_element_type=F32)))
                if len(pending) > QK_AHEAD:
                    finish(*pending.pop(0))
        for item in pending:
            finish(*item)

    exact_step(0, True)
    lag_ok = jnp.max(_score_bound(kmax_ref, q_s) - m_s[...]) < LAG_MARGIN

    @pl.when(lag_ok)
    def _():
        def body(tt, carry):
            fixed_shift_steps([LAG_UNROLL * tt + 1 + u for u in range(LAG_UNROLL)])
            return carry
        n_trips = (n_kt - 1) // LAG_UNROLL
        lax.fori_loop(0, n_trips, body, 0)
        if n_trips * LAG_UNROLL + 1 < n_kt:
            fixed_shift_steps(list(range(n_trips * LAG_UNROLL + 1, n_kt)))

    @pl.when(jnp.logical_not(lag_ok))
    def _():
        def body(kt, carry):
            exact_step(kt, False)
            return carry
        lax.fori_loop(1, n_kt, body, 0)

    dl = dl_ref[...]
    lam = (jnp.exp(jnp.sum(dl[0:1] * dl[1:2], axis=1, keepdims=True))
           - jnp.exp(jnp.sum(dl[2:3] * dl[3:4], axis=1, keepdims=True)) + lam_init)
    g = g_ref[...]
    outs = []
    for hd in range(2):
        p = [acc_s[0:HEAD_DIM, (2 * hd + mp) * tq:(2 * hd + mp + 1) * tq]
             / acc_s[HEAD_DIM:HEAD_DIM + 1, (2 * hd + mp) * tq:(2 * hd + mp + 1) * tq] for mp in range(2)]
        o = p[0] - lam * p[1]
        ms = jnp.mean(o * o, axis=0, keepdims=True)
        outs.append((o * lax.rsqrt(ms + EPS)) * g * (1.0 - lam_init))
    o_ref[0] = jnp.concatenate(outs, axis=0).T.astype(BF16)


def _diff_attention(qt, k, vt, kmax, diff_lambda, subln, layer_idx, tq):
    bsz, _, s = qt.shape
    n_pairs = N_HEADS_DIFF // 2
    lam_init = 0.8 - 0.6 * math.exp(-0.3 * layer_idx)
    return pl.pallas_call(
        functools.partial(_diff_kernel, lam_init),
        grid=(bsz, n_pairs, s // tq),
        in_specs=[
            pl.BlockSpec(diff_lambda.shape, lambda b, p, i: (0, 0)),
            pl.BlockSpec((HEAD_DIM, 1), lambda b, p, i: (0, 0)),
            pl.BlockSpec((1, kmax.shape[1], 1, PAIR_W), lambda b, p, i: (b, 0, 0, p)),
            pl.BlockSpec((1, PAIR_W, tq), lambda b, p, i: (b, p, i)),
            pl.BlockSpec((1, s, PAIR_W), lambda b, p, i: (b, 0, p)),
            pl.BlockSpec((1, s // TILE, PAIR_W, TILE), lambda b, p, i: (b, 0, p, 0)),
        ],
        out_specs=pl.BlockSpec((1, tq, PAIR_W), lambda b, p, i: (b, i, p)),
        out_shape=jax.ShapeDtypeStruct((bsz, s, W_DIFF), BF16),
        scratch_shapes=[
            pltpu.VMEM((PAIR_W, 4 * tq), BF16),
            pltpu.VMEM((TILE, 4 * tq), F32),
            pltpu.VMEM((1, 4 * tq), F32),
            pltpu.VMEM((HEAD_DIM + SUM_ROWS, 4 * tq), F32),
        ],
        compiler_params=_cparams(("arbitrary", "arbitrary", "arbitrary")),
        name="diff_attention",
    )(diff_lambda, subln.reshape(HEAD_DIM, 1), kmax, qt, k, vt)


def _band_kernel(offsets, window_bias, bmax_ref, kmax_ref, qt_ref, k_ref, vt_ref, bias_ref, o_ref,
                 q_s, sc_s, s_s, acc_s):
    i = pl.program_id(2)
    n_blk = vt_ref.shape[1]
    n_off = len(offsets)
    centre = offsets.index(0)
    n_qt = qt_ref.shape[2] // TILE
    qt = qt_ref[0]
    for t in range(n_qt):
        for hd, qm in enumerate(_masked_queries(qt[:, t * TILE:(t + 1) * TILE], 2)):
            q_s[:, (2 * t + hd) * TILE:(2 * t + hd + 1) * TILE] = qm
    lane_tiles = [slice(c * TILE, (c + 1) * TILE) for c in range(2 * n_qt)]

    def q_tile(ln):
        return i * n_qt + ln.start // (2 * TILE)

    def key_tile(oi, ln):
        return jnp.clip(q_tile(ln) + offsets[oi], 0, n_blk - 1)

    def scores(oi, ln):
        qi = q_tile(ln)
        head_lanes = slice(ln.start % (2 * TILE), ln.start % (2 * TILE) + TILE)
        if window_bias:
            variant = jnp.where(qi == 0, 0, jnp.where(qi == n_blk - 1, 2, 1))
            bias = bias_ref[variant, 0, oi * TILE:(oi + 1) * TILE, head_lanes]
        else:
            kbi = qi + offsets[oi]
            bias = bias_ref[jnp.where((kbi >= 0) & (kbi < n_blk), oi, n_off), :, head_lanes]
        kb = k_ref[0, pl.ds(pl.multiple_of(key_tile(oi, ln) * TILE, TILE), TILE), :]
        return jnp.dot(kb, q_s[:, ln], preferred_element_type=F32) + bias

    def weighted_values(oi, ln, e):
        vtb = vt_ref[0, key_tile(oi, ln)]
        return jnp.dot(_v_with_ones(vtb, (ln.start // TILE) % 2), e, preferred_element_type=F32)

    for ln in lane_tiles:
        sc_s[:, ln] = scores(centre, ln)
    m_c = jnp.max(sc_s[...], axis=0, keepdims=True)
    shift_ok = jnp.max(_score_bound(kmax_ref, q_s) + bmax_ref[...] - m_c) < LAG_MARGIN

    @pl.when(shift_ok)
    def _():
        acc = {ln.start: None for ln in lane_tiles}

        def finish(oi, ln, s):
            pv = weighted_values(oi, ln, jnp.exp2(s - m_c[:, ln]).astype(BF16))
            acc[ln.start] = pv if acc[ln.start] is None else acc[ln.start] + pv

        pending = []
        for oi in range(n_off):
            for ln in lane_tiles:
                pending.append((oi, ln, sc_s[:, ln] if oi == centre else scores(oi, ln)))
                if len(pending) > QK_AHEAD:
                    finish(*pending.pop(0))
        for item in pending:
            finish(*item)
        for ln in lane_tiles:
            acc_s[:, ln] = acc[ln.start]

    @pl.when(jnp.logical_not(shift_ok))
    def _():
        for ln in lane_tiles:
            col_max = m_c[:, ln]
            for oi in range(n_off):
                s_s[oi] = sc_s[:, ln] if oi == centre else scores(oi, ln)
                col_max = jnp.maximum(col_max, jnp.max(s_s[oi], axis=0, keepdims=True))
            acc = None
            for oi in range(n_off):
                pv = weighted_values(oi, ln, jnp.exp2(s_s[oi] - col_max).astype(BF16))
                acc = pv if acc is None else acc + pv
            acc_s[:, ln] = acc

    out = acc_s[0:HEAD_DIM, :] / acc_s[HEAD_DIM:HEAD_DIM + 1, :]
    for t in range(n_qt):
        pair = jnp.concatenate([out[:, lane_tiles[2 * t]], out[:, lane_tiles[2 * t + 1]]], axis=0)
        o_ref[0, t * TILE:(t + 1) * TILE, :] = pair.T.astype(BF16)


def _band_attention(qt, k, vt, kmax, bias, bias_spec, bias_max, offsets, window_bias, q_tiles, pair0, n_pairs,
                    name):
    bsz, _, s = qt.shape
    tq = q_tiles * TILE
    return pl.pallas_call(
        functools.partial(_band_kernel, offsets, window_bias),
        grid=(bsz, n_pairs, s // tq),
        in_specs=[
            pl.BlockSpec((1, 1), lambda b, p, i: (0, 0)),
            pl.BlockSpec((1, kmax.shape[1], 1, PAIR_W), lambda b, p, i: (b, 0, 0, pair0 + p)),
            pl.BlockSpec((1, PAIR_W, tq), lambda b, p, i: (b, pair0 + p, i)),
            pl.BlockSpec((1, s, PAIR_W), lambda b, p, i: (b, 0, pair0 + p)),
            pl.BlockSpec((1, s // TILE, PAIR_W, TILE), lambda b, p, i: (b, 0, pair0 + p, 0)),
            bias_spec,
        ],
        out_specs=pl.BlockSpec((1, tq, PAIR_W), lambda b, p, i: (b, i, p)),
        out_shape=jax.ShapeDtypeStruct((bsz, s, n_pairs * PAIR_W), BF16),
        scratch_shapes=[
            pltpu.VMEM((PAIR_W, 2 * tq), BF16),
            pltpu.VMEM((TILE, 2 * tq), F32),
            pltpu.VMEM((len(offsets), TILE, TILE), F32),
            pltpu.VMEM((HEAD_DIM + SUM_ROWS, 2 * tq), F32),
        ],
        compiler_params=_cparams(("arbitrary", "arbitrary", "arbitrary")),
        name=name,
    )(jnp.reshape(bias_max, (1, 1)).astype(F32), kmax, qt, k, vt, bias)


def _dilated_bias():
    reach = max(w // 2 for w, _ in DIL_PATTERNS)
    n_off = -(-reach // TILE)
    offsets = tuple(range(-n_off, n_off + 1))
    kj = np.arange(TILE)[:, None]
    qi = np.arange(TILE)[None, :]
    tabs = []
    for off in offsets:
        delta = off * TILE + kj - qi
        cnt = np.zeros((TILE, TILE), np.int32)
        for window, dil in DIL_PATTERNS:
            cnt += ((delta % dil == 0) & (np.abs(delta) <= window // 2)).astype(np.int32)
        tabs.append(np.where(cnt > 0, np.log2(np.maximum(cnt, 1).astype(np.float64)), NEG))
    tabs.append(np.full((TILE, TILE), NEG))
    return offsets, jnp.asarray(np.tile(np.stack(tabs), (1, 1, 2)), F32)


def _na_bias_kernel(rows, rpb_ref, o_ref):
    hd = pl.program_id(0)
    kh = min(NA_KH, rows)
    q_rows = TILE // GRID_W
    kc = lax.broadcasted_iota(jnp.int32, (GRID_W, LANES), 0)
    lane = lax.broadcasted_iota(jnp.int32, (GRID_W, LANES), 1)
    qc = lane % GRID_W
    dc = jnp.clip(kc - qc, -(NA_KW - 1), NA_KW - 1) + (NA_KW - 1)
    cs = jnp.clip(qc - NA_KW // 2, 0, GRID_W - NA_KW)
    col_ok = (kc >= cs) & (kc < cs + NA_KW)
    neg = jnp.full((GRID_W, LANES), NEG, F32)
    tiles = []
    for dr in range(2 * NA_KH - 1):
        t = neg
        for j in range(2 * NA_KW - 1):
            t = jnp.where(dc == j, rpb_ref[hd, dr, j] * LOG2E, t)
        tiles.append(jnp.where(col_ok, t, neg))
    for v, r0 in enumerate((0, 2 * q_rows, rows - q_rows)):
        for kr_rel in range(3 * q_rows):
            kr = r0 - q_rows + kr_rel
            for pair in range(q_rows // 2):
                halves = []
                for r in (r0 + 2 * pair, r0 + 2 * pair + 1):
                    rs = min(max(r - kh // 2, 0), rows - kh)
                    halves.append(tiles[kr - r + NA_KH - 1] if rs <= kr < rs + kh else neg)
                blk = jnp.where(lane < GRID_W, halves[0], halves[1])
                o_ref[v, 0, kr_rel * GRID_W:(kr_rel + 1) * GRID_W, pair * LANES:(pair + 1) * LANES] = blk


def _na_bias(rpb, rows):
    n_heads = rpb.shape[0]
    return pl.pallas_call(
        functools.partial(_na_bias_kernel, rows),
        grid=(n_heads,),
        in_specs=[pl.BlockSpec(memory_space=pltpu.SMEM)],
        out_specs=pl.BlockSpec((3, 1, 3 * TILE, TILE), lambda h: (0, h // 2, 0, h % 2)),
        out_shape=jax.ShapeDtypeStruct((3, n_heads // 2, 3 * TILE, 2 * TILE), F32),
        compiler_params=_cparams(("arbitrary",)),
        name="na_bias_table",
    )(rpb)


def _outproj_kernel(x_ref, mod_ref, oa_ref, ob_ref, oc_ref, w_ref, y_ref):
    o = jnp.concatenate([oa_ref[0], ob_ref[0], oc_ref[0]], axis=1)
    mix = jnp.dot(o, w_ref[...], preferred_element_type=F32)
    y_ref[0] = x_ref[0] + mod_ref[0, 2:3, :] * mix


def _out_projection(x, mod_l, o_a, o_b, o_c, w_out, tm):
    bsz, s, d = x.shape
    row = lambda w: pl.BlockSpec((1, tm, w), lambda b, i: (b, i, 0))
    return pl.pallas_call(
        _outproj_kernel,
        grid=(bsz, s // tm),
        in_specs=[
            row(d),
            pl.BlockSpec((1, 6, d), lambda b, i: (b, 0, 0)),
            row(W_DIFF), row(W_DIL), row(W_NA),
            pl.BlockSpec(w_out.shape, lambda b, i: (0, 0)),
        ],
        out_specs=row(d),
        out_shape=jax.ShapeDtypeStruct((bsz, s, d), F32),
        compiler_params=_cparams(("arbitrary", "arbitrary"), fusible_input=5, n_inputs=6),
        name="out_proj_residual",
    )(x, mod_l, o_a, o_b, o_c, w_out)


def _ffn_up_kernel(x_ref, xp_ref, xn_ref, mod_ref, g_ref, w_ref, cw_ref, cb_ref, a_ref, h_s, g_s, u_s):
    i = pl.program_id(1)
    tm = x_ref.shape[1]
    g, sc, sh = g_ref[...], mod_ref[0, 4:5, :], mod_ref[0, 3:4, :]
    keep_prev = (i > 0).astype(F32)
    keep_next = (i < pl.num_programs(1) - 1).astype(F32)
    h_s[0:HALO, :] = (_norm_mod(xp_ref[0], g, sc, sh) * keep_prev).astype(BF16)
    h_s[HALO:HALO + tm, :] = _norm_mod(x_ref[0], g, sc, sh).astype(BF16)
    h_s[HALO + tm:, :] = (_norm_mod(xn_ref[0], g, sc, sh) * keep_next).astype(BF16)
    d_ff = cw_ref.shape[1]
    n_ch = d_ff // FF_CHUNK

    def matmuls(c):
        cols = slice(c * FF_CHUNK, (c + 1) * FF_CHUNK)
        g_s[c % 2] = jnp.dot(h_s[...], w_ref[:, cols], preferred_element_type=F32)
        u_s[c % 2] = jnp.dot(h_s[HALO:HALO + tm, :], w_ref[:, d_ff + cols.start:d_ff + cols.stop],
                             preferred_element_type=F32)

    def gate(c):
        cw = cw_ref[:, c * FF_CHUNK:(c + 1) * FF_CHUNK]
        gc = cb_ref[:, c * FF_CHUNK:(c + 1) * FF_CHUNK]
        for t in range(CONV_W):
            gc = gc + g_s[c % 2, pl.ds(HALO - CONV_W // 2 + t, tm), :] * cw[t:t + 1, :]
        a_ref[0, :, c * FF_CHUNK:(c + 1) * FF_CHUNK] = (gc * jax.nn.sigmoid(gc) * u_s[c % 2]).astype(BF16)

    matmuls(0)
    for c in range(n_ch):
        if c + 1 < n_ch:
            matmuls(c + 1)
        gate(c)


def _ffn_up(x, mod_l, g, w_up, cw, cb, tm):
    bsz, s, d = x.shape
    d_ff = cw.shape[1]
    hb = tm // HALO
    n_hb = s // HALO
    return pl.pallas_call(
        _ffn_up_kernel,
        grid=(bsz, s // tm),
        in_specs=[
            pl.BlockSpec((1, tm, d), lambda b, i: (b, i, 0)),
            pl.BlockSpec((1, HALO, d), lambda b, i: (b, jnp.maximum(i * hb - 1, 0), 0)),
            pl.BlockSpec((1, HALO, d), lambda b, i: (b, jnp.minimum((i + 1) * hb, n_hb - 1), 0)),
            pl.BlockSpec((1, 6, d), lambda b, i: (b, 0, 0)),
            pl.BlockSpec((1, d), lambda b, i: (0, 0)),
            pl.BlockSpec(w_up.shape, lambda b, i: (0, 0)),
            pl.BlockSpec(cw.shape, lambda b, i: (0, 0)),
            pl.BlockSpec(cb.shape, lambda b, i: (0, 0)),
        ],
        out_specs=pl.BlockSpec((1, tm, d_ff), lambda b, i: (b, i, 0)),
        out_shape=jax.ShapeDtypeStruct((bsz, s, d_ff), BF16),
        scratch_shapes=[
            pltpu.VMEM((tm + 2 * HALO, d), BF16),
            pltpu.VMEM((2, tm + 2 * HALO, FF_CHUNK), F32),
            pltpu.VMEM((2, tm, FF_CHUNK), F32),
        ],
        compiler_params=_cparams(("arbitrary", "arbitrary"), fusible_input=5, n_inputs=8),
        name="ffn_up_conv_glu",
    )(x, x, x, mod_l, g.reshape(1, d), w_up, cw, cb)


def _ffn_down_kernel(final, x_ref, mod_ref, a_ref, w_ref, gf_ref, y_ref):
    y = x_ref[0] + mod_ref[0, 5:6, :] * jnp.dot(a_ref[0], w_ref[...], preferred_element_type=F32)
    if final:
        ms = jnp.mean(y * y, axis=-1, keepdims=True)
        y = (y * lax.rsqrt(ms + EPS)) * gf_ref[...]
    y_ref[0] = y


def _ffn_down(x, mod_l, a, w_down, g_final, final, tm):
    bsz, s, d = x.shape
    return pl.pallas_call(
        functools.partial(_ffn_down_kernel, final),
        grid=(bsz, s // tm),
        in_specs=[
            pl.BlockSpec((1, tm, d), lambda b, i: (b, i, 0)),
            pl.BlockSpec((1, 6, d), lambda b, i: (b, 0, 0)),
            pl.BlockSpec((1, tm, a.shape[2]), lambda b, i: (b, i, 0)),
            pl.BlockSpec(w_down.shape, lambda b, i: (0, 0)),
            pl.BlockSpec((1, d), lambda b, i: (0, 0)),
        ],
        out_specs=pl.BlockSpec((1, tm, d), lambda b, i: (b, i, 0)),
        out_shape=jax.ShapeDtypeStruct((bsz, s, d), F32),
        compiler_params=_cparams(("arbitrary", "arbitrary"), fusible_input=3, n_inputs=5),
        name="ffn_down_residual",
    )(x, mod_l, a, w_down, g_final.reshape(1, d))


def kernel(x, c, w_ada, b_ada, g_attn, w_in, diff_lambda, diff_subln, na_rpb, w_out, g_ffn, w_up,
           conv_w, conv_b, w_down, g_final):
    bsz, s, d = x.shape
    depth = w_ada.shape[0]
    d_ff = w_down.shape[1]
    tm = ROW_TILE
    assert s % tm == 0 and tm % TILE == 0 and d_ff % FF_CHUNK == 0 and TILE % (2 * GRID_W) == 0
    assert (s // TILE) % max(DIL_Q_TILES, NA_Q_TILES) == 0 and s // GRID_W >= 4 * (TILE // GRID_W)

    mod = _modulation(c, w_ada, b_ada).reshape(depth, bsz, 6, d)
    tables = _rope_tables(s, DIFF_QK_DIM) + _rope_tables(s, HEAD_DIM)
    dil_offsets, dil_bias = _dilated_bias()
    rows = s // GRID_W

    wa, wb, wc = W_DIFF, W_DIL, W_NA
    offs = np.cumsum([0, wa, wa, wa, wb, wb, wb, wc, wc, wc])
    order = (0, 3, 6, 1, 4, 7, 2, 5, 8)

    for l in range(depth):
        w_perm = jnp.concatenate([w_in[l][:, offs[j]:offs[j + 1]] for j in order], axis=1).astype(BF16)
        qt, k, vt, kmax = _projection(x, mod[l], g_attn[l], w_perm, tables, tm)

        o_a = _diff_attention(qt, k, vt, kmax, diff_lambda[l], diff_subln[l], l, 2 * TILE)
        o_b = _band_attention(
            qt, k, vt, kmax, dil_bias,
            pl.BlockSpec(dil_bias.shape, lambda b, p, i: (0, 0, 0)),
            jnp.float32(math.log2(len(DIL_PATTERNS))),
            dil_offsets, False, DIL_Q_TILES, N_HEADS_DIFF // 2, N_HEADS_DIL // 2, "dilated_attention")
        o_c = _band_attention(
            qt, k, vt, kmax, _na_bias(na_rpb[l], rows),
            pl.BlockSpec((3, 1, 3 * TILE, 2 * TILE), lambda b, p, i: (0, p, 0, 0)),
            jnp.max(na_rpb[l]) * LOG2E,
            (-1, 0, 1), True, NA_Q_TILES, (N_HEADS_DIFF + N_HEADS_DIL) // 2, N_HEADS_NA // 2,
            "neighbourhood_attention")

        x = _out_projection(x, mod[l], o_a, o_b, o_c, w_out[l].astype(BF16), tm)

        a = _ffn_up(x, mod[l], g_ffn[l], w_up[l].astype(BF16), conv_w[l], conv_b[l].reshape(1, d_ff), tm)
        x = _ffn_down(x, mod[l], a, w_down[l].astype(BF16), g_final, l == depth - 1, tm)
    return x
```
